```python
import math
import jax, jax.numpy as jnp
from jax import lax
import numpy as np

D_MODEL = 1024
BATCH = 2
SEQ = 8192
DEPTH = 2

N_MIXERS = 2
N_HEADS = 8
HEAD_DIM = D_MODEL // N_HEADS
DIFF_HALF = HEAD_DIM // 2
MOBA_BLOCK = 256
MOBA_TOPK = 3
Q_BLOCK = 128
D_FF = -(-8 * D_MODEL // (3 * 256)) * 256
N_BUCKETS = 32
MAX_EXACT = N_BUCKETS // 2
MAX_DISTANCE = 2048
N_MOBA_LAYERS = (DEPTH + 1) // 2
N_DIFF_LAYERS = DEPTH // 2
RMS_EPS = 1e-6
NEG_INF = -1e30

kernel_name = "hybrid_moba_diffattn_sandwich_adaln"


def rms_norm(x, g):
    x32 = x.astype(jnp.float32)
    y = x32 * lax.rsqrt(jnp.mean(x32 * x32, axis=-1, keepdims=True) + RMS_EPS)
    return (y * g.astype(jnp.float32)).astype(x.dtype)


def t5_bucket(rel):
    n = jnp.maximum(rel, 0)
    nf = jnp.maximum(n, 1).astype(jnp.float32)
    large = MAX_EXACT + (jnp.log(nf / MAX_EXACT) / math.log(MAX_DISTANCE / MAX_EXACT)
                         * (N_BUCKETS - MAX_EXACT)).astype(jnp.int32)
    large = jnp.minimum(large, N_BUCKETS - 1)
    return jnp.where(n < MAX_EXACT, n, large)


def moba_attention(h, w_qkv, w_o, rel_table):
    B, S, _ = h.shape
    pad = (-S) % MOBA_BLOCK
    s_pad = S + pad
    nb = s_pad // MOBA_BLOCK
    nq = s_pad // Q_BLOCK
    k_sel_n = min(MOBA_TOPK, nb)
    q, k, v = jnp.split(h @ w_qkv, 3, axis=-1)

    def heads(t):
        t = t.reshape(B, S, N_HEADS, HEAD_DIM).transpose(0, 2, 1, 3)
        return jnp.pad(t, ((0, 0), (0, 0), (0, pad), (0, 0)))

    q = heads(q) * (HEAD_DIM ** -0.5)
    k, v = heads(k), heads(v)
    k_blk = k.reshape(B, N_HEADS, nb, MOBA_BLOCK, HEAD_DIM)
    v_blk = v.reshape(B, N_HEADS, nb, MOBA_BLOCK, HEAD_DIM)
    k_mean = jnp.mean(k_blk, axis=3)
    q_blocks = q.reshape(B, N_HEADS, nq, Q_BLOCK, HEAD_DIM).transpose(2, 0, 1, 3, 4)
    b_idx = jnp.arange(B)[:, None, None, None]
    h_idx = jnp.arange(N_HEADS)[None, :, None, None]
    table_t = rel_table.T

    def one_block(args):
        qb, qi = args
        q_pos = qi * Q_BLOCK + jnp.arange(Q_BLOCK)
        own = qi // (MOBA_BLOCK // Q_BLOCK)
        gate = jnp.einsum('bhqd,bhnd->bhqn', qb, k_mean).astype(jnp.float32)
        gate = jnp.where(jnp.arange(nb) < own, gate, NEG_INF)
        _, sel = lax.top_k(gate, k_sel_n)
        slot_valid = jnp.repeat(jnp.arange(k_sel_n) < own, MOBA_BLOCK)
        k_sel = k_blk[b_idx, h_idx, sel].reshape(B, N_HEADS, Q_BLOCK, k_sel_n * MOBA_BLOCK, HEAD_DIM)
        v_sel = v_blk[b_idx, h_idx, sel].reshape(B, N_HEADS, Q_BLOCK, k_sel_n * MOBA_BLOCK, HEAD_DIM)
        k_pos_sel = (sel[..., None] * MOBA_BLOCK + jnp.arange(MOBA_BLOCK)).reshape(
            B, N_HEADS, Q_BLOCK, k_sel_n * MOBA_BLOCK)
        bias_sel = table_t[h_idx, t5_bucket(q_pos[None, None, :, None] - k_pos_sel)]
        s_sel = jnp.einsum('bhqd,bhqkd->bhqk', qb, k_sel).astype(jnp.float32) + bias_sel.astype(jnp.float32)
        s_sel = jnp.where(slot_valid, s_sel, NEG_INF)
        k_own = lax.dynamic_slice_in_dim(k, own * MOBA_BLOCK, MOBA_BLOCK, axis=2)
        v_own = lax.dynamic_slice_in_dim(v, own * MOBA_BLOCK, MOBA_BLOCK, axis=2)
        k_pos_own = own * MOBA_BLOCK + jnp.arange(MOBA_BLOCK)
        rel_own = q_pos[:, None] - k_pos_own[None, :]
        bias_own = jnp.moveaxis(rel_table[t5_bucket(rel_own)], -1, 0)
        s_own = jnp.einsum('bhqd,bhkd->bhqk', qb, k_own).astype(jnp.float32) + bias_own[None].astype(jnp.float32)
        s_own = jnp.where(rel_own >= 0, s_own, NEG_INF)
        p = jax.nn.softmax(jnp.concatenate([s_sel, s_own], axis=-1), axis=-1).astype(v.dtype)
        n_sel = k_sel_n * MOBA_BLOCK
        return (jnp.einsum('bhqk,bhqkd->bhqd', p[..., :n_sel], v_sel)
                + jnp.einsum('bhqk,bhkd->bhqd', p[..., n_sel:], v_own))

    o = lax.map(one_block, (q_blocks, jnp.arange(nq)))
    o = o.transpose(1, 0, 3, 2, 4).reshape(B, s_pad, D_MODEL)[:, :S]
    return o @ w_o


def diff_attention(h, w_qkv, w_o, lam, subln_g, rel_table, layer_idx):
    B, S, _ = h.shape
    nq = S // Q_BLOCK
    lambda_init = 0.8 - 0.6 * math.exp(-0.3 * layer_idx)
    q, k, v = jnp.split(h @ w_qkv, 3, axis=-1)
    q = q.reshape(B, S, N_HEADS, 2, DIFF_HALF).transpose(0, 2, 3, 1, 4) * (DIFF_HALF ** -0.5)
    k = k.reshape(B, S, N_HEADS, 2, DIFF_HALF).transpose(0, 2, 3, 1, 4)
    v = v.reshape(B, S, N_HEADS, HEAD_DIM).transpose(0, 2, 1, 3)
    lam32 = lam.astype(jnp.float32)
    lam_full = (jnp.exp(jnp.sum(lam32[0] * lam32[1])) - jnp.exp(jnp.sum(lam32[2] * lam32[3]))
                + lambda_init)
    q_blocks = q.reshape(B, N_HEADS, 2, nq, Q_BLOCK, DIFF_HALF).transpose(3, 0, 1, 2, 4, 5)
    k_pos = jnp.arange(S)

    def one_block(args):
        qb, qi = args
        q_pos = qi * Q_BLOCK + jnp.arange(Q_BLOCK)
        rel = q_pos[:, None] - k_pos[None, :]
        bias = jnp.moveaxis(rel_table[t5_bucket(rel)], -1, 0)
        s = jnp.einsum('bhcqd,bhckd->bhcqk', qb, k).astype(jnp.float32) + bias[None, :, None].astype(jnp.float32)
        s = jnp.where(rel >= 0, s, NEG_INF)
        p = jax.nn.softmax(s, axis=-1)
        a = p[:, :, 0] - lam_full * p[:, :, 1]
        return jnp.einsum('bhqk,bhkd->bhqd', a.astype(v.dtype), v)

    o = lax.map(one_block, (q_blocks, jnp.arange(nq)))
    o = o.transpose(1, 2, 0, 3, 4).reshape(B, N_HEADS, S, HEAD_DIM)
    o = rms_norm(o, subln_g) * (1.0 - lambda_init)
    o = o.transpose(0, 2, 1, 3).reshape(B, S, D_MODEL)
    return o @ w_o


def swiglu(h, w_in, w_out):
    g, u = jnp.split(h @ w_in, 2, axis=-1)
    return (jax.nn.silu(g) * u) @ w_out


def setup_inputs(seed: int = 0) -> dict:
    key = jax.random.key(seed)
    ks = jax.random.split(key, 16)
    f32 = jnp.float32
    D = D_MODEL
    return {
        "x": jax.random.normal(ks[0], (BATCH, SEQ, D), f32),
        "c": jax.random.normal(ks[1], (BATCH, D), f32),
        "rel_bias": 0.5 * jax.random.normal(ks[2], (N_BUCKETS, N_HEADS), f32),
        "ada_w": jax.random.normal(ks[3], (DEPTH, D, 6 * D), f32) * D ** -0.5,
        "ada_b": 0.01 * jax.random.normal(ks[4], (DEPTH, 6 * D), f32),
        "norm_g": 1.0 + 0.1 * jax.random.normal(ks[5], (DEPTH, 4, D), f32),
        "moba_w_qkv": jax.random.normal(ks[6], (N_MOBA_LAYERS, D, 3 * D), f32) * D ** -0.5,
        "moba_w_o": jax.random.normal(ks[7], (N_MOBA_LAYERS, D, D), f32) * D ** -0.5,
        "diff_w_qkv": jax.random.normal(ks[8], (N_DIFF_LAYERS, D, 3 * D), f32) * D ** -0.5,
        "diff_w_o": jax.random.normal(ks[9], (N_DIFF_LAYERS, D, D), f32) * D ** -0.5,
        "diff_lambda": 0.1 * jax.random.normal(ks[10], (N_DIFF_LAYERS, 4, DIFF_HALF), f32),
        "diff_subln_g": 1.0 + 0.1 * jax.random.normal(ks[11], (N_DIFF_LAYERS, HEAD_DIM), f32),
        "ffn_w_in": jax.random.normal(ks[12], (DEPTH, D, 2 * D_FF), f32) * D ** -0.5,
        "ffn_w_out": jax.random.normal(ks[13], (DEPTH, D_FF, D), f32) * D_FF ** -0.5,
    }


def reference(x, c, rel_bias, ada_w, ada_b, norm_g, moba_w_qkv, moba_w_o, diff_w_qkv, diff_w_o,
              diff_lambda, diff_subln_g, ffn_w_in, ffn_w_out):
    c_act = jax.nn.silu(c)
    for i in range(DEPTH):
        mod = (c_act @ ada_w[i] + ada_b[i])[:, None, :]
        sh_a, sc_a, g_a, sh_f, sc_f, g_f = jnp.split(mod, 6, axis=-1)
        h = rms_norm(x, norm_g[i, 0]) * (1.0 + sc_a) + sh_a
        if i % N_MIXERS == 0:
            y = moba_attention(h, moba_w_qkv[i // 2], moba_w_o[i // 2], rel_bias)
        else:
            y = diff_attention(h, diff_w_qkv[i // 2], diff_w_o[i // 2], diff_lambda[i // 2],
                               diff_subln_g[i // 2], rel_bias, i)
        x = x + g_a * rms_norm(y, norm_g[i, 1])
        h = rms_norm(x, norm_g[i, 2]) * (1.0 + sc_f) + sh_f
        x = x + g_f * rms_norm(swiglu(h, ffn_w_in[i], ffn_w_out[i]), norm_g[i, 3])
    return x
```

```python
import functools
import math

import numpy as np
import jax
import jax.numpy as jnp
from jax import lax
from jax.experimental import pallas as pl
from jax.experimental.pallas import tpu as pltpu

D_MODEL = 1024
BATCH = 2
SEQ = 8192
DEPTH = 2
N_HEADS = 8
HEAD_DIM = D_MODEL // N_HEADS
DIFF_HALF = HEAD_DIM // 2
MOBA_BLOCK = 256
MOBA_TOPK = 3
D_FF = 2816
N_BUCKETS = 32
MAX_EXACT = N_BUCKETS // 2
MAX_DISTANCE = 2048
RMS_EPS = 1e-6
NEG_INF = -1e30

ROWS = BATCH * SEQ
N_KV_BLOCKS = SEQ // MOBA_BLOCK
TILE = MOBA_BLOCK
VMEM_LIMIT = 48 * 1024 * 1024

F32 = jnp.float32
BF16 = jnp.bfloat16


def _bucket_of_distance():
    n = np.arange(MAX_DISTANCE + 1)
    nf = np.maximum(n, 1).astype(np.float64)
    val = np.log(nf / MAX_EXACT) / math.log(MAX_DISTANCE / MAX_EXACT) * (N_BUCKETS - MAX_EXACT)
    frac = np.abs(val - np.round(val))
    assert np.all((frac > 5e-5) | (n <= MAX_EXACT) | (n == MAX_DISTANCE))
    large = np.minimum(MAX_EXACT + np.floor(val + 1e-9).astype(np.int64), N_BUCKETS - 1)
    return np.where(n < MAX_EXACT, n, large)


_BUCKETS = _bucket_of_distance()
FAR_DISTANCE = int(np.min(np.nonzero(_BUCKETS == N_BUCKETS - 1)[0]))
N_BIAS_TILES = (FAR_DISTANCE + TILE - 1) // TILE + 1


def _bias_bucket_tiles():
    i = np.arange(TILE)[:, None]
    j = np.arange(TILE)[None, :]
    out = []
    for t in range(N_BIAS_TILES):
        rel = t * TILE + i - j
        out.append(np.where(rel >= 0, _BUCKETS[np.clip(rel, 0, MAX_DISTANCE)], -1))
    return np.stack(out).astype(np.int32)


MOD_TN = 512


def _mod_kernel(ct_ref, w_ref, b_ref, o_ref):
    ct = ct_ref[...]
    cact = ct / (1.0 + jnp.exp(-ct))
    w = w_ref[0]
    for b in range(BATCH):
        row = jnp.sum(w * cact[:, b:b + 1], axis=0, keepdims=True)
        o_ref[0, b:b + 1, :] = row + b_ref[0]


def _adaln_mod(c, ada_w, ada_b):
    n_out = ada_w.shape[-1]
    return pl.pallas_call(
        _mod_kernel,
        grid=(DEPTH, n_out // MOD_TN),
        in_specs=[
            pl.BlockSpec((D_MODEL, BATCH), lambda i, n: (0, 0)),
            pl.BlockSpec((1, D_MODEL, MOD_TN), lambda i, n: (i, 0, n)),
            pl.BlockSpec((1, 1, MOD_TN), lambda i, n: (i, 0, n)),
        ],
        out_specs=pl.BlockSpec((1, BATCH, MOD_TN), lambda i, n: (i, 0, n)),
        out_shape=jax.ShapeDtypeStruct((DEPTH, BATCH, n_out), F32),
        name="adaln_mod",
    )(c.T, ada_w, ada_b.reshape(DEPTH, 1, n_out))


def _bias_kernel(tab_ref, bkt_ref, o_ref):
    h = pl.program_id(0)
    bkt = bkt_ref[0]
    far = tab_ref[N_BUCKETS - 1, h]
    acc = jnp.full(bkt.shape, NEG_INF, F32)
    for b in range(N_BUCKETS):
        acc = jnp.where(bkt == b, tab_ref[b, h] - far, acc)
    o_ref[0, 0] = acc


def _bias_tiles(rel_bias):
    bkt = jnp.asarray(_bias_bucket_tiles())
    return pl.pallas_call(
        _bias_kernel,
        grid=(N_HEADS, N_BIAS_TILES),
        in_specs=[
            pl.BlockSpec(memory_space=pltpu.SMEM),
            pl.BlockSpec((1, TILE, TILE), lambda h, t: (t, 0, 0)),
        ],
        out_specs=pl.BlockSpec((1, 1, TILE, TILE), lambda h, t: (h, t, 0, 0)),
        out_shape=jax.ShapeDtypeStruct((N_HEADS, N_BIAS_TILES, TILE, TILE), F32),
        name="bias_tiles",
    )(rel_bias, bkt)


QKV_TM = 512


def _norm_mod(x, g, sc, sh):
    y = x * lax.rsqrt(jnp.mean(x * x, axis=-1, keepdims=True) + RMS_EPS)
    return (y * g) * (1.0 + sc) + sh


def _qkv_kernel(x_ref, g_ref, sc_ref, sh_ref, w_ref, o_ref, *km_ref, q_scale):
    hb = _norm_mod(x_ref[...], g_ref[...], sc_ref[0], sh_ref[0]).astype(BF16)
    for n in range(3):
        r = jnp.dot(hb, w_ref[:, n * D_MODEL:(n + 1) * D_MODEL], preferred_element_type=F32)
        if n == 0:
            r = r * q_scale
        if n == 1 and km_ref:
            km_ref[0][0] = jnp.mean(r.reshape(QKV_TM // MOBA_BLOCK, MOBA_BLOCK, D_MODEL), axis=1)
        o_ref[:, n * D_MODEL:(n + 1) * D_MODEL] = r.astype(BF16)


def _qkv_proj(x2, g, sc, sh, w_bf16, q_scale, with_kmean):
    tiles_per_batch = SEQ // QKV_TM
    n_tiles = ROWS // QKV_TM
    vec = pl.BlockSpec((1, 1, D_MODEL), lambda i: (i // tiles_per_batch, 0, 0))
    out_shape = [jax.ShapeDtypeStruct((ROWS, 3 * D_MODEL), BF16)]
    out_specs = [pl.BlockSpec((QKV_TM, 3 * D_MODEL), lambda i: (i, 0))]
    if with_kmean:
        per_tile = QKV_TM // MOBA_BLOCK
        out_shape.append(jax.ShapeDtypeStruct((n_tiles, per_tile, D_MODEL), F32))
        out_specs.append(pl.BlockSpec((1, per_tile, D_MODEL), lambda i: (i, 0, 0)))
    return pl.pallas_call(
        functools.partial(_qkv_kernel, q_scale=q_scale),
        grid=(n_tiles,),
        in_specs=[
            pl.BlockSpec((QKV_TM, D_MODEL), lambda i: (i, 0)),
            pl.BlockSpec((1, D_MODEL), lambda i: (0, 0)),
            vec, vec,
            pl.BlockSpec((D_MODEL, 3 * D_MODEL), lambda i: (0, 0)),
        ],
        out_specs=out_specs,
        out_shape=out_shape,
        compiler_params=pltpu.CompilerParams(vmem_limit_bytes=VMEM_LIMIT),
        name="qkv_moba" if with_kmean else "qkv_diff",
    )(x2, g.reshape(1, D_MODEL), sc, sh, w_bf16)


_NT = (((1,), (1,)), ((), ()))


def _flash_init(s, v, m_ref, l_ref, acc_ref):
    m = jnp.max(s, axis=1, keepdims=True)
    p = jnp.exp(s - m)
    m_ref[...] = m
    l_ref[...] = jnp.sum(p, axis=1, keepdims=True)
    acc_ref[...] = jnp.dot(p.astype(BF16), v, preferred_element_type=F32)


def _flash_step(s, v, m_ref, l_ref, acc_ref):
    m_prev = m_ref[...]
    m_new = jnp.maximum(m_prev, jnp.max(s, axis=1, keepdims=True))
    alpha = jnp.exp(m_prev - m_new)
    p = jnp.exp(s - m_new)
    m_ref[...] = m_new
    l_ref[...] = alpha * l_ref[...] + jnp.sum(p, axis=1, keepdims=True)
    acc_ref[...] = alpha * acc_ref[...] + jnp.dot(p.astype(BF16), v, preferred_element_type=F32)


def _rows(ref, blk):
    return ref[pl.ds(pl.multiple_of(blk * TILE, TILE), TILE), :]


def _moba_kernel(q_ref, k_ref, v_ref, km_ref, e_ref, bias_ref, o_ref, qa_ref, m_ref, l_ref, acc_ref):
    own = pl.program_id(2)
    q = q_ref[...]

    km = jnp.concatenate(
        [km_ref[...].astype(BF16), jnp.zeros((HEAD_DIM - N_KV_BLOCKS, HEAD_DIM), BF16)], axis=0)
    gate = lax.dot_general(q, km, _NT, preferred_element_type=F32)
    col = lax.broadcasted_iota(jnp.int32, gate.shape, 1)
    colf = col.astype(F32)
    past = col < own
    g = jnp.where(past, gate, NEG_INF)
    madd = jnp.full(gate.shape, NEG_INF, F32)
    for _ in range(MOBA_TOPK):
        mx = jnp.max(g, axis=1, keepdims=True)
        first = jnp.min(jnp.where(g == mx, colf, float(HEAD_DIM)), axis=1, keepdims=True)
        hit = colf == first
        madd = jnp.where(hit & past, 0.0, madd)
        g = jnp.where(hit, -jnp.inf, g)
    qa_ref[:, :HEAD_DIM] = q
    qa_ref[:, HEAD_DIM:] = madd.astype(BF16)

    s = lax.dot_general(q, _rows(k_ref, own), _NT, preferred_element_type=F32) + bias_ref[0, 0]
    _flash_init(s, _rows(v_ref, own), m_ref, l_ref, acc_ref)

    def scores(blk):
        ka = jnp.concatenate([_rows(k_ref, blk), _rows(e_ref, blk)], axis=1)
        return lax.dot_general(qa_ref[...], ka, _NT, preferred_element_type=F32)

    def near(t, carry):
        blk = own - t
        _flash_step(scores(blk) + bias_ref[0, t], _rows(v_ref, blk), m_ref, l_ref, acc_ref)
        return carry

    def far(blk, carry):
        _flash_step(scores(blk), _rows(v_ref, blk), m_ref, l_ref, acc_ref)
        return carry

    n_near = jnp.minimum(own, N_BIAS_TILES - 1)
    lax.fori_loop(1, n_near + 1, near, 0)
    lax.fori_loop(0, own - n_near, far, 0)

    o_ref[...] = (acc_ref[...] / l_ref[...]).astype(BF16)


def _block_onehot():
    blk = np.arange(SEQ)[:, None] // MOBA_BLOCK
    return jnp.asarray((blk == np.arange(HEAD_DIM)[None, :]).astype(np.float32), dtype=BF16)


def _moba_attention(qkv, kmean, bias):
    nq = N_KV_BLOCKS
    return pl.pallas_call(
        _moba_kernel,
        grid=(BATCH, N_HEADS, nq),
        in_specs=[
            pl.BlockSpec((TILE, HEAD_DIM), lambda b, h, i: (b * nq + i, h)),
            pl.BlockSpec((SEQ, HEAD_DIM), lambda b, h, i: (b, N_HEADS + h)),
            pl.BlockSpec((SEQ, HEAD_DIM), lambda b, h, i: (b, 2 * N_HEADS + h)),
            pl.BlockSpec((N_KV_BLOCKS, HEAD_DIM), lambda b, h, i: (b, h)),
            pl.BlockSpec((SEQ, HEAD_DIM), lambda b, h, i: (0, 0)),
            pl.BlockSpec((1, N_BIAS_TILES, TILE, TILE), lambda b, h, i: (h, 0, 0, 0)),
        ],
        out_specs=pl.BlockSpec((TILE, HEAD_DIM), lambda b, h, i: (b * nq + i, h)),
        out_shape=jax.ShapeDtypeStruct((ROWS, D_MODEL), BF16),
        scratch_shapes=[
            pltpu.VMEM((TILE, 2 * HEAD_DIM), BF16),
            pltpu.VMEM((TILE, 1), F32),
            pltpu.VMEM((TILE, 1), F32),
            pltpu.VMEM((TILE, HEAD_DIM), F32),
        ],
        compiler_params=pltpu.CompilerParams(vmem_limit_bytes=VMEM_LIMIT),
        name="moba_attn",
    )(qkv, qkv, qkv, kmean, _block_onehot(), bias)


def _diff_kernel(q_ref, k_ref, v_ref, bias_ref, lam_ref, g_ref, o_ref, qa_ref, m_ref, l_ref, acc_ref,
                 *, lambda_init):
    own = pl.program_id(2)
    q = q_ref[...]
    lane = lax.broadcasted_iota(jnp.int32, q.shape, 1)
    zero = jnp.zeros_like(q)
    qa_ref[:TILE, :] = jnp.where(lane < DIFF_HALF, q, zero)
    qa_ref[TILE:, :] = jnp.where(lane >= DIFF_HALF, q, zero)

    def scores(blk):
        return lax.dot_general(qa_ref[...], _rows(k_ref, blk), _NT, preferred_element_type=F32)

    def biased(s, t):
        return (s.reshape(2, TILE, TILE) + bias_ref[0, t][None]).reshape(2 * TILE, TILE)

    _flash_init(biased(scores(own), 0), _rows(v_ref, own), m_ref, l_ref, acc_ref)

    def near(t, carry):
        blk = own - t
        _flash_step(biased(scores(blk), t), _rows(v_ref, blk), m_ref, l_ref, acc_ref)
        return carry

    def far(blk, carry):
        _flash_step(scores(blk), _rows(v_ref, blk), m_ref, l_ref, acc_ref)
        return carry

    n_near = jnp.minimum(own, N_BIAS_TILES - 1)
    lax.fori_loop(1, n_near + 1, near, 0)
    lax.fori_loop(0, own - n_near, far, 0)

    lam = lam_ref[...]
    lam_full = (jnp.exp(jnp.sum(lam[0:1] * lam[1:2], axis=1, keepdims=True))
                - jnp.exp(jnp.sum(lam[2:3] * lam[3:4], axis=1, keepdims=True)) + lambda_init)
    att = acc_ref[...] / l_ref[...]
    o = att[:TILE] - lam_full * att[TILE:]
    o = o * lax.rsqrt(jnp.mean(o * o, axis=-1, keepdims=True) + RMS_EPS) * g_ref[...]
    o_ref[...] = (o * (1.0 - lambda_init)).astype(BF16)


def _diff_attention(qkv, bias, lam, subln_g, layer_idx):
    nq = SEQ // TILE
    lambda_init = 0.8 - 0.6 * math.exp(-0.3 * layer_idx)
    return pl.pallas_call(
        functools.partial(_diff_kernel, lambda_init=lambda_init),
        grid=(BATCH, N_HEADS, nq),
        in_specs=[
            pl.BlockSpec((TILE, HEAD_DIM), lambda b, h, i: (b * nq + i, h)),
            pl.BlockSpec((SEQ, HEAD_DIM), lambda b, h, i: (b, N_HEADS + h)),
            pl.BlockSpec((SEQ, HEAD_DIM), lambda b, h, i: (b, 2 * N_HEADS + h)),
            pl.BlockSpec((1, N_BIAS_TILES, TILE, TILE), lambda b, h, i: (h, 0, 0, 0)),
            pl.BlockSpec((4, DIFF_HALF), lambda b, h, i: (0, 0)),
            pl.BlockSpec((1, HEAD_DIM), lambda b, h, i: (0, 0)),
        ],
        out_specs=pl.BlockSpec((TILE, HEAD_DIM), lambda b, h, i: (b * nq + i, h)),
        out_shape=jax.ShapeDtypeStruct((ROWS, D_MODEL), BF16),
        scratch_shapes=[
            pltpu.VMEM((2 * TILE, HEAD_DIM), BF16),
            pltpu.VMEM((2 * TILE, 1), F32),
            pltpu.VMEM((2 * TILE, 1), F32),
            pltpu.VMEM((2 * TILE, HEAD_DIM), F32),
        ],
        compiler_params=pltpu.CompilerParams(vmem_limit_bytes=VMEM_LIMIT),
        name="diff_attn",
    )(qkv, qkv, qkv, bias, lam, subln_g.reshape(1, HEAD_DIM))


OUT_TM = 512


def _rms(y, g):
    return y * lax.rsqrt(jnp.mean(y * y, axis=-1, keepdims=True) + RMS_EPS) * g


def _outproj_kernel(o_ref, x_ref, w_ref, g_ref, gate_ref, out_ref):
    y = jnp.dot(o_ref[...], w_ref[...], preferred_element_type=F32)
    out_ref[...] = x_ref[...] + gate_ref[0] * _rms(y, g_ref[...])


def _out_proj(o, x2, w_bf16, g, gate):
    tiles_per_batch = SEQ // OUT_TM
    return pl.pallas_call(
        _outproj_kernel,
        grid=(ROWS // OUT_TM,),
        in_specs=[
            pl.BlockSpec((OUT_TM, D_MODEL), lambda i: (i, 0)),
            pl.BlockSpec((OUT_TM, D_MODEL), lambda i: (i, 0)),
            pl.BlockSpec((D_MODEL, D_MODEL), lambda i: (0, 0)),
            pl.BlockSpec((1, D_MODEL), lambda i: (0, 0)),
            pl.BlockSpec((1, 1, D_MODEL), lambda i: (i // tiles_per_batch, 0, 0)),
        ],
        out_specs=pl.BlockSpec((OUT_TM, D_MODEL), lambda i: (i, 0)),
        out_shape=jax.ShapeDtypeStruct((ROWS, D_MODEL), F32),
        compiler_params=pltpu.CompilerParams(vmem_limit_bytes=VMEM_LIMIT),
        name="out_proj",
    )(o, x2, w_bf16, g.reshape(1, D_MODEL), gate)


FFN_TM = 1024
FFN_TF = 256


def _ffn_kernel(x_ref, g2_ref, sc_ref, sh_ref, wg_ref, wu_ref, wo_ref, g3_ref, gate_ref, out_ref,
                h_ref, acc_ref):
    c = pl.program_id(1)

    @pl.when(c == 0)
    def _():
        h_ref[...] = _norm_mod(x_ref[...], g2_ref[...], sc_ref[0], sh_ref[0]).astype(BF16)

    h = h_ref[...]
    gp = jnp.dot(h, wg_ref[...], preferred_element_type=F32)
    up = jnp.dot(h, wu_ref[...], preferred_element_type=F32)
    act = (gp / (1.0 + jnp.exp(-gp)) * up).astype(BF16)
    part = jnp.dot(act, wo_ref[...], preferred_element_type=F32)

    @pl.when(c == 0)
    def _():
        acc_ref[...] = part

    @pl.when(c > 0)
    def _():
        acc_ref[...] += part

    @pl.when(c == pl.num_programs(1) - 1)
    def _():
        out_ref[...] = x_ref[...] + gate_ref[0] * _rms(acc_ref[...], g3_ref[...])


def _ffn(x2, g2, sc, sh, w_in_bf16, w_out_bf16, g3, gate):
    tiles_per_batch = SEQ // FFN_TM
    n_chunks = D_FF // FFN_TF
    vec = pl.BlockSpec((1, 1, D_MODEL), lambda i, c: (i // tiles_per_batch, 0, 0))
    gvec = pl.BlockSpec((1, D_MODEL), lambda i, c: (0, 0))
    return pl.pallas_call(
        _ffn_kernel,
        grid=(ROWS // FFN_TM, n_chunks),
        in_specs=[
            pl.BlockSpec((FFN_TM, D_MODEL), lambda i, c: (i, 0)),
            gvec, vec, vec,
            pl.BlockSpec((D_MODEL, FFN_TF), lambda i, c: (0, c)),
            pl.BlockSpec((D_MODEL, FFN_TF), lambda i, c: (0, n_chunks + c)),
            pl.BlockSpec((FFN_TF, D_MODEL), lambda i, c: (c, 0)),
            gvec, vec,
        ],
        out_specs=pl.BlockSpec((FFN_TM, D_MODEL), lambda i, c: (i, 0)),
        out_shape=jax.ShapeDtypeStruct((ROWS, D_MODEL), F32),
        scratch_shapes=[
            pltpu.VMEM((FFN_TM, D_MODEL), BF16),
            pltpu.VMEM((FFN_TM, D_MODEL), F32),
        ],
        compiler_params=pltpu.CompilerParams(
            dimension_semantics=("parallel", "arbitrary"), vmem_limit_bytes=VMEM_LIMIT),
        name="ffn",
    )(x2, g2.reshape(1, D_MODEL), sc, sh, w_in_bf16, w_in_bf16, w_out_bf16, g3.reshape(1, D_MODEL), gate)


def kernel(x, c, rel_bias, ada_w, ada_b, norm_g, moba_w_qkv, moba_w_o, diff_w_qkv, diff_w_o, diff_lambda,
           diff_subln_g, ffn_w_in, ffn_w_out):
    x2 = x.reshape(ROWS, D_MODEL)
    mod = _adaln_mod(c, ada_w, ada_b)
    bias = _bias_tiles(rel_bias)
    for i in range(DEPTH):
        sh_a, sc_a, g_a, sh_f, sc_f, g_f = [
            mod[i, :, j * D_MODEL:(j + 1) * D_MODEL].reshape(BATCH, 1, D_MODEL) for j in range(6)]
        if i % 2 == 0:
            qkv, kmean = _qkv_proj(x2, norm_g[i, 0], sc_a, sh_a, moba_w_qkv[i // 2].astype(BF16),
                                   HEAD_DIM ** -0.5, True)
            o = _moba_attention(qkv, kmean.reshape(BATCH * N_KV_BLOCKS, D_MODEL), bias)
            w_o = moba_w_o[i // 2]
        else:
            (qkv,) = _qkv_proj(x2, norm_g[i, 0], sc_a, sh_a, diff_w_qkv[i // 2].astype(BF16),
                               DIFF_HALF ** -0.5, False)
            o = _diff_attention(qkv, bias, diff_lambda[i // 2], diff_subln_g[i // 2], i)
            w_o = diff_w_o[i // 2]
        x2 = _out_proj(o, x2, w_o.astype(BF16), norm_g[i, 1], g_a)
        x2 = _ffn(x2, norm_g[i, 2], sc_f, sh_f, ffn_w_in[i].astype(BF16), ffn_w_out[i].astype(BF16),
                  norm_g[i, 3], g_f)
    return x2.reshape(BATCH, SEQ, D_MODEL)
```

```python
import functools
import math

import numpy as np
import jax
import jax.numpy as jnp
from jax import lax
from jax.experimental import pallas as pl
from jax.experimental.pallas import tpu as pltpu

D_MODEL = 1024
BATCH = 2
SEQ = 8192
DEPTH = 2
N_HEADS = 8
HEAD_DIM = D_MODEL // N_HEADS
DIFF_HALF = HEAD_DIM // 2
MOBA_BLOCK = 256
MOBA_TOPK = 3
D_FF = 2816
N_BUCKETS = 32
MAX_EXACT = N_BUCKETS // 2
MAX_DISTANCE = 2048
RMS_EPS = 1e-6
NEG_INF = -1e30

ROWS = BATCH * SEQ
N_KV_BLOCKS = SEQ // MOBA_BLOCK
TILE = MOBA_BLOCK
VMEM_LIMIT = 48 * 1024 * 1024

F32 = jnp.float32
BF16 = jnp.bfloat16


def _bucket_of_distance():
    n = np.arange(MAX_DISTANCE + 1)
    nf = np.maximum(n, 1).astype(np.float64)
    val = np.log(nf / MAX_EXACT) / math.log(MAX_DISTANCE / MAX_EXACT) * (N_BUCKETS - MAX_EXACT)
    frac = np.abs(val - np.round(val))
    assert np.all((frac > 5e-5) | (n <= MAX_EXACT) | (n == MAX_DISTANCE))
    large = np.minimum(MAX_EXACT + np.floor(val + 1e-9).astype(np.int64), N_BUCKETS - 1)
    return np.where(n < MAX_EXACT, n, large)


_BUCKETS = _bucket_of_distance()
FAR_DISTANCE = int(np.min(np.nonzero(_BUCKETS == N_BUCKETS - 1)[0]))
N_BIAS_TILES = (FAR_DISTANCE + TILE - 1) // TILE + 1


N_TABLE_TILES = N_BIAS_TILES + 2


def _bias_bucket_tiles():
    i = np.arange(TILE)[:, None]
    j = np.arange(TILE)[None, :]
    out = []
    for t in range(-1, N_BIAS_TILES + 1):
        rel = t * TILE + i - j
        out.append(np.where(rel >= 0, _BUCKETS[np.clip(rel, 0, MAX_DISTANCE)], -1))
    return np.stack(out).astype(np.int32)


def _table_index(tiles_behind):
    return jnp.clip(tiles_behind, -1, N_BIAS_TILES) + 1


MOD_TN = 512


def _mod_kernel(ct_ref, w_ref, b_ref, o_ref):
    ct = ct_ref[...]
    cact = ct / (1.0 + jnp.exp(-ct))
    w = w_ref[0]
    for b in range(BATCH):
        row = jnp.sum(w * cact[:, b:b + 1], axis=0, keepdims=True)
        o_ref[0, b:b + 1, :] = row + b_ref[0]


def _adaln_mod(c, ada_w, ada_b):
    n_out = ada_w.shape[-1]
    return pl.pallas_call(
        _mod_kernel,
        grid=(DEPTH, n_out // MOD_TN),
        in_specs=[
            pl.BlockSpec((D_MODEL, BATCH), lambda i, n: (0, 0)),
            pl.BlockSpec((1, D_MODEL, MOD_TN), lambda i, n: (i, 0, n)),
            pl.BlockSpec((1, 1, MOD_TN), lambda i, n: (i, 0, n)),
        ],
        out_specs=pl.BlockSpec((1, BATCH, MOD_TN), lambda i, n: (i, 0, n)),
        out_shape=jax.ShapeDtypeStruct((DEPTH, BATCH, n_out), F32),
        name="adaln_mod",
    )(c.T, ada_w, ada_b.reshape(DEPTH, 1, n_out))


def _bias_kernel(tab_ref, bkt_ref, o_ref):
    h = pl.program_id(0)
    bkt = bkt_ref[0]
    far = tab_ref[N_BUCKETS - 1, h]
    acc = jnp.full(bkt.shape, NEG_INF, F32)
    for b in range(N_BUCKETS):
        acc = jnp.where(bkt == b, tab_ref[b, h] - far, acc)
    o_ref[0, 0] = acc


def _bias_tiles(rel_bias):
    bkt = jnp.asarray(_bias_bucket_tiles())
    return pl.pallas_call(
        _bias_kernel,
        grid=(N_HEADS, N_TABLE_TILES),
        in_specs=[
            pl.BlockSpec(memory_space=pltpu.SMEM),
            pl.BlockSpec((1, TILE, TILE), lambda h, t: (t, 0, 0)),
        ],
        out_specs=pl.BlockSpec((1, 1, TILE, TILE), lambda h, t: (h, t, 0, 0)),
        out_shape=jax.ShapeDtypeStruct((N_HEADS, N_TABLE_TILES, TILE, TILE), F32),
        name="bias_tiles",
    )(rel_bias, bkt)


QKV_TM = 512


def _norm_mod(x, g, sc, sh):
    y = x * lax.rsqrt(jnp.mean(x * x, axis=-1, keepdims=True) + RMS_EPS)
    return (y * g) * (1.0 + sc) + sh


def _qkv_kernel(x_ref, g_ref, sc_ref, sh_ref, w_ref, o_ref, *km_ref, q_scale):
    hb = _norm_mod(x_ref[...], g_ref[...], sc_ref[0], sh_ref[0]).astype(BF16)
    for n in range(3):
        r = jnp.dot(hb, w_ref[:, n * D_MODEL:(n + 1) * D_MODEL], preferred_element_type=F32)
        if n == 0:
            r = r * q_scale
        if n == 1 and km_ref:
            km_ref[0][0] = jnp.mean(r.reshape(QKV_TM // MOBA_BLOCK, MOBA_BLOCK, D_MODEL), axis=1)
        o_ref[:, n * D_MODEL:(n + 1) * D_MODEL] = r.astype(BF16)


def _qkv_proj(x2, g, sc, sh, w_bf16, q_scale, with_kmean):
    tiles_per_batch = SEQ // QKV_TM
    n_tiles = ROWS // QKV_TM
    vec = pl.BlockSpec((1, 1, D_MODEL), lambda i: (i // tiles_per_batch, 0, 0))
    out_shape = [jax.ShapeDtypeStruct((ROWS, 3 * D_MODEL), BF16)]
    out_specs = [pl.BlockSpec((QKV_TM, 3 * D_MODEL), lambda i: (i, 0))]
    if with_kmean:
        per_tile = QKV_TM // MOBA_BLOCK
        out_shape.append(jax.ShapeDtypeStruct((n_tiles, per_tile, D_MODEL), F32))
        out_specs.append(pl.BlockSpec((1, per_tile, D_MODEL), lambda i: (i, 0, 0)))
    return pl.pallas_call(
        functools.partial(_qkv_kernel, q_scale=q_scale),
        grid=(n_tiles,),
        in_specs=[
            pl.BlockSpec((QKV_TM, D_MODEL), lambda i: (i, 0)),
            pl.BlockSpec((1, D_MODEL), lambda i: (0, 0)),
            vec, vec,
            pl.BlockSpec((D_MODEL, 3 * D_MODEL), lambda i: (0, 0)),
        ],
        out_specs=out_specs,
        out_shape=out_shape,
        compiler_params=pltpu.CompilerParams(vmem_limit_bytes=VMEM_LIMIT),
        name="qkv_moba" if with_kmean else "qkv_diff",
    )(x2, g.reshape(1, D_MODEL), sc, sh, w_bf16)


_NT = (((1,), (1,)), ((), ()))
GROUP = 4
GROUP_KEYS = GROUP * TILE
N_NEAR_GROUPS = -(-(N_BIAS_TILES - 1) // GROUP)


def _flash_first(s, va, m_ref, acc_ref):
    m = jnp.max(s, axis=1, keepdims=True)
    p = jnp.exp(s - m)
    m_ref[...] = m
    acc_ref[...] = jnp.dot(p.astype(BF16), va, preferred_element_type=F32)


def _flash_step(s, va, m_ref, acc_ref):
    m_prev = m_ref[...]
    m_new = jnp.maximum(m_prev, jnp.max(s, axis=1, keepdims=True))
    alpha = jnp.exp(m_prev - m_new)
    p = jnp.exp(s - m_new)
    m_ref[...] = m_new
    acc_ref[...] = alpha * acc_ref[...] + jnp.dot(p.astype(BF16), va, preferred_element_type=F32)


def _flash_result(acc_ref):
    acc = acc_ref[...]
    return acc[:, :HEAD_DIM] / acc[:, HEAD_DIM:]


def _rows(ref, blk):
    return ref[pl.ds(pl.multiple_of(blk * TILE, TILE), TILE), :]


def _group_rows(ref, grp):
    return ref[pl.ds(pl.multiple_of(grp * GROUP_KEYS, GROUP_KEYS), GROUP_KEYS), :]


def _add_group_bias(s, bias_ref, own, grp):
    cols = []
    for j in range(GROUP):
        tile = bias_ref[0, _table_index(own - (grp * GROUP + j))]
        sj = s[:, j * TILE:(j + 1) * TILE]
        cols.append(jnp.concatenate(
            [sj[r * TILE:(r + 1) * TILE] + tile for r in range(s.shape[0] // TILE)], axis=0))
    return jnp.concatenate(cols, axis=1)


def _fill_values(v_ref, va_ref):
    ones = jnp.ones((TILE, HEAD_DIM), BF16)

    def fill(blk, carry):
        rows = pl.ds(pl.multiple_of(blk * TILE, TILE), TILE)
        va_ref[rows, :HEAD_DIM] = v_ref[rows, :]
        va_ref[rows, HEAD_DIM:] = ones
        return carry

    lax.fori_loop(0, N_KV_BLOCKS, fill, 0)


def _moba_kernel(q_ref, k_ref, v_ref, km_ref, bias_ref, o_ref, qa_ref, ka_ref, va_ref, m_ref, acc_ref):
    own = pl.program_id(2)
    q = q_ref[...]

    @pl.when(own == 0)
    def _():
        lane = lax.broadcasted_iota(jnp.int32, (TILE, HEAD_DIM), 1)

        def fill(blk, carry):
            rows = pl.ds(pl.multiple_of(blk * TILE, TILE), TILE)
            ka_ref[rows, :HEAD_DIM] = k_ref[rows, :]
            ka_ref[rows, HEAD_DIM:] = jnp.where(lane == blk, 1.0, 0.0).astype(BF16)
            return carry

        lax.fori_loop(0, N_KV_BLOCKS, fill, 0)
        _fill_values(v_ref, va_ref)

    km = jnp.concatenate(
        [km_ref[...].astype(BF16), jnp.zeros((HEAD_DIM - N_KV_BLOCKS, HEAD_DIM), BF16)], axis=0)
    gate = lax.dot_general(q, km, _NT, preferred_element_type=F32)
    col = lax.broadcasted_iota(jnp.int32, gate.shape, 1)
    colf = col.astype(F32)
    past = col < own
    g = jnp.where(past, gate, NEG_INF)
    madd = jnp.full(gate.shape, NEG_INF, F32)
    for _ in range(MOBA_TOPK):
        mx = jnp.max(g, axis=1, keepdims=True)
        first = jnp.min(jnp.where(g == mx, colf, float(HEAD_DIM)), axis=1, keepdims=True)
        hit = colf == first
        madd = jnp.where(hit & past, 0.0, madd)
        g = jnp.where(hit, -jnp.inf, g)
    madd = jnp.where(col == own, 0.0, madd)
    qa_ref[:, :HEAD_DIM] = q
    qa_ref[:, HEAD_DIM:] = madd.astype(BF16)

    def scores(grp):
        return lax.dot_general(qa_ref[...], _group_rows(ka_ref, grp), _NT, preferred_element_type=F32)

    own_grp = own // GROUP
    _flash_first(_add_group_bias(scores(own_grp), bias_ref, own, own_grp), _group_rows(va_ref, own_grp),
                 m_ref, acc_ref)

    def near(r, carry):
        grp = own_grp - 1 - r
        _flash_step(_add_group_bias(scores(grp), bias_ref, own, grp), _group_rows(va_ref, grp), m_ref, acc_ref)
        return carry

    def far(grp, carry):
        _flash_step(scores(grp), _group_rows(va_ref, grp), m_ref, acc_ref)
        return carry

    n_near = jnp.minimum(own_grp, N_NEAR_GROUPS)
    lax.fori_loop(0, n_near, near, 0)
    lax.fori_loop(0, own_grp - n_near, far, 0)

    o_ref[...] = _flash_result(acc_ref).astype(BF16)


def _moba_attention(qkv, kmean, bias):
    nq = N_KV_BLOCKS
    return pl.pallas_call(
        _moba_kernel,
        grid=(BATCH, N_HEADS, nq),
        in_specs=[
            pl.BlockSpec((TILE, HEAD_DIM), lambda b, h, i: (b * nq + i, h)),
            pl.BlockSpec((SEQ, HEAD_DIM), lambda b, h, i: (b, N_HEADS + h)),
            pl.BlockSpec((SEQ, HEAD_DIM), lambda b, h, i: (b, 2 * N_HEADS + h)),
            pl.BlockSpec((N_KV_BLOCKS, HEAD_DIM), lambda b, h, i: (b, h)),
            pl.BlockSpec((1, N_TABLE_TILES, TILE, TILE), lambda b, h, i: (h, 0, 0, 0)),
        ],
        out_specs=pl.BlockSpec((TILE, HEAD_DIM), lambda b, h, i: (b * nq + i, h)),
        out_shape=jax.ShapeDtypeStruct((ROWS, D_MODEL), BF16),
        scratch_shapes=[
            pltpu.VMEM((TILE, 2 * HEAD_DIM), BF16),
            pltpu.VMEM((SEQ, 2 * HEAD_DIM), BF16),
            pltpu.VMEM((SEQ, 2 * HEAD_DIM), BF16),
            pltpu.VMEM((TILE, 1), F32),
            pltpu.VMEM((TILE, 2 * HEAD_DIM), F32),
        ],
        compiler_params=pltpu.CompilerParams(
            dimension_semantics=("arbitrary", "arbitrary", "arbitrary"), vmem_limit_bytes=VMEM_LIMIT),
        name="moba_attn",
    )(qkv, qkv, qkv, kmean, bias)


def _diff_kernel(q_ref, k_ref, v_ref, bias_ref, lam_ref, g_ref, o_ref, qa_ref, va_ref, m_ref, acc_ref,
                 *, lambda_init):
    own = pl.program_id(2)

    @pl.when(own == 0)
    def _():
        _fill_values(v_ref, va_ref)

    q = q_ref[...]
    lane = lax.broadcasted_iota(jnp.int32, q.shape, 1)
    zero = jnp.zeros_like(q)
    qa_ref[:TILE, :] = jnp.where(lane < DIFF_HALF, q, zero)
    qa_ref[TILE:, :] = jnp.where(lane >= DIFF_HALF, q, zero)

    def scores(grp):
        return lax.dot_general(qa_ref[...], _group_rows(k_ref, grp), _NT, preferred_element_type=F32)

    own_grp = own // GROUP
    _flash_first(_add_group_bias(scores(own_grp), bias_ref, own, own_grp), _group_rows(va_ref, own_grp),
                 m_ref, acc_ref)

    def near(r, carry):
        grp = own_grp - 1 - r
        _flash_step(_add_group_bias(scores(grp), bias_ref, own, grp), _group_rows(va_ref, grp), m_ref, acc_ref)
        return carry

    def far(grp, carry):
        _flash_step(scores(grp), _group_rows(va_ref, grp), m_ref, acc_ref)
        return carry

    n_near = jnp.minimum(own_grp, N_NEAR_GROUPS)
    lax.fori_loop(0, n_near, near, 0)
    lax.fori_loop(0, own_grp - n_near, far, 0)

    lam = lam_ref[...]
    lam_full = (jnp.exp(jnp.sum(lam[0:1] * lam[1:2], axis=1, keepdims=True))
                - jnp.exp(jnp.sum(lam[2:3] * lam[3:4], axis=1, keepdims=True)) + lambda_init)
    att = _flash_result(acc_ref)
    o = att[:TILE] - lam_full * att[TILE:]
    o = o * lax.rsqrt(jnp.mean(o * o, axis=-1, keepdims=True) + RMS_EPS) * g_ref[...]
    o_ref[...] = (o * (1.0 - lambda_init)).astype(BF16)


def _diff_attention(qkv, bias, lam, subln_g, layer_idx):
    nq = SEQ // TILE
    lambda_init = 0.8 - 0.6 * math.exp(-0.3 * layer_idx)
    return pl.pallas_call(
        functools.partial(_diff_kernel, lambda_init=lambda_init),
        grid=(BATCH, N_HEADS, nq),
        in_specs=[
            pl.BlockSpec((TILE, HEAD_DIM), lambda b, h, i: (b * nq + i, h)),
            pl.BlockSpec((SEQ, HEAD_DIM), lambda b, h, i: (b, N_HEADS + h)),
            pl.BlockSpec((SEQ, HEAD_DIM), lambda b, h, i: (b, 2 * N_HEADS + h)),
            pl.BlockSpec((1, N_TABLE_TILES, TILE, TILE), lambda b, h, i: (h, 0, 0, 0)),
            pl.BlockSpec((4, DIFF_HALF), lambda b, h, i: (0, 0)),
            pl.BlockSpec((1, HEAD_DIM), lambda b, h, i: (0, 0)),
        ],
        out_specs=pl.BlockSpec((TILE, HEAD_DIM), lambda b, h, i: (b * nq + i, h)),
        out_shape=jax.ShapeDtypeStruct((ROWS, D_MODEL), BF16),
        scratch_shapes=[
            pltpu.VMEM((2 * TILE, HEAD_DIM), BF16),
            pltpu.VMEM((SEQ, 2 * HEAD_DIM), BF16),
            pltpu.VMEM((2 * TILE, 1), F32),
            pltpu.VMEM((2 * TILE, 2 * HEAD_DIM), F32),
        ],
        compiler_params=pltpu.CompilerParams(
            dimension_semantics=("arbitrary", "arbitrary", "arbitrary"), vmem_limit_bytes=VMEM_LIMIT),
        name="diff_attn",
    )(qkv, qkv, qkv, bias, lam, subln_g.reshape(1, HEAD_DIM))


OUT_TM = 512


def _rms(y, g):
    return y * lax.rsqrt(jnp.mean(y * y, axis=-1, keepdims=True) + RMS_EPS) * g


def _outproj_kernel(o_ref, x_ref, w_ref, g_ref, gate_ref, out_ref):
    y = jnp.dot(o_ref[...], w_ref[...], preferred_element_type=F32)
    out_ref[...] = x_ref[...] + gate_ref[0] * _rms(y, g_ref[...])


def _out_proj(o, x2, w_bf16, g, gate):
    tiles_per_batch = SEQ // OUT_TM
    return pl.pallas_call(
        _outproj_kernel,
        grid=(ROWS // OUT_TM,),
        in_specs=[
            pl.BlockSpec((OUT_TM, D_MODEL), lambda i: (i, 0)),
            pl.BlockSpec((OUT_TM, D_MODEL), lambda i: (i, 0)),
            pl.BlockSpec((D_MODEL, D_MODEL), lambda i: (0, 0)),
            pl.BlockSpec((1, D_MODEL), lambda i: (0, 0)),
            pl.BlockSpec((1, 1, D_MODEL), lambda i: (i // tiles_per_batch, 0, 0)),
        ],
        out_specs=pl.BlockSpec((OUT_TM, D_MODEL), lambda i: (i, 0)),
        out_shape=jax.ShapeDtypeStruct((ROWS, D_MODEL), F32),
        compiler_params=pltpu.CompilerParams(vmem_limit_bytes=VMEM_LIMIT),
        name="out_proj",
    )(o, x2, w_bf16, g.reshape(1, D_MODEL), gate)


FFN_TM = 1024
FFN_TF = 256


def _ffn_kernel(x_ref, g2_ref, sc_ref, sh_ref, wg_ref, wu_ref, wo_ref, g3_ref, gate_ref, out_ref,
                h_ref, acc_ref):
    c = pl.program_id(1)

    @pl.when(c == 0)
    def _():
        h_ref[...] = _norm_mod(x_ref[...], g2_ref[...], sc_ref[0], sh_ref[0]).astype(BF16)

    h = h_ref[...]
    gp = jnp.dot(h, wg_ref[...], preferred_element_type=F32)
    up = jnp.dot(h, wu_ref[...], preferred_element_type=F32)
    act = (gp / (1.0 + jnp.exp(-gp)) * up).astype(BF16)
    part = jnp.dot(act, wo_ref[...], preferred_element_type=F32)

    @pl.when(c == 0)
    def _():
        acc_ref[...] = part

    @pl.when(c > 0)
    def _():
        acc_ref[...] += part

    @pl.when(c == pl.num_programs(1) - 1)
    def _():
        out_ref[...] = x_ref[...] + gate_ref[0] * _rms(acc_ref[...], g3_ref[...])


def _ffn(x2, g2, sc, sh, w_in_bf16, w_out_bf16, g3, gate):
    tiles_per_batch = SEQ // FFN_TM
    n_chunks = D_FF // FFN_TF
    vec = pl.BlockSpec((1, 1, D_MODEL), lambda i, c: (i // tiles_per_batch, 0, 0))
    gvec = pl.BlockSpec((1, D_MODEL), lambda i, c: (0, 0))
    return pl.pallas_call(
        _ffn_kernel,
        grid=(ROWS // FFN_TM, n_chunks),
        in_specs=[
            pl.BlockSpec((FFN_TM, D_MODEL), lambda i, c: (i, 0)),
            gvec, vec, vec,
            pl.BlockSpec((D_MODEL, FFN_TF), lambda i, c: (0, c)),
            pl.BlockSpec((D_MODEL, FFN_TF), lambda i, c: (0, n_chunks + c)),
            pl.BlockSpec((FFN_TF, D_MODEL), lambda i, c: (c, 0)),
            gvec, vec,
        ],
        out_specs=pl.BlockSpec((FFN_TM, D_MODEL), lambda i, c: (i, 0)),
        out_shape=jax.ShapeDtypeStruct((ROWS, D_MODEL), F32),
        scratch_shapes=[
            pltpu.VMEM((FFN_TM, D_MODEL), BF16),
            pltpu.VMEM((FFN_TM, D_MODEL), F32),
        ],
        compiler_params=pltpu.CompilerParams(
            dimension_semantics=("parallel", "arbitrary"), vmem_limit_bytes=VMEM_LIMIT),
        name="ffn",
    )(x2, g2.reshape(1, D_MODEL), sc, sh, w_in_bf16, w_in_bf16, w_out_bf16, g3.reshape(1, D_MODEL), gate)


def kernel(x, c, rel_bias, ada_w, ada_b, norm_g, moba_w_qkv, moba_w_o, diff_w_qkv, diff_w_o, diff_lambda,
           diff_subln_g, ffn_w_in, ffn_w_out):
    x2 = x.reshape(ROWS, D_MODEL)
    mod = _adaln_mod(c, ada_w, ada_b)
    bias = _bias_tiles(rel_bias)
    for i in range(DEPTH):
        sh_a, sc_a, g_a, sh_f, sc_f, g_f = [
            mod[i, :, j * D_MODEL:(j + 1) * D_MODEL].reshape(BATCH, 1, D_MODEL) for j in range(6)]
        if i % 2 == 0:
            qkv, kmean = _qkv_proj(x2, norm_g[i, 0], sc_a, sh_a, moba_w_qkv[i // 2].astype(BF16),
                                   HEAD_DIM ** -0.5, True)
            o = _moba_attention(qkv, kmean.reshape(BATCH * N_KV_BLOCKS, D_MODEL), bias)
            w_o = moba_w_o[i // 2]
        else:
            (qkv,) = _qkv_proj(x2, norm_g[i, 0], sc_a, sh_a, diff_w_qkv[i // 2].astype(BF16),
                               DIFF_HALF ** -0.5, False)
            o = _diff_attention(qkv, bias, diff_lambda[i // 2], diff_subln_g[i // 2], i)
            w_o = diff_w_o[i // 2]
        x2 = _out_proj(o, x2, w_o.astype(BF16), norm_g[i, 1], g_a)
        x2 = _ffn(x2, norm_g[i, 2], sc_f, sh_f, ffn_w_in[i].astype(BF16), ffn_w_out[i].astype(BF16),
                  norm_g[i, 3], g_f)
    return x2.reshape(BATCH, SEQ, D_MODEL)
```

```python
import functools
import math

import numpy as np
import jax
import jax.numpy as jnp
from jax import lax
from jax.experimental import pallas as pl
from jax.experimental.pallas import tpu as pltpu

D_MODEL = 1024
BATCH = 2
SEQ = 8192
DEPTH = 2
N_HEADS = 8
HEAD_DIM = D_MODEL // N_HEADS
DIFF_HALF = HEAD_DIM // 2
MOBA_BLOCK = 256
MOBA_TOPK = 3
D_FF = 2816
N_BUCKETS = 32
MAX_EXACT = N_BUCKETS // 2
MAX_DISTANCE = 2048
RMS_EPS = 1e-6
NEG_INF = -1e30

ROWS = BATCH * SEQ
N_KV_BLOCKS = SEQ // MOBA_BLOCK
TILE = MOBA_BLOCK
VMEM_LIMIT = 48 * 1024 * 1024

F32 = jnp.float32
BF16 = jnp.bfloat16


def _bucket_of_distance():
    n = np.arange(MAX_DISTANCE + 1)
    nf = np.maximum(n, 1).astype(np.float64)
    val = np.log(nf / MAX_EXACT) / math.log(MAX_DISTANCE / MAX_EXACT) * (N_BUCKETS - MAX_EXACT)
    frac = np.abs(val - np.round(val))
    assert np.all((frac > 5e-5) | (n <= MAX_EXACT) | (n == MAX_DISTANCE))
    large = np.minimum(MAX_EXACT + np.floor(val + 1e-9).astype(np.int64), N_BUCKETS - 1)
    return np.where(n < MAX_EXACT, n, large)


_BUCKETS = _bucket_of_distance()
FAR_DISTANCE = int(np.min(np.nonzero(_BUCKETS == N_BUCKETS - 1)[0]))
N_BIAS_TILES = (FAR_DISTANCE + TILE - 1) // TILE + 1


N_TABLE_TILES = N_BIAS_TILES + 2


def _bias_bucket_tiles():
    i = np.arange(TILE)[:, None]
    j = np.arange(TILE)[None, :]
    out = []
    for t in range(-1, N_BIAS_TILES + 1):
        rel = t * TILE + i - j
        out.append(np.where(rel >= 0, _BUCKETS[np.clip(rel, 0, MAX_DISTANCE)], -1))
    return np.stack(out).astype(np.int32)


def _table_index(tiles_behind):
    return jnp.clip(tiles_behind, -1, N_BIAS_TILES) + 1


MOD_TN = 512


def _mod_kernel(ct_ref, w_ref, b_ref, o_ref):
    ct = ct_ref[...]
    cact = ct / (1.0 + jnp.exp(-ct))
    w = w_ref[0]
    for b in range(BATCH):
        row = jnp.sum(w * cact[:, b:b + 1], axis=0, keepdims=True)
        o_ref[0, b:b + 1, :] = row + b_ref[0]


def _adaln_mod(c, ada_w, ada_b):
    n_out = ada_w.shape[-1]
    return pl.pallas_call(
        _mod_kernel,
        grid=(DEPTH, n_out // MOD_TN),
        in_specs=[
            pl.BlockSpec((D_MODEL, BATCH), lambda i, n: (0, 0)),
            pl.BlockSpec((1, D_MODEL, MOD_TN), lambda i, n: (i, 0, n)),
            pl.BlockSpec((1, 1, MOD_TN), lambda i, n: (i, 0, n)),
        ],
        out_specs=pl.BlockSpec((1, BATCH, MOD_TN), lambda i, n: (i, 0, n)),
        out_shape=jax.ShapeDtypeStruct((DEPTH, BATCH, n_out), F32),
        name="adaln_mod",
    )(c.T, ada_w, ada_b.reshape(DEPTH, 1, n_out))


def _bias_kernel(tab_ref, bkt_ref, o_ref):
    h = pl.program_id(0)
    bkt = bkt_ref[0]
    far = tab_ref[N_BUCKETS - 1, h]
    acc = jnp.full(bkt.shape, NEG_INF, F32)
    for b in range(N_BUCKETS):
        acc = jnp.where(bkt == b, tab_ref[b, h] - far, acc)
    o_ref[0, 0] = acc


def _bias_tiles(rel_bias):
    bkt = jnp.asarray(_bias_bucket_tiles())
    return pl.pallas_call(
        _bias_kernel,
        grid=(N_HEADS, N_TABLE_TILES),
        in_specs=[
            pl.BlockSpec(memory_space=pltpu.SMEM),
            pl.BlockSpec((1, TILE, TILE), lambda h, t: (t, 0, 0)),
        ],
        out_specs=pl.BlockSpec((1, 1, TILE, TILE), lambda h, t: (h, t, 0, 0)),
        out_shape=jax.ShapeDtypeStruct((N_HEADS, N_TABLE_TILES, TILE, TILE), F32),
        name="bias_tiles",
    )(rel_bias, bkt)


QKV_TM = 512


def _norm_mod(x, g, sc, sh):
    y = x * lax.rsqrt(jnp.mean(x * x, axis=-1, keepdims=True) + RMS_EPS)
    return (y * g) * (1.0 + sc) + sh


def _qkv_kernel(x_ref, g_ref, sc_ref, sh_ref, w_ref, o_ref, *km_ref, q_scale):
    hb = _norm_mod(x_ref[...], g_ref[...], sc_ref[0], sh_ref[0]).astype(BF16)
    for n in range(3):
        r = jnp.dot(hb, w_ref[:, n * D_MODEL:(n + 1) * D_MODEL], preferred_element_type=F32)
        if n == 0:
            r = r * q_scale
        if n == 1 and km_ref:
            km_ref[0][0] = jnp.mean(r.reshape(QKV_TM // MOBA_BLOCK, MOBA_BLOCK, D_MODEL), axis=1)
        o_ref[:, n * D_MODEL:(n + 1) * D_MODEL] = r.astype(BF16)


def _qkv_proj(x2, g, sc, sh, w_bf16, q_scale, with_kmean):
    tiles_per_batch = SEQ // QKV_TM
    n_tiles = ROWS // QKV_TM
    vec = pl.BlockSpec((1, 1, D_MODEL), lambda i: (i // tiles_per_batch, 0, 0))
    out_shape = [jax.ShapeDtypeStruct((ROWS, 3 * D_MODEL), BF16)]
    out_specs = [pl.BlockSpec((QKV_TM, 3 * D_MODEL), lambda i: (i, 0))]
    if with_kmean:
        per_tile = QKV_TM // MOBA_BLOCK
        out_shape.append(jax.ShapeDtypeStruct((n_tiles, per_tile, D_MODEL), F32))
        out_specs.append(pl.BlockSpec((1, per_tile, D_MODEL), lambda i: (i, 0, 0)))
    return pl.pallas_call(
        functools.partial(_qkv_kernel, q_scale=q_scale),
        grid=(n_tiles,),
        in_specs=[
            pl.BlockSpec((QKV_TM, D_MODEL), lambda i: (i, 0)),
            pl.BlockSpec((1, D_MODEL), lambda i: (0, 0)),
            vec, vec,
            pl.BlockSpec((D_MODEL, 3 * D_MODEL), lambda i: (0, 0)),
        ],
        out_specs=out_specs,
        out_shape=out_shape,
        compiler_params=pltpu.CompilerParams(vmem_limit_bytes=VMEM_LIMIT),
        name="qkv_moba" if with_kmean else "qkv_diff",
    )(x2, g.reshape(1, D_MODEL), sc, sh, w_bf16)


_NT = (((1,), (1,)), ((), ()))
GROUP = 4
GROUP_KEYS = GROUP * TILE
SOFTMAX_ROWS = 32


def _group_rows(ref, grp):
    return ref[pl.ds(pl.multiple_of(grp * GROUP_KEYS, GROUP_KEYS), GROUP_KEYS), :]


def _store_scores(s_ref, qa_ref, keys_ref, bias_ref, own, grp):
    raw = lax.dot_general(qa_ref[...], _group_rows(keys_ref, grp), _NT, preferred_element_type=F32)
    for j in range(GROUP):
        tile = bias_ref[0, _table_index(own - (grp * GROUP + j))]
        for r in range(s_ref.shape[0] // TILE):
            s_ref[r * TILE:(r + 1) * TILE, j * TILE:(j + 1) * TILE] = (
                raw[r * TILE:(r + 1) * TILE, j * TILE:(j + 1) * TILE] + tile)


def _softmax(s_ref, p_ref, m_ref, alpha_ref):
    for c in range(s_ref.shape[0] // SOFTMAX_ROWS):
        rows = slice(c * SOFTMAX_ROWS, (c + 1) * SOFTMAX_ROWS)
        s = s_ref[rows, :]
        m_prev = m_ref[rows, :]
        m_new = jnp.maximum(m_prev, jnp.max(s, axis=1, keepdims=True))
        alpha_ref[rows, :] = jnp.exp(m_prev - m_new)
        m_ref[rows, :] = m_new
        p_ref[rows, :] = jnp.exp(s - m_new).astype(BF16)


def _accumulate(p_ref, alpha_ref, va, acc_ref):
    acc_ref[...] = alpha_ref[...] * acc_ref[...] + jnp.dot(p_ref[...], va, preferred_element_type=F32)


def _flash_attend(own, qa_ref, keys_ref, va_ref, bias_ref, s0_ref, s1_ref, p0_ref, p1_ref, alpha0_ref,
                  alpha1_ref, m_ref, acc_ref):
    s_refs, p_refs, alpha_refs = (s0_ref, s1_ref), (p0_ref, p1_ref), (alpha0_ref, alpha1_ref)
    own_grp = own // GROUP
    n_steps = own_grp + 1
    m_ref[...] = jnp.full(m_ref.shape, NEG_INF, F32)
    acc_ref[...] = jnp.zeros(acc_ref.shape, F32)

    def group_of(step):
        return jnp.maximum(own_grp - step, 0)

    def stage(step, cur, nxt):
        _store_scores(s_refs[nxt], qa_ref, keys_ref, bias_ref, own, group_of(step + 1))
        _softmax(s_refs[cur], p_refs[cur], m_ref, alpha_refs[cur])
        _accumulate(p_refs[nxt], alpha_refs[nxt], _group_rows(va_ref, group_of(step - 1)), acc_ref)

    _store_scores(s_refs[0], qa_ref, keys_ref, bias_ref, own, own_grp)
    _store_scores(s_refs[1], qa_ref, keys_ref, bias_ref, own, group_of(1))
    _softmax(s_refs[0], p_refs[0], m_ref, alpha_refs[0])

    def pair(t, carry):
        step = 2 * t + 1
        stage(step, 1, 0)

        @pl.when(step + 1 < n_steps)
        def _():
            stage(step + 1, 0, 1)

        return carry

    lax.fori_loop(0, n_steps // 2, pair, 0)

    for parity in range(2):
        @pl.when((n_steps - 1) % 2 == parity)
        def _():
            _accumulate(p_refs[parity], alpha_refs[parity], _group_rows(va_ref, 0), acc_ref)

    acc = acc_ref[...]
    return acc[:, :HEAD_DIM] / acc[:, HEAD_DIM:]


def _fill_values(v_ref, va_ref):
    ones = jnp.ones((TILE, HEAD_DIM), BF16)

    def fill(blk, carry):
        rows = pl.ds(pl.multiple_of(blk * TILE, TILE), TILE)
        va_ref[rows, :HEAD_DIM] = v_ref[rows, :]
        va_ref[rows, HEAD_DIM:] = ones
        return carry

    lax.fori_loop(0, N_KV_BLOCKS, fill, 0)


def _moba_kernel(q_ref, k_ref, v_ref, km_ref, bias_ref, o_ref, qa_ref, ka_ref, va_ref, *flash_refs):
    own = pl.program_id(2)
    q = q_ref[...]

    @pl.when(own == 0)
    def _():
        lane = lax.broadcasted_iota(jnp.int32, (TILE, HEAD_DIM), 1)

        def fill(blk, carry):
            rows = pl.ds(pl.multiple_of(blk * TILE, TILE), TILE)
            ka_ref[rows, :HEAD_DIM] = k_ref[rows, :]
            ka_ref[rows, HEAD_DIM:] = jnp.where(lane == blk, 1.0, 0.0).astype(BF16)
            return carry

        lax.fori_loop(0, N_KV_BLOCKS, fill, 0)
        _fill_values(v_ref, va_ref)

    km = jnp.concatenate(
        [km_ref[...].astype(BF16), jnp.zeros((HEAD_DIM - N_KV_BLOCKS, HEAD_DIM), BF16)], axis=0)
    gate = lax.dot_general(q, km, _NT, preferred_element_type=F32)
    col = lax.broadcasted_iota(jnp.int32, gate.shape, 1)
    colf = col.astype(F32)
    past = col < own
    g = jnp.where(past, gate, NEG_INF)
    madd = jnp.full(gate.shape, NEG_INF, F32)
    for _ in range(MOBA_TOPK):
        mx = jnp.max(g, axis=1, keepdims=True)
        first = jnp.min(jnp.where(g == mx, colf, float(HEAD_DIM)), axis=1, keepdims=True)
        hit = colf == first
        madd = jnp.where(hit & past, 0.0, madd)
        g = jnp.where(hit, -jnp.inf, g)
    madd = jnp.where(col == own, 0.0, madd)
    qa_ref[:, :HEAD_DIM] = q
    qa_ref[:, HEAD_DIM:] = madd.astype(BF16)

    att = _flash_attend(own, qa_ref, ka_ref, va_ref, bias_ref, *flash_refs)
    o_ref[...] = att.astype(BF16)


def _flash_scratch(n_rows):
    scores = pltpu.VMEM((n_rows, GROUP_KEYS), F32)
    probs = pltpu.VMEM((n_rows, GROUP_KEYS), BF16)
    rescale = pltpu.VMEM((n_rows, 1), F32)
    running_max = pltpu.VMEM((n_rows, 1), F32)
    acc = pltpu.VMEM((n_rows, 2 * HEAD_DIM), F32)
    return [scores, scores, probs, probs, rescale, rescale, running_max, acc]


def _moba_attention(qkv, kmean, bias):
    nq = N_KV_BLOCKS
    return pl.pallas_call(
        _moba_kernel,
        grid=(BATCH, N_HEADS, nq),
        in_specs=[
            pl.BlockSpec((TILE, HEAD_DIM), lambda b, h, i: (b * nq + i, h)),
            pl.BlockSpec((SEQ, HEAD_DIM), lambda b, h, i: (b, N_HEADS + h)),
            pl.BlockSpec((SEQ, HEAD_DIM), lambda b, h, i: (b, 2 * N_HEADS + h)),
            pl.BlockSpec((N_KV_BLOCKS, HEAD_DIM), lambda b, h, i: (b, h)),
            pl.BlockSpec((1, N_TABLE_TILES, TILE, TILE), lambda b, h, i: (h, 0, 0, 0)),
        ],
        out_specs=pl.BlockSpec((TILE, HEAD_DIM), lambda b, h, i: (b * nq + i, h)),
        out_shape=jax.ShapeDtypeStruct((ROWS, D_MODEL), BF16),
        scratch_shapes=[
            pltpu.VMEM((TILE, 2 * HEAD_DIM), BF16),
            pltpu.VMEM((SEQ, 2 * HEAD_DIM), BF16),
            pltpu.VMEM((SEQ, 2 * HEAD_DIM), BF16),
        ] + _flash_scratch(TILE),
        compiler_params=pltpu.CompilerParams(
            dimension_semantics=("arbitrary", "arbitrary", "arbitrary"), vmem_limit_bytes=VMEM_LIMIT),
        name="moba_attn",
    )(qkv, qkv, qkv, kmean, bias)


def _diff_kernel(q_ref, k_ref, v_ref, bias_ref, lam_ref, g_ref, o_ref, qa_ref, va_ref, *flash_refs,
                 lambda_init):
    own = pl.program_id(2)

    @pl.when(own == 0)
    def _():
        _fill_values(v_ref, va_ref)

    q = q_ref[...]
    lane = lax.broadcasted_iota(jnp.int32, q.shape, 1)
    zero = jnp.zeros_like(q)
    qa_ref[:TILE, :] = jnp.where(lane < DIFF_HALF, q, zero)
    qa_ref[TILE:, :] = jnp.where(lane >= DIFF_HALF, q, zero)

    att = _flash_attend(own, qa_ref, k_ref, va_ref, bias_ref, *flash_refs)

    lam = lam_ref[...]
    lam_full = (jnp.exp(jnp.sum(lam[0:1] * lam[1:2], axis=1, keepdims=True))
                - jnp.exp(jnp.sum(lam[2:3] * lam[3:4], axis=1, keepdims=True)) + lambda_init)
    o = att[:TILE] - lam_full * att[TILE:]
    o = o * lax.rsqrt(jnp.mean(o * o, axis=-1, keepdims=True) + RMS_EPS) * g_ref[...]
    o_ref[...] = (o * (1.0 - lambda_init)).astype(BF16)


def _diff_attention(qkv, bias, lam, subln_g, layer_idx):
    nq = SEQ // TILE
    lambda_init = 0.8 - 0.6 * math.exp(-0.3 * layer_idx)
    return pl.pallas_call(
        functools.partial(_diff_kernel, lambda_init=lambda_init),
        grid=(BATCH, N_HEADS, nq),
        in_specs=[
            pl.BlockSpec((TILE, HEAD_DIM), lambda b, h, i: (b * nq + i, h)),
            pl.BlockSpec((SEQ, HEAD_DIM), lambda b, h, i: (b, N_HEADS + h)),
            pl.BlockSpec((SEQ, HEAD_DIM), lambda b, h, i: (b, 2 * N_HEADS + h)),
            pl.BlockSpec((1, N_TABLE_TILES, TILE, TILE), lambda b, h, i: (h, 0, 0, 0)),
            pl.BlockSpec((4, DIFF_HALF), lambda b, h, i: (0, 0)),
            pl.BlockSpec((1, HEAD_DIM), lambda b, h, i: (0, 0)),
        ],
        out_specs=pl.BlockSpec((TILE, HEAD_DIM), lambda b, h, i: (b * nq + i, h)),
        out_shape=jax.ShapeDtypeStruct((ROWS, D_MODEL), BF16),
        scratch_shapes=[
            pltpu.VMEM((2 * TILE, HEAD_DIM), BF16),
            pltpu.VMEM((SEQ, 2 * HEAD_DIM), BF16),
        ] + _flash_scratch(2 * TILE),
        compiler_params=pltpu.CompilerParams(
            dimension_semantics=("arbitrary", "arbitrary", "arbitrary"), vmem_limit_bytes=VMEM_LIMIT),
        name="diff_attn",
    )(qkv, qkv, qkv, bias, lam, subln_g.reshape(1, HEAD_DIM))


OUT_TM = 512


def _rms(y, g):
    return y * lax.rsqrt(jnp.mean(y * y, axis=-1, keepdims=True) + RMS_EPS) * g


def _outproj_kernel(o_ref, x_ref, w_ref, g_ref, gate_ref, out_ref):
    y = jnp.dot(o_ref[...], w_ref[...], preferred_element_type=F32)
    out_ref[...] = x_ref[...] + gate_ref[0] * _rms(y, g_ref[...])


def _out_proj(o, x2, w_bf16, g, gate):
    tiles_per_batch = SEQ // OUT_TM
    return pl.pallas_call(
        _outproj_kernel,
        grid=(ROWS // OUT_TM,),
        in_specs=[
            pl.BlockSpec((OUT_TM, D_MODEL), lambda i: (i, 0)),
            pl.BlockSpec((OUT_TM, D_MODEL), lambda i: (i, 0)),
            pl.BlockSpec((D_MODEL, D_MODEL), lambda i: (0, 0)),
            pl.BlockSpec((1, D_MODEL), lambda i: (0, 0)),
            pl.BlockSpec((1, 1, D_MODEL), lambda i: (i // tiles_per_batch, 0, 0)),
        ],
        out_specs=pl.BlockSpec((OUT_TM, D_MODEL), lambda i: (i, 0)),
        out_shape=jax.ShapeDtypeStruct((ROWS, D_MODEL), F32),
        compiler_params=pltpu.CompilerParams(vmem_limit_bytes=VMEM_LIMIT),
        name="out_proj",
    )(o, x2, w_bf16, g.reshape(1, D_MODEL), gate)


FFN_TM = 1024
FFN_TF = 256


def _ffn_kernel(x_ref, g2_ref, sc_ref, sh_ref, wg_ref, wu_ref, wo_ref, g3_ref, gate_ref, out_ref,
                h_ref, acc_ref):
    c = pl.program_id(1)

    @pl.when(c == 0)
    def _():
        h_ref[...] = _norm_mod(x_ref[...], g2_ref[...], sc_ref[0], sh_ref[0]).astype(BF16)

    h = h_ref[...]
    gp = jnp.dot(h, wg_ref[...], preferred_element_type=F32)
    up = jnp.dot(h, wu_ref[...], preferred_element_type=F32)
    act = (gp / (1.0 + jnp.exp(-gp)) * up).astype(BF16)
    part = jnp.dot(act, wo_ref[...], preferred_element_type=F32)

    @pl.when(c == 0)
    def _():
        acc_ref[...] = part

    @pl.when(c > 0)
    def _():
        acc_ref[...] += part

    @pl.when(c == pl.num_programs(1) - 1)
    def _():
        out_ref[...] = x_ref[...] + gate_ref[0] * _rms(acc_ref[...], g3_ref[...])


def _ffn(x2, g2, sc, sh, w_in_bf16, w_out_bf16, g3, gate):
    tiles_per_batch = SEQ // FFN_TM
    n_chunks = D_FF // FFN_TF
    vec = pl.BlockSpec((1, 1, D_MODEL), lambda i, c: (i // tiles_per_batch, 0, 0))
    gvec = pl.BlockSpec((1, D_MODEL), lambda i, c: (0, 0))
    return pl.pallas_call(
        _ffn_kernel,
        grid=(ROWS // FFN_TM, n_chunks),
        in_specs=[
            pl.BlockSpec((FFN_TM, D_MODEL), lambda i, c: (i, 0)),
            gvec, vec, vec,
            pl.BlockSpec((D_MODEL, FFN_TF), lambda i, c: (0, c)),
            pl.BlockSpec((D_MODEL, FFN_TF), lambda i, c: (0, n_chunks + c)),
            pl.BlockSpec((FFN_TF, D_MODEL), lambda i, c: (c, 0)),
            gvec, vec,
        ],
        out_specs=pl.BlockSpec((FFN_TM, D_MODEL), lambda i, c: (i, 0)),
        out_shape=jax.ShapeDtypeStruct((ROWS, D_MODEL), F32),
        scratch_shapes=[
            pltpu.VMEM((FFN_TM, D_MODEL), BF16),
            pltpu.VMEM((FFN_TM, D_MODEL), F32),
        ],
        compiler_params=pltpu.CompilerParams(
            dimension_semantics=("parallel", "arbitrary"), vmem_limit_bytes=VMEM_LIMIT),
        name="ffn",
    )(x2, g2.reshape(1, D_MODEL), sc, sh, w_in_bf16, w_in_bf16, w_out_bf16, g3.reshape(1, D_MODEL), gate)


def kernel(x, c, rel_bias, ada_w, ada_b, norm_g, moba_w_qkv, moba_w_o, diff_w_qkv, diff_w_o, diff_lambda,
           diff_subln_g, ffn_w_in, ffn_w_out):
    x2 = x.reshape(ROWS, D_MODEL)
    mod = _adaln_mod(c, ada_w, ada_b)
    bias = _bias_tiles(rel_bias)
    for i in range(DEPTH):
        sh_a, sc_a, g_a, sh_f, sc_f, g_f = [
            mod[i, :, j * D_MODEL:(j + 1) * D_MODEL].reshape(BATCH, 1, D_MODEL) for j in range(6)]
        if i % 2 == 0:
            qkv, kmean = _qkv_proj(x2, norm_g[i, 0], sc_a, sh_a, moba_w_qkv[i // 2].astype(BF16),
                                   HEAD_DIM ** -0.5, True)
            o = _moba_attention(qkv, kmean.reshape(BATCH * N_KV_BLOCKS, D_MODEL), bias)
            w_o = moba_w_o[i // 2]
        else:
            (qkv,) = _qkv_proj(x2, norm_g[i, 0], sc_a, sh_a, diff_w_qkv[i // 2].astype(BF16),
                               DIFF_HALF ** -0.5, False)
            o = _diff_attention(qkv, bias, diff_lambda[i // 2], diff_subln_g[i // 2], i)
            w_o = diff_w_o[i // 2]
        x2 = _out_proj(o, x2, w_o.astype(BF16), norm_g[i, 1], g_a)
        x2 = _ffn(x2, norm_g[i, 2], sc_f, sh_f, ffn_w_in[i].astype(BF16), ffn_w_out[i].astype(BF16),
                  norm_g[i, 3], g_f)
    return x2.reshape(BATCH, SEQ, D_MODEL)
```

```python
import functools
import math

import numpy as np
import jax
import jax.numpy as jnp
from jax import lax
from jax.experimental import pallas as pl
from jax.experimental.pallas import tpu as pltpu

D_MODEL = 1024
BATCH = 2
SEQ = 8192
DEPTH = 2
N_HEADS = 8
HEAD_DIM = D_MODEL // N_HEADS
DIFF_HALF = HEAD_DIM // 2
MOBA_BLOCK = 256
MOBA_TOPK = 3
D_FF = 2816
N_BUCKETS = 32
MAX_EXACT = N_BUCKETS // 2
MAX_DISTANCE = 2048
RMS_EPS = 1e-6
NEG_INF = -1e30

ROWS = BATCH * SEQ
N_KV_BLOCKS = SEQ // MOBA_BLOCK
TILE = MOBA_BLOCK
N_TILES = SEQ // TILE
VMEM_LIMIT = 48 * 1024 * 1024

F32 = jnp.float32
BF16 = jnp.bfloat16


def _bucket_of_distance():
    n = np.arange(MAX_DISTANCE + 1)
    nf = np.maximum(n, 1).astype(np.float64)
    val = np.log(nf / MAX_EXACT) / math.log(MAX_DISTANCE / MAX_EXACT) * (N_BUCKETS - MAX_EXACT)
    frac = np.abs(val - np.round(val))
    assert np.all((frac > 5e-5) | (n <= MAX_EXACT) | (n == MAX_DISTANCE))
    large = np.minimum(MAX_EXACT + np.floor(val + 1e-9).astype(np.int64), N_BUCKETS - 1)
    return np.where(n < MAX_EXACT, n, large)


_BUCKETS = _bucket_of_distance()
FAR_DISTANCE = int(np.min(np.nonzero(_BUCKETS == N_BUCKETS - 1)[0]))
N_BIAS_TILES = (FAR_DISTANCE + TILE - 1) // TILE + 1
N_TABLE_TILES = N_BIAS_TILES + 2


def _bias_bucket_tiles():
    i = np.arange(TILE)[:, None]
    j = np.arange(TILE)[None, :]
    out = []
    for t in range(-1, N_BIAS_TILES + 1):
        rel = t * TILE + i - j
        out.append(np.where(rel >= 0, _BUCKETS[np.clip(rel, 0, MAX_DISTANCE)], -1))
    return np.stack(out).astype(np.int32)


def _table_index(tiles_behind):
    return jnp.clip(tiles_behind, -1, N_BIAS_TILES) + 1


MOD_TN = 512


def _mod_kernel(ct_ref, w_ref, b_ref, o_ref):
    ct = ct_ref[...]
    cact = ct / (1.0 + jnp.exp(-ct))
    w = w_ref[0]
    for b in range(BATCH):
        row = jnp.sum(w * cact[:, b:b + 1], axis=0, keepdims=True)
        o_ref[0, b:b + 1, :] = row + b_ref[0]


def _adaln_mod(c, ada_w, ada_b):
    n_out = ada_w.shape[-1]
    return pl.pallas_call(
        _mod_kernel,
        grid=(DEPTH, n_out // MOD_TN),
        in_specs=[
            pl.BlockSpec((D_MODEL, BATCH), lambda i, n: (0, 0)),
            pl.BlockSpec((1, D_MODEL, MOD_TN), lambda i, n: (i, 0, n)),
            pl.BlockSpec((1, 1, MOD_TN), lambda i, n: (i, 0, n)),
        ],
        out_specs=pl.BlockSpec((1, BATCH, MOD_TN), lambda i, n: (i, 0, n)),
        out_shape=jax.ShapeDtypeStruct((DEPTH, BATCH, n_out), F32),
        name="adaln_mod",
    )(c.T, ada_w, ada_b.reshape(DEPTH, 1, n_out))


def _bias_kernel(tab_ref, bkt_ref, o_ref):
    h = pl.program_id(0)
    bkt = bkt_ref[0]
    far = tab_ref[N_BUCKETS - 1, h]
    acc = jnp.full(bkt.shape, NEG_INF, F32)
    for b in range(N_BUCKETS):
        acc = jnp.where(bkt == b, tab_ref[b, h] - far, acc)
    o_ref[0, 0] = acc


def _bias_tiles(rel_bias):
    bkt = jnp.asarray(_bias_bucket_tiles())
    return pl.pallas_call(
        _bias_kernel,
        grid=(N_HEADS, N_TABLE_TILES),
        in_specs=[
            pl.BlockSpec(memory_space=pltpu.SMEM),
            pl.BlockSpec((1, TILE, TILE), lambda h, t: (t, 0, 0)),
        ],
        out_specs=pl.BlockSpec((1, 1, TILE, TILE), lambda h, t: (h, t, 0, 0)),
        out_shape=jax.ShapeDtypeStruct((N_HEADS, N_TABLE_TILES, TILE, TILE), F32),
        name="bias_tiles",
    )(rel_bias, bkt)


QKV_TM = 512


def _norm_mod(x, g, sc, sh):
    y = x * lax.rsqrt(jnp.mean(x * x, axis=-1, keepdims=True) + RMS_EPS)
    return (y * g) * (1.0 + sc) + sh


def _qkv_kernel(x_ref, g_ref, sc_ref, sh_ref, w_ref, o_ref, *km_ref, q_scale):
    hb = _norm_mod(x_ref[...], g_ref[...], sc_ref[0], sh_ref[0]).astype(BF16)
    for n in range(3):
        r = jnp.dot(hb, w_ref[:, n * D_MODEL:(n + 1) * D_MODEL], preferred_element_type=F32)
        if n == 0:
            r = r * q_scale
        if n == 1 and km_ref:
            km_ref[0][0] = jnp.mean(r.reshape(QKV_TM // MOBA_BLOCK, MOBA_BLOCK, D_MODEL), axis=1)
        o_ref[:, n * D_MODEL:(n + 1) * D_MODEL] = r.astype(BF16)


def _qkv_proj(x2, g, sc, sh, w_bf16, q_scale, with_kmean):
    tiles_per_batch = SEQ // QKV_TM
    n_tiles = ROWS // QKV_TM
    vec = pl.BlockSpec((1, 1, D_MODEL), lambda i: (i // tiles_per_batch, 0, 0))
    out_shape = [jax.ShapeDtypeStruct((ROWS, 3 * D_MODEL), BF16)]
    out_specs = [pl.BlockSpec((QKV_TM, 3 * D_MODEL), lambda i: (i, 0))]
    if with_kmean:
        per_tile = QKV_TM // MOBA_BLOCK
        out_shape.append(jax.ShapeDtypeStruct((n_tiles, per_tile, D_MODEL), F32))
        out_specs.append(pl.BlockSpec((1, per_tile, D_MODEL), lambda i: (i, 0, 0)))
    return pl.pallas_call(
        functools.partial(_qkv_kernel, q_scale=q_scale),
        grid=(n_tiles,),
        in_specs=[
            pl.BlockSpec((QKV_TM, D_MODEL), lambda i: (i, 0)),
            pl.BlockSpec((1, D_MODEL), lambda i: (0, 0)),
            vec, vec,
            pl.BlockSpec((D_MODEL, 3 * D_MODEL), lambda i: (0, 0)),
        ],
        out_specs=out_specs,
        out_shape=out_shape,
        compiler_params=pltpu.CompilerParams(vmem_limit_bytes=VMEM_LIMIT),
        name="qkv_moba" if with_kmean else "qkv_diff",
    )(x2, g.reshape(1, D_MODEL), sc, sh, w_bf16)


_NT = (((1,), (1,)), ((), ()))
GROUP = 4
GROUP_KEYS = GROUP * TILE
N_GROUPS = SEQ // GROUP_KEYS
SOFTMAX_ROWS = 32


def _flash_scratch(tile_rows, n_state_tiles):
    scores = pltpu.VMEM((tile_rows, GROUP_KEYS), F32)
    probs = pltpu.VMEM((tile_rows, GROUP_KEYS), BF16)
    rescale = pltpu.VMEM((tile_rows, 1), F32)
    running_max = pltpu.VMEM((n_state_tiles * tile_rows, 1), F32)
    acc = pltpu.VMEM((n_state_tiles * tile_rows, 2 * HEAD_DIM), F32)
    return [scores, scores, probs, probs, rescale, rescale, running_max, acc]


def _flash_group(first, count, base, grp, tile_rows, qa_ref, keys_ref, va_ref, bias_ref,
                 s0_ref, s1_ref, p0_ref, p1_ref, alpha0_ref, alpha1_ref, m_ref, acc_ref):
    def rows_of(tile):
        return pl.ds(pl.multiple_of((tile - base) * tile_rows, tile_rows), tile_rows)

    def scores(tile, s_ref):
        raw = lax.dot_general(qa_ref[rows_of(tile), :], keys_ref[...], _NT, preferred_element_type=F32)
        for j in range(GROUP):
            bias = bias_ref[0, _table_index(tile - (grp * GROUP + j))]
            for r in range(tile_rows // TILE):
                s_ref[r * TILE:(r + 1) * TILE, j * TILE:(j + 1) * TILE] = (
                    raw[r * TILE:(r + 1) * TILE, j * TILE:(j + 1) * TILE] + bias)

    def softmax(tile, s_ref, p_ref, alpha_ref):
        start = (tile - base) * tile_rows
        for c in range(tile_rows // SOFTMAX_ROWS):
            chunk = slice(c * SOFTMAX_ROWS, (c + 1) * SOFTMAX_ROWS)
            rows = pl.ds(pl.multiple_of(start + c * SOFTMAX_ROWS, SOFTMAX_ROWS), SOFTMAX_ROWS)
            s = s_ref[chunk, :]
            m_prev = m_ref[rows, :]
            m_new = jnp.maximum(m_prev, jnp.max(s, axis=1, keepdims=True))
            alpha_ref[chunk, :] = jnp.exp(m_prev - m_new)
            m_ref[rows, :] = m_new
            p_ref[chunk, :] = jnp.exp(s - m_new).astype(BF16)

    def accumulate(tile, p_ref, alpha_ref):
        rows = rows_of(tile)
        acc_ref[rows, :] = (alpha_ref[...] * acc_ref[rows, :]
                            + jnp.dot(p_ref[...], va_ref[...], preferred_element_type=F32))

    last = first + count - 1
    scores(first, s0_ref)
    scores(first + 1, s1_ref)
    softmax(first, s0_ref, p0_ref, alpha0_ref)

    def pair(t, carry):
        tile = first + 1 + 2 * t
        scores(tile + 1, s0_ref)
        softmax(tile, s1_ref, p1_ref, alpha1_ref)
        accumulate(tile - 1, p0_ref, alpha0_ref)
        scores(tile + 2, s1_ref)
        softmax(tile + 1, s0_ref, p0_ref, alpha0_ref)
        accumulate(tile, p1_ref, alpha1_ref)
        return carry

    lax.fori_loop(0, (count - 2) // 2, pair, 0)
    softmax(last, s1_ref, p1_ref, alpha1_ref)
    accumulate(last - 1, p0_ref, alpha0_ref)
    accumulate(last, p1_ref, alpha1_ref)


def _fill_values(v_ref, va_ref):
    va_ref[:, :HEAD_DIM] = v_ref[...]
    va_ref[:, HEAD_DIM:] = jnp.ones(v_ref.shape, BF16)


def _init_state(m_ref, acc_ref):
    m_ref[...] = jnp.full(m_ref.shape, NEG_INF, F32)
    acc_ref[...] = jnp.zeros(acc_ref.shape, F32)


def _moba_gating(q_ref, km_ref, qa_ref):
    km = jnp.concatenate(
        [km_ref[...].astype(BF16), jnp.zeros((HEAD_DIM - N_KV_BLOCKS, HEAD_DIM), BF16)], axis=0)
    blk = lax.broadcasted_iota(jnp.int32, (N_KV_BLOCKS, TILE), 0)
    blkf = blk.astype(F32)

    def gate_tile(tile, carry):
        rows = pl.ds(pl.multiple_of(tile * TILE, TILE), TILE)
        q = q_ref[rows, :]
        gate = lax.dot_general(km, q, _NT, preferred_element_type=F32)[:N_KV_BLOCKS]
        past = blk < tile
        g = jnp.where(past, gate, NEG_INF)
        madd = jnp.full(gate.shape, NEG_INF, F32)
        for _ in range(MOBA_TOPK):
            mx = jnp.max(g, axis=0, keepdims=True)
            first = jnp.min(jnp.where(g == mx, blkf, float(N_KV_BLOCKS)), axis=0, keepdims=True)
            hit = blkf == first
            madd = jnp.where(hit & past, 0.0, madd)
            g = jnp.where(hit, -jnp.inf, g)
        madd = jnp.where(blk == tile, 0.0, madd)
        madd = jnp.concatenate([madd, jnp.zeros((HEAD_DIM - N_KV_BLOCKS, TILE), F32)], axis=0)
        qa_ref[rows, :HEAD_DIM] = q
        qa_ref[rows, HEAD_DIM:] = madd.T.astype(BF16)
        return carry

    lax.fori_loop(0, N_TILES, gate_tile, 0)


def _moba_kernel(q_ref, k_ref, v_ref, km_ref, bias_ref, o_ref, qa_ref, ka_ref, va_ref, *flash_refs):
    grp = pl.program_id(2)
    m_ref, acc_ref = flash_refs[-2:]

    @pl.when(grp == 0)
    def _():
        _moba_gating(q_ref, km_ref, qa_ref)
        _init_state(m_ref, acc_ref)

    lane = lax.broadcasted_iota(jnp.int32, (GROUP_KEYS, HEAD_DIM), 1)
    key_block = grp * GROUP + lax.broadcasted_iota(jnp.int32, (GROUP_KEYS, HEAD_DIM), 0) // TILE
    ka_ref[:, :HEAD_DIM] = k_ref[...]
    ka_ref[:, HEAD_DIM:] = jnp.where(lane == key_block, 1.0, 0.0).astype(BF16)
    _fill_values(v_ref, va_ref)

    first = grp * GROUP
    _flash_group(first, N_TILES - first, 0, grp, TILE, qa_ref, ka_ref, va_ref, bias_ref, *flash_refs)

    @pl.when(grp == N_GROUPS - 1)
    def _():
        def finish(tile, carry):
            rows = pl.ds(pl.multiple_of(tile * TILE, TILE), TILE)
            acc = acc_ref[rows, :]
            o_ref[rows, :] = (acc[:, :HEAD_DIM] / acc[:, HEAD_DIM:]).astype(BF16)
            return carry

        lax.fori_loop(0, N_TILES, finish, 0)


def _moba_attention(qkv, kmean, bias):
    return pl.pallas_call(
        _moba_kernel,
        grid=(BATCH, N_HEADS, N_GROUPS),
        in_specs=[
            pl.BlockSpec((SEQ, HEAD_DIM), lambda b, h, g: (b, h)),
            pl.BlockSpec((GROUP_KEYS, HEAD_DIM), lambda b, h, g: (b * N_GROUPS + g, N_HEADS + h)),
            pl.BlockSpec((GROUP_KEYS, HEAD_DIM), lambda b, h, g: (b * N_GROUPS + g, 2 * N_HEADS + h)),
            pl.BlockSpec((N_KV_BLOCKS, HEAD_DIM), lambda b, h, g: (b, h)),
            pl.BlockSpec((1, N_TABLE_TILES, TILE, TILE), lambda b, h, g: (h, 0, 0, 0)),
        ],
        out_specs=pl.BlockSpec((SEQ, HEAD_DIM), lambda b, h, g: (b, h)),
        out_shape=jax.ShapeDtypeStruct((ROWS, D_MODEL), BF16),
        scratch_shapes=[
            pltpu.VMEM((SEQ, 2 * HEAD_DIM), BF16),
            pltpu.VMEM((GROUP_KEYS, 2 * HEAD_DIM), BF16),
            pltpu.VMEM((GROUP_KEYS, 2 * HEAD_DIM), BF16),
        ] + _flash_scratch(TILE, N_TILES),
        compiler_params=pltpu.CompilerParams(
            dimension_semantics=("arbitrary", "arbitrary", "arbitrary"), vmem_limit_bytes=VMEM_LIMIT),
        name="moba_attn",
    )(qkv, qkv, qkv, kmean, bias)


DIFF_Q_SPLIT = 2
DIFF_TILES = N_TILES // DIFF_Q_SPLIT
DIFF_ROWS = SEQ // DIFF_Q_SPLIT


def _diff_last_group(q_part):
    return ((q_part + 1) * DIFF_TILES - 1) // GROUP


def _diff_kernel(q_ref, k_ref, v_ref, bias_ref, lam_ref, g_ref, o_ref, qa_ref, va_ref, *flash_refs,
                 lambda_init):
    q_part = pl.program_id(2)
    grp = pl.program_id(3)
    m_ref, acc_ref = flash_refs[-2:]
    base = q_part * DIFF_TILES
    last_grp = _diff_last_group(q_part)

    @pl.when(grp == 0)
    def _():
        lane = lax.broadcasted_iota(jnp.int32, (TILE, HEAD_DIM), 1)
        zero = jnp.zeros((TILE, HEAD_DIM), BF16)

        def split(tile, carry):
            q = q_ref[pl.ds(pl.multiple_of(tile * TILE, TILE), TILE), :]
            qa_ref[pl.ds(pl.multiple_of(2 * tile * TILE, TILE), TILE), :] = jnp.where(lane < DIFF_HALF, q, zero)
            qa_ref[pl.ds(pl.multiple_of((2 * tile + 1) * TILE, TILE), TILE), :] = (
                jnp.where(lane >= DIFF_HALF, q, zero))
            return carry

        lax.fori_loop(0, DIFF_TILES, split, 0)
        _init_state(m_ref, acc_ref)

    @pl.when(grp <= last_grp)
    def _():
        _fill_values(v_ref, va_ref)
        first = jnp.maximum(grp * GROUP, base)
        _flash_group(first, base + DIFF_TILES - first, base, grp, 2 * TILE, qa_ref, k_ref, va_ref, bias_ref,
                     *flash_refs)

    @pl.when(grp == last_grp)
    def _():
        lam = lam_ref[...]
        lam_full = (jnp.exp(jnp.sum(lam[0:1] * lam[1:2], axis=1, keepdims=True))
                    - jnp.exp(jnp.sum(lam[2:3] * lam[3:4], axis=1, keepdims=True)) + lambda_init)

        def finish(tile, carry):
            acc = acc_ref[pl.ds(pl.multiple_of(2 * tile * TILE, 2 * TILE), 2 * TILE), :]
            att = acc[:, :HEAD_DIM] / acc[:, HEAD_DIM:]
            o = att[:TILE] - lam_full * att[TILE:]
            o = o * lax.rsqrt(jnp.mean(o * o, axis=-1, keepdims=True) + RMS_EPS) * g_ref[...]
            o_ref[pl.ds(pl.multiple_of(tile * TILE, TILE), TILE), :] = (o * (1.0 - lambda_init)).astype(BF16)
            return carry

        lax.fori_loop(0, DIFF_TILES, finish, 0)


def _diff_attention(qkv, bias, lam, subln_g, layer_idx):
    lambda_init = 0.8 - 0.6 * math.exp(-0.3 * layer_idx)

    def kv_rows(b, p, g):
        return b * N_GROUPS + jnp.minimum(g, _diff_last_group(p))

    return pl.pallas_call(
        functools.partial(_diff_kernel, lambda_init=lambda_init),
        grid=(BATCH, N_HEADS, DIFF_Q_SPLIT, N_GROUPS),
        in_specs=[
            pl.BlockSpec((DIFF_ROWS, HEAD_DIM), lambda b, h, p, g: (b * DIFF_Q_SPLIT + p, h)),
            pl.BlockSpec((GROUP_KEYS, HEAD_DIM), lambda b, h, p, g: (kv_rows(b, p, g), N_HEADS + h)),
            pl.BlockSpec((GROUP_KEYS, HEAD_DIM), lambda b, h, p, g: (kv_rows(b, p, g), 2 * N_HEADS + h)),
            pl.BlockSpec((1, N_TABLE_TILES, TILE, TILE), lambda b, h, p, g: (h, 0, 0, 0)),
            pl.BlockSpec((4, DIFF_HALF), lambda b, h, p, g: (0, 0)),
            pl.BlockSpec((1, HEAD_DIM), lambda b, h, p, g: (0, 0)),
        ],
        out_specs=pl.BlockSpec((DIFF_ROWS, HEAD_DIM), lambda b, h, p, g: (b * DIFF_Q_SPLIT + p, h)),
        out_shape=jax.ShapeDtypeStruct((ROWS, D_MODEL), BF16),
        scratch_shapes=[
            pltpu.VMEM((2 * DIFF_ROWS, HEAD_DIM), BF16),
            pltpu.VMEM((GROUP_KEYS, 2 * HEAD_DIM), BF16),
        ] + _flash_scratch(2 * TILE, DIFF_TILES),
        compiler_params=pltpu.CompilerParams(
            dimension_semantics=("arbitrary", "arbitrary", "arbitrary", "arbitrary"),
            vmem_limit_bytes=VMEM_LIMIT),
        name="diff_attn",
    )(qkv, qkv, qkv, bias, lam, subln_g.reshape(1, HEAD_DIM))


OUT_TM = 512


def _rms(y, g):
    return y * lax.rsqrt(jnp.mean(y * y, axis=-1, keepdims=True) + RMS_EPS) * g


def _outproj_kernel(o_ref, x_ref, w_ref, g_ref, gate_ref, out_ref):
    y = jnp.dot(o_ref[...], w_ref[...], preferred_element_type=F32)
    out_ref[...] = x_ref[...] + gate_ref[0] * _rms(y, g_ref[...])


def _out_proj(o, x2, w_bf16, g, gate):
    tiles_per_batch = SEQ // OUT_TM
    return pl.pallas_call(
        _outproj_kernel,
        grid=(ROWS // OUT_TM,),
        in_specs=[
            pl.BlockSpec((OUT_TM, D_MODEL), lambda i: (i, 0)),
            pl.BlockSpec((OUT_TM, D_MODEL), lambda i: (i, 0)),
            pl.BlockSpec((D_MODEL, D_MODEL), lambda i: (0, 0)),
            pl.BlockSpec((1, D_MODEL), lambda i: (0, 0)),
            pl.BlockSpec((1, 1, D_MODEL), lambda i: (i // tiles_per_batch, 0, 0)),
        ],
        out_specs=pl.BlockSpec((OUT_TM, D_MODEL), lambda i: (i, 0)),
        out_shape=jax.ShapeDtypeStruct((ROWS, D_MODEL), F32),
        compiler_params=pltpu.CompilerParams(vmem_limit_bytes=VMEM_LIMIT),
        name="out_proj",
    )(o, x2, w_bf16, g.reshape(1, D_MODEL), gate)


FFN_TM = 1024
FFN_TF = 256


def _ffn_kernel(x_ref, g2_ref, sc_ref, sh_ref, wg_ref, wu_ref, wo_ref, g3_ref, gate_ref, out_ref,
                h_ref, acc_ref):
    c = pl.program_id(1)

    @pl.when(c == 0)
    def _():
        h_ref[...] = _norm_mod(x_ref[...], g2_ref[...], sc_ref[0], sh_ref[0]).astype(BF16)

    h = h_ref[...]
    gp = jnp.dot(h, wg_ref[...], preferred_element_type=F32)
    up = jnp.dot(h, wu_ref[...], preferred_element_type=F32)
    act = (gp / (1.0 + jnp.exp(-gp)) * up).astype(BF16)
    part = jnp.dot(act, wo_ref[...], preferred_element_type=F32)

    @pl.when(c == 0)
    def _():
        acc_ref[...] = part

    @pl.when(c > 0)
    def _():
        acc_ref[...] += part

    @pl.when(c == pl.num_programs(1) - 1)
    def _():
        out_ref[...] = x_ref[...] + gate_ref[0] * _rms(acc_ref[...], g3_ref[...])


def _ffn(x2, g2, sc, sh, w_in_bf16, w_out_bf16, g3, gate):
    tiles_per_batch = SEQ // FFN_TM
    n_chunks = D_FF // FFN_TF
    vec = pl.BlockSpec((1, 1, D_MODEL), lambda i, c: (i // tiles_per_batch, 0, 0))
    gvec = pl.BlockSpec((1, D_MODEL), lambda i, c: (0, 0))
    return pl.pallas_call(
        _ffn_kernel,
        grid=(ROWS // FFN_TM, n_chunks),
        in_specs=[
            pl.BlockSpec((FFN_TM, D_MODEL), lambda i, c: (i, 0)),
            gvec, vec, vec,
            pl.BlockSpec((D_MODEL, FFN_TF), lambda i, c: (0, c)),
            pl.BlockSpec((D_MODEL, FFN_TF), lambda i, c: (0, n_chunks + c)),
            pl.BlockSpec((FFN_TF, D_MODEL), lambda i, c: (c, 0)),
            gvec, vec,
        ],
        out_specs=pl.BlockSpec((FFN_TM, D_MODEL), lambda i, c: (i, 0)),
        out_shape=jax.ShapeDtypeStruct((ROWS, D_MODEL), F32),
        scratch_shapes=[
            pltpu.VMEM((FFN_TM, D_MODEL), BF16),
            pltpu.VMEM((FFN_TM, D_MODEL), F32),
        ],
        compiler_params=pltpu.CompilerParams(
            dimension_semantics=("parallel", "arbitrary"), vmem_limit_bytes=VMEM_LIMIT),
        name="ffn",
    )(x2, g2.reshape(1, D_MODEL), sc, sh, w_in_bf16, w_in_bf16, w_out_bf16, g3.reshape(1, D_MODEL), gate)


def kernel(x, c, rel_bias, ada_w, ada_b, norm_g, moba_w_qkv, moba_w_o, diff_w_qkv, diff_w_o, diff_lambda,
           diff_subln_g, ffn_w_in, ffn_w_out):
    x2 = x.reshape(ROWS, D_MODEL)
    mod = _adaln_mod(c, ada_w, ada_b)
    bias = _bias_tiles(rel_bias)
    for i in range(DEPTH):
        sh_a, sc_a, g_a, sh_f, sc_f, g_f = [
            mod[i, :, j * D_MODEL:(j + 1) * D_MODEL].reshape(BATCH, 1, D_MODEL) for j in range(6)]
        if i % 2 == 0:
            qkv, kmean = _qkv_proj(x2, norm_g[i, 0], sc_a, sh_a, moba_w_qkv[i // 2].astype(BF16),
                                   HEAD_DIM ** -0.5, True)
            o = _moba_attention(qkv, kmean.reshape(BATCH * N_KV_BLOCKS, D_MODEL), bias)
            w_o = moba_w_o[i // 2]
        else:
            (qkv,) = _qkv_proj(x2, norm_g[i, 0], sc_a, sh_a, diff_w_qkv[i // 2].astype(BF16),
                               DIFF_HALF ** -0.5, False)
            o = _diff_attention(qkv, bias, diff_lambda[i // 2], diff_subln_g[i // 2], i)
            w_o = diff_w_o[i // 2]
        x2 = _out_proj(o, x2, w_o.astype(BF16), norm_g[i, 1], g_a)
        x2 = _ffn(x2, norm_g[i, 2], sc_f, sh_f, ffn_w_in[i].astype(BF16), ffn_w_out[i].astype(BF16),
                  norm_g[i, 3], g_f)
    return x2.reshape(BATCH, SEQ, D_MODEL)
```

```python
import functools
import math

import numpy as np
import jax
import jax.numpy as jnp
from jax import lax
from jax.experimental import pallas as pl
from jax.experimental.pallas import tpu as pltpu

D_MODEL = 1024
BATCH = 2
SEQ = 8192
DEPTH = 2
N_HEADS = 8
HEAD_DIM = D_MODEL // N_HEADS
DIFF_HALF = HEAD_DIM // 2
MOBA_BLOCK = 256
MOBA_TOPK = 3
D_FF = 2816
N_BUCKETS = 32
MAX_EXACT = N_BUCKETS // 2
MAX_DISTANCE = 2048
RMS_EPS = 1e-6
NEG_INF = -1e30

ROWS = BATCH * SEQ
N_KV_BLOCKS = SEQ // MOBA_BLOCK
TILE = MOBA_BLOCK
N_TILES = SEQ // TILE
VMEM_LIMIT = 48 * 1024 * 1024

F32 = jnp.float32
BF16 = jnp.bfloat16


def _bucket_of_distance():
    n = np.arange(MAX_DISTANCE + 1)
    nf = np.maximum(n, 1).astype(np.float64)
    val = np.log(nf / MAX_EXACT) / math.log(MAX_DISTANCE / MAX_EXACT) * (N_BUCKETS - MAX_EXACT)
    frac = np.abs(val - np.round(val))
    assert np.all((frac > 5e-5) | (n <= MAX_EXACT) | (n == MAX_DISTANCE))
    large = np.minimum(MAX_EXACT + np.floor(val + 1e-9).astype(np.int64), N_BUCKETS - 1)
    return np.where(n < MAX_EXACT, n, large)


_BUCKETS = _bucket_of_distance()
FAR_DISTANCE = int(np.min(np.nonzero(_BUCKETS == N_BUCKETS - 1)[0]))
N_BIAS_TILES = (FAR_DISTANCE + TILE - 1) // TILE + 1
N_TABLE_TILES = N_BIAS_TILES + 2


def _bias_bucket_tiles():
    i = np.arange(TILE)[:, None]
    j = np.arange(TILE)[None, :]
    out = []
    for t in range(-1, N_BIAS_TILES + 1):
        rel = t * TILE + i - j
        out.append(np.where(rel >= 0, _BUCKETS[np.clip(rel, 0, MAX_DISTANCE)], -1))
    return np.stack(out).astype(np.int32)


def _table_index(tiles_behind):
    return jnp.clip(tiles_behind, -1, N_BIAS_TILES) + 1


MOD_TN = 512


def _mod_kernel(ct_ref, w_ref, b_ref, o_ref):
    ct = ct_ref[...]
    cact = ct / (1.0 + jnp.exp(-ct))
    w = w_ref[0]
    for b in range(BATCH):
        row = jnp.sum(w * cact[:, b:b + 1], axis=0, keepdims=True)
        o_ref[0, b:b + 1, :] = row + b_ref[0]


def _adaln_mod(c, ada_w, ada_b):
    n_out = ada_w.shape[-1]
    return pl.pallas_call(
        _mod_kernel,
        grid=(DEPTH, n_out // MOD_TN),
        in_specs=[
            pl.BlockSpec((D_MODEL, BATCH), lambda i, n: (0, 0)),
            pl.BlockSpec((1, D_MODEL, MOD_TN), lambda i, n: (i, 0, n)),
            pl.BlockSpec((1, 1, MOD_TN), lambda i, n: (i, 0, n)),
        ],
        out_specs=pl.BlockSpec((1, BATCH, MOD_TN), lambda i, n: (i, 0, n)),
        out_shape=jax.ShapeDtypeStruct((DEPTH, BATCH, n_out), F32),
        name="adaln_mod",
    )(c.T, ada_w, ada_b.reshape(DEPTH, 1, n_out))


def _bias_kernel(tab_ref, bkt_ref, o_ref):
    h = pl.program_id(0)
    bkt = bkt_ref[0]
    far = tab_ref[N_BUCKETS - 1, h]
    acc = jnp.full(bkt.shape, NEG_INF, F32)
    for b in range(N_BUCKETS):
        acc = jnp.where(bkt == b, tab_ref[b, h] - far, acc)
    o_ref[0, 0] = acc


def _bias_tiles(rel_bias):
    bkt = jnp.asarray(_bias_bucket_tiles())
    return pl.pallas_call(
        _bias_kernel,
        grid=(N_HEADS, N_TABLE_TILES),
        in_specs=[
            pl.BlockSpec(memory_space=pltpu.SMEM),
            pl.BlockSpec((1, TILE, TILE), lambda h, t: (t, 0, 0)),
        ],
        out_specs=pl.BlockSpec((1, 1, TILE, TILE), lambda h, t: (h, t, 0, 0)),
        out_shape=jax.ShapeDtypeStruct((N_HEADS, N_TABLE_TILES, TILE, TILE), F32),
        name="bias_tiles",
    )(rel_bias, bkt)


QKV_TM = 512


def _norm_mod(x, g, sc, sh):
    y = x * lax.rsqrt(jnp.mean(x * x, axis=-1, keepdims=True) + RMS_EPS)
    return (y * g) * (1.0 + sc) + sh


def _qkv_kernel(x_ref, g_ref, sc_ref, sh_ref, w_ref, o_ref, *km_ref, q_scale):
    hb = _norm_mod(x_ref[...], g_ref[...], sc_ref[0], sh_ref[0]).astype(BF16)
    for n in range(3):
        r = jnp.dot(hb, w_ref[:, n * D_MODEL:(n + 1) * D_MODEL], preferred_element_type=F32)
        if n == 0:
            r = r * q_scale
        if n == 1 and km_ref:
            km_ref[0][0] = jnp.mean(r.reshape(QKV_TM // MOBA_BLOCK, MOBA_BLOCK, D_MODEL), axis=1)
        o_ref[:, n * D_MODEL:(n + 1) * D_MODEL] = r.astype(BF16)


def _qkv_proj(x2, g, sc, sh, w_bf16, q_scale, with_kmean):
    tiles_per_batch = SEQ // QKV_TM
    n_tiles = ROWS // QKV_TM
    vec = pl.BlockSpec((1, 1, D_MODEL), lambda i: (i // tiles_per_batch, 0, 0))
    out_shape = [jax.ShapeDtypeStruct((ROWS, 3 * D_MODEL), BF16)]
    out_specs = [pl.BlockSpec((QKV_TM, 3 * D_MODEL), lambda i: (i, 0))]
    if with_kmean:
        per_tile = QKV_TM // MOBA_BLOCK
        out_shape.append(jax.ShapeDtypeStruct((n_tiles, per_tile, D_MODEL), F32))
        out_specs.append(pl.BlockSpec((1, per_tile, D_MODEL), lambda i: (i, 0, 0)))
    return pl.pallas_call(
        functools.partial(_qkv_kernel, q_scale=q_scale),
        grid=(n_tiles,),
        in_specs=[
            pl.BlockSpec((QKV_TM, D_MODEL), lambda i: (i, 0)),
            pl.BlockSpec((1, D_MODEL), lambda i: (0, 0)),
            vec, vec,
            pl.BlockSpec((D_MODEL, 3 * D_MODEL), lambda i: (0, 0)),
        ],
        out_specs=out_specs,
        out_shape=out_shape,
        compiler_params=pltpu.CompilerParams(vmem_limit_bytes=VMEM_LIMIT),
        name="qkv_moba" if with_kmean else "qkv_diff",
    )(x2, g.reshape(1, D_MODEL), sc, sh, w_bf16)


_NT = (((1,), (1,)), ((), ()))
GROUP = 4
GROUP_KEYS = GROUP * TILE
N_GROUPS = SEQ // GROUP_KEYS
SOFTMAX_ROWS = 32


def _flash_scratch(tile_rows, n_state_tiles):
    scores = pltpu.VMEM((tile_rows, GROUP_KEYS), F32)
    probs = pltpu.VMEM((tile_rows, GROUP_KEYS), BF16)
    rescale = pltpu.VMEM((tile_rows, 1), F32)
    running_max = pltpu.VMEM((n_state_tiles * tile_rows, 1), F32)
    acc = pltpu.VMEM((n_state_tiles * tile_rows, 2 * HEAD_DIM), F32)
    return [scores, scores, probs, probs, rescale, rescale, running_max, acc]


def _flash_group(first, count, base, grp, position_tiles, qa_ref, keys_ref, va_ref, bias_ref,
                 s0_ref, s1_ref, p0_ref, p1_ref, alpha0_ref, alpha1_ref, m_ref, acc_ref):
    tile_rows = s0_ref.shape[0]

    def rows_of(tile):
        return pl.ds(pl.multiple_of((tile - base) * tile_rows, tile_rows), tile_rows)

    def scores(tile, s_ref):
        raw = lax.dot_general(qa_ref[rows_of(tile), :], keys_ref[...], _NT, preferred_element_type=F32)
        positions = position_tiles(tile)
        for j in range(GROUP):
            biases = []
            for r, pos in enumerate(positions):
                same = [b for q, b in zip(positions[:r], biases) if q is pos]
                biases.append(same[0] if same else bias_ref[0, _table_index(pos - (grp * GROUP + j))])
                s_ref[r * TILE:(r + 1) * TILE, j * TILE:(j + 1) * TILE] = (
                    raw[r * TILE:(r + 1) * TILE, j * TILE:(j + 1) * TILE] + biases[r])

    def softmax(tile, s_ref, p_ref, alpha_ref):
        start = (tile - base) * tile_rows
        for c in range(tile_rows // SOFTMAX_ROWS):
            chunk = slice(c * SOFTMAX_ROWS, (c + 1) * SOFTMAX_ROWS)
            rows = pl.ds(pl.multiple_of(start + c * SOFTMAX_ROWS, SOFTMAX_ROWS), SOFTMAX_ROWS)
            s = s_ref[chunk, :]
            m_prev = m_ref[rows, :]
            m_new = jnp.maximum(m_prev, jnp.max(s, axis=1, keepdims=True))
            alpha_ref[chunk, :] = jnp.exp(m_prev - m_new)
            m_ref[rows, :] = m_new
            p_ref[chunk, :] = jnp.exp(s - m_new).astype(BF16)

    def accumulate(tile, p_ref, alpha_ref):
        rows = rows_of(tile)
        acc_ref[rows, :] = (alpha_ref[...] * acc_ref[rows, :]
                            + jnp.dot(p_ref[...], va_ref[...], preferred_element_type=F32))

    last = first + count - 1
    scores(first, s0_ref)
    scores(first + 1, s1_ref)
    softmax(first, s0_ref, p0_ref, alpha0_ref)

    def pair(t, carry):
        tile = first + 1 + 2 * t
        scores(tile + 1, s0_ref)
        softmax(tile, s1_ref, p1_ref, alpha1_ref)
        accumulate(tile - 1, p0_ref, alpha0_ref)
        scores(tile + 2, s1_ref)
        softmax(tile + 1, s0_ref, p0_ref, alpha0_ref)
        accumulate(tile, p1_ref, alpha1_ref)
        return carry

    lax.fori_loop(0, (count - 2) // 2, pair, 0)
    softmax(last, s1_ref, p1_ref, alpha1_ref)
    accumulate(last - 1, p0_ref, alpha0_ref)
    accumulate(last, p1_ref, alpha1_ref)


def _fill_values(v_ref, va_ref):
    va_ref[:, :HEAD_DIM] = v_ref[...]
    va_ref[:, HEAD_DIM:] = jnp.ones(v_ref.shape, BF16)


GATE_QUERIES = 4 * TILE
MOBA_PAIR = 2


def _moba_gating(q_ref, km_ref, qa_ref, m_ref, acc_ref):
    blk = lax.broadcasted_iota(jnp.int32, (N_KV_BLOCKS, GATE_QUERIES), 0)
    blkf = blk.astype(F32)
    tile_in_step = lax.broadcasted_iota(jnp.int32, (N_KV_BLOCKS, GATE_QUERIES), 1) // TILE

    def gate_step(step, carry):
        rows = pl.ds(pl.multiple_of(step * GATE_QUERIES, GATE_QUERIES), GATE_QUERIES)
        m_ref[rows, :] = jnp.full((GATE_QUERIES, 1), NEG_INF, F32)
        acc_ref[rows, :] = jnp.zeros((GATE_QUERIES, 2 * HEAD_DIM), F32)
        q = q_ref[rows, :]
        gate = lax.dot_general(km_ref[...].astype(BF16), q, _NT, preferred_element_type=F32)
        tile = step * (GATE_QUERIES // TILE) + tile_in_step
        past = blk < tile
        g = jnp.where(past, gate, NEG_INF)
        madd = jnp.full(gate.shape, NEG_INF, F32)
        for _ in range(MOBA_TOPK):
            mx = jnp.max(g, axis=0, keepdims=True)
            first = jnp.min(jnp.where(g == mx, blkf, float(N_KV_BLOCKS)), axis=0, keepdims=True)
            hit = blkf == first
            madd = jnp.where(hit & past, 0.0, madd)
            g = jnp.where(hit, -jnp.inf, g)
        madd = jnp.where(blk == tile, 0.0, madd)
        madd = jnp.concatenate([madd, jnp.zeros((HEAD_DIM - N_KV_BLOCKS, GATE_QUERIES), F32)], axis=0)
        qa_ref[rows, :HEAD_DIM] = q
        qa_ref[rows, HEAD_DIM:] = madd.T.astype(BF16)
        return carry

    lax.fori_loop(0, SEQ // GATE_QUERIES, gate_step, 0)


def _moba_kernel(q_ref, k_ref, v_ref, km_ref, bias_ref, o_ref, qa_ref, ka_ref, va_ref, *flash_refs):
    grp = pl.program_id(2)
    m_ref, acc_ref = flash_refs[-2:]

    @pl.when(grp == 0)
    def _():
        _moba_gating(q_ref, km_ref, qa_ref, m_ref, acc_ref)

    lane = lax.broadcasted_iota(jnp.int32, (GROUP_KEYS, HEAD_DIM), 1)
    key_block = grp * GROUP + lax.broadcasted_iota(jnp.int32, (GROUP_KEYS, HEAD_DIM), 0) // TILE
    ka_ref[:, :HEAD_DIM] = k_ref[...]
    ka_ref[:, HEAD_DIM:] = jnp.where(lane == key_block, 1.0, 0.0).astype(BF16)
    _fill_values(v_ref, va_ref)

    first = grp * GROUP // MOBA_PAIR
    _flash_group(first, N_TILES // MOBA_PAIR - first, 0, grp,
                 lambda tile: [tile * MOBA_PAIR + r for r in range(MOBA_PAIR)],
                 qa_ref, ka_ref, va_ref, bias_ref, *flash_refs)

    @pl.when(grp == N_GROUPS - 1)
    def _():
        def finish(tile, carry):
            rows = pl.ds(pl.multiple_of(tile * TILE, TILE), TILE)
            acc = acc_ref[rows, :]
            o_ref[rows, :] = (acc[:, :HEAD_DIM] / acc[:, HEAD_DIM:]).astype(BF16)
            return carry

        lax.fori_loop(0, N_TILES, finish, 0)


def _moba_attention(qkv, kmean, bias):
    return pl.pallas_call(
        _moba_kernel,
        grid=(BATCH, N_HEADS, N_GROUPS),
        in_specs=[
            pl.BlockSpec((SEQ, HEAD_DIM), lambda b, h, g: (b, h)),
            pl.BlockSpec((GROUP_KEYS, HEAD_DIM), lambda b, h, g: (b * N_GROUPS + g, N_HEADS + h)),
            pl.BlockSpec((GROUP_KEYS, HEAD_DIM), lambda b, h, g: (b * N_GROUPS + g, 2 * N_HEADS + h)),
            pl.BlockSpec((N_KV_BLOCKS, HEAD_DIM), lambda b, h, g: (b, h)),
            pl.BlockSpec((1, N_TABLE_TILES, TILE, TILE), lambda b, h, g: (h, 0, 0, 0)),
        ],
        out_specs=pl.BlockSpec((SEQ, HEAD_DIM), lambda b, h, g: (b, h)),
        out_shape=jax.ShapeDtypeStruct((ROWS, D_MODEL), BF16),
        scratch_shapes=[
            pltpu.VMEM((SEQ, 2 * HEAD_DIM), BF16),
            pltpu.VMEM((GROUP_KEYS, 2 * HEAD_DIM), BF16),
            pltpu.VMEM((GROUP_KEYS, 2 * HEAD_DIM), BF16),
        ] + _flash_scratch(MOBA_PAIR * TILE, N_TILES // MOBA_PAIR),
        compiler_params=pltpu.CompilerParams(
            dimension_semantics=("arbitrary", "arbitrary", "arbitrary"), vmem_limit_bytes=VMEM_LIMIT),
        name="moba_attn",
    )(qkv, qkv, qkv, kmean, bias)


DIFF_Q_SPLIT = 2
DIFF_TILES = N_TILES // DIFF_Q_SPLIT
DIFF_ROWS = SEQ // DIFF_Q_SPLIT


def _diff_last_group(q_part):
    return ((q_part + 1) * DIFF_TILES - 1) // GROUP


def _diff_kernel(q_ref, k_ref, v_ref, bias_ref, lam_ref, g_ref, o_ref, qa_ref, va_ref, *flash_refs,
                 lambda_init):
    q_part = pl.program_id(2)
    grp = pl.program_id(3)
    m_ref, acc_ref = flash_refs[-2:]
    base = q_part * DIFF_TILES
    last_grp = _diff_last_group(q_part)

    @pl.when(grp == 0)
    def _():
        lane = lax.broadcasted_iota(jnp.int32, (TILE, HEAD_DIM), 1)
        zero = jnp.zeros((TILE, HEAD_DIM), BF16)

        def split(tile, carry):
            q = q_ref[pl.ds(pl.multiple_of(tile * TILE, TILE), TILE), :]
            qa_ref[pl.ds(pl.multiple_of(2 * tile * TILE, TILE), TILE), :] = jnp.where(lane < DIFF_HALF, q, zero)
            qa_ref[pl.ds(pl.multiple_of((2 * tile + 1) * TILE, TILE), TILE), :] = (
                jnp.where(lane >= DIFF_HALF, q, zero))
            state_rows = pl.ds(pl.multiple_of(2 * tile * TILE, 2 * TILE), 2 * TILE)
            m_ref[state_rows, :] = jnp.full((2 * TILE, 1), NEG_INF, F32)
            acc_ref[state_rows, :] = jnp.zeros((2 * TILE, 2 * HEAD_DIM), F32)
            return carry

        lax.fori_loop(0, DIFF_TILES, split, 0)

    @pl.when(grp <= last_grp)
    def _():
        _fill_values(v_ref, va_ref)
        first = jnp.maximum(grp * GROUP, base)
        _flash_group(first, base + DIFF_TILES - first, base, grp, lambda tile: [tile, tile],
                     qa_ref, k_ref, va_ref, bias_ref, *flash_refs)

    @pl.when(grp == last_grp)
    def _():
        lam = lam_ref[...]
        lam_full = (jnp.exp(jnp.sum(lam[0:1] * lam[1:2], axis=1, keepdims=True))
                    - jnp.exp(jnp.sum(lam[2:3] * lam[3:4], axis=1, keepdims=True)) + lambda_init)

        def finish(tile, carry):
            acc = acc_ref[pl.ds(pl.multiple_of(2 * tile * TILE, 2 * TILE), 2 * TILE), :]
            att = acc[:, :HEAD_DIM] / acc[:, HEAD_DIM:]
            o = att[:TILE] - lam_full * att[TILE:]
            o = o * lax.rsqrt(jnp.mean(o * o, axis=-1, keepdims=True) + RMS_EPS) * g_ref[...]
            o_ref[pl.ds(pl.multiple_of(tile * TILE, TILE), TILE), :] = (o * (1.0 - lambda_init)).astype(BF16)
            return carry

        lax.fori_loop(0, DIFF_TILES, finish, 0)


def _diff_attention(qkv, bias, lam, subln_g, layer_idx):
    lambda_init = 0.8 - 0.6 * math.exp(-0.3 * layer_idx)

    def kv_rows(b, p, g):
        return b * N_GROUPS + jnp.minimum(g, _diff_last_group(p))

    return pl.pallas_call(
        functools.partial(_diff_kernel, lambda_init=lambda_init),
        grid=(BATCH, N_HEADS, DIFF_Q_SPLIT, N_GROUPS),
        in_specs=[
            pl.BlockSpec((DIFF_ROWS, HEAD_DIM), lambda b, h, p, g: (b * DIFF_Q_SPLIT + p, h)),
            pl.BlockSpec((GROUP_KEYS, HEAD_DIM), lambda b, h, p, g: (kv_rows(b, p, g), N_HEADS + h)),
            pl.BlockSpec((GROUP_KEYS, HEAD_DIM), lambda b, h, p, g: (kv_rows(b, p, g), 2 * N_HEADS + h)),
            pl.BlockSpec((1, N_TABLE_TILES, TILE, TILE), lambda b, h, p, g: (h, 0, 0, 0)),
            pl.BlockSpec((4, DIFF_HALF), lambda b, h, p, g: (0, 0)),
            pl.BlockSpec((1, HEAD_DIM), lambda b, h, p, g: (0, 0)),
        ],
        out_specs=pl.BlockSpec((DIFF_ROWS, HEAD_DIM), lambda b, h, p, g: (b * DIFF_Q_SPLIT + p, h)),
        out_shape=jax.ShapeDtypeStruct((ROWS, D_MODEL), BF16),
        scratch_shapes=[
            pltpu.VMEM((2 * DIFF_ROWS, HEAD_DIM), BF16),
            pltpu.VMEM((GROUP_KEYS, 2 * HEAD_DIM), BF16),
        ] + _flash_scratch(2 * TILE, DIFF_TILES),
        compiler_params=pltpu.CompilerParams(
            dimension_semantics=("arbitrary", "arbitrary", "arbitrary", "arbitrary"),
            vmem_limit_bytes=VMEM_LIMIT),
        name="diff_attn",
    )(qkv, qkv, qkv, bias, lam, subln_g.reshape(1, HEAD_DIM))


OUT_TM = 512


def _rms(y, g):
    return y * lax.rsqrt(jnp.mean(y * y, axis=-1, keepdims=True) + RMS_EPS) * g


def _outproj_kernel(o_ref, x_ref, w_ref, g_ref, gate_ref, out_ref):
    y = jnp.dot(o_ref[...], w_ref[...], preferred_element_type=F32)
    out_ref[...] = x_ref[...] + gate_ref[0] * _rms(y, g_ref[...])


def _out_proj(o, x2, w_bf16, g, gate):
    tiles_per_batch = SEQ // OUT_TM
    return pl.pallas_call(
        _outproj_kernel,
        grid=(ROWS // OUT_TM,),
        in_specs=[
            pl.BlockSpec((OUT_TM, D_MODEL), lambda i: (i, 0)),
            pl.BlockSpec((OUT_TM, D_MODEL), lambda i: (i, 0)),
            pl.BlockSpec((D_MODEL, D_MODEL), lambda i: (0, 0)),
            pl.BlockSpec((1, D_MODEL), lambda i: (0, 0)),
            pl.BlockSpec((1, 1, D_MODEL), lambda i: (i // tiles_per_batch, 0, 0)),
        ],
        out_specs=pl.BlockSpec((OUT_TM, D_MODEL), lambda i: (i, 0)),
        out_shape=jax.ShapeDtypeStruct((ROWS, D_MODEL), F32),
        compiler_params=pltpu.CompilerParams(vmem_limit_bytes=VMEM_LIMIT),
        name="out_proj",
    )(o, x2, w_bf16, g.reshape(1, D_MODEL), gate)


FFN_TM = 1024
FFN_TF = 256


def _ffn_kernel(x_ref, g2_ref, sc_ref, sh_ref, wg_ref, wu_ref, wo_ref, g3_ref, gate_ref, out_ref,
                h_ref, acc_ref):
    c = pl.program_id(1)

    @pl.when(c == 0)
    def _():
        h_ref[...] = _norm_mod(x_ref[...], g2_ref[...], sc_ref[0], sh_ref[0]).astype(BF16)
        acc_ref[...] = jnp.zeros(acc_ref.shape, F32)

    h = h_ref[...]
    gp = jnp.dot(h, wg_ref[...], preferred_element_type=F32)
    up = jnp.dot(h, wu_ref[...], preferred_element_type=F32)
    act = (gp / (1.0 + jnp.exp(-gp)) * up).astype(BF16)
    acc_ref[...] += jnp.dot(act, wo_ref[...], preferred_element_type=F32)

    @pl.when(c == pl.num_programs(1) - 1)
    def _():
        out_ref[...] = x_ref[...] + gate_ref[0] * _rms(acc_ref[...], g3_ref[...])


def _ffn(x2, g2, sc, sh, w_in_bf16, w_out_bf16, g3, gate):
    tiles_per_batch = SEQ // FFN_TM
    n_chunks = D_FF // FFN_TF
    vec = pl.BlockSpec((1, 1, D_MODEL), lambda i, c: (i // tiles_per_batch, 0, 0))
    gvec = pl.BlockSpec((1, D_MODEL), lambda i, c: (0, 0))
    return pl.pallas_call(
        _ffn_kernel,
        grid=(ROWS // FFN_TM, n_chunks),
        in_specs=[
            pl.BlockSpec((FFN_TM, D_MODEL), lambda i, c: (i, 0)),
            gvec, vec, vec,
            pl.BlockSpec((D_MODEL, FFN_TF), lambda i, c: (0, c)),
            pl.BlockSpec((D_MODEL, FFN_TF), lambda i, c: (0, n_chunks + c)),
            pl.BlockSpec((FFN_TF, D_MODEL), lambda i, c: (c, 0)),
            gvec, vec,
        ],
        out_specs=pl.BlockSpec((FFN_TM, D_MODEL), lambda i, c: (i, 0)),
        out_shape=jax.ShapeDtypeStruct((ROWS, D_MODEL), F32),
        scratch_shapes=[
            pltpu.VMEM((FFN_TM, D_MODEL), BF16),
            pltpu.VMEM((FFN_TM, D_MODEL), F32),
        ],
        compiler_params=pltpu.CompilerParams(
            dimension_semantics=("parallel", "arbitrary"), vmem_limit_bytes=VMEM_LIMIT),
        name="ffn",
    )(x2, g2.reshape(1, D_MODEL), sc, sh, w_in_bf16, w_in_bf16, w_out_bf16, g3.reshape(1, D_MODEL), gate)


def kernel(x, c, rel_bias, ada_w, ada_b, norm_g, moba_w_qkv, moba_w_o, diff_w_qkv, diff_w_o, diff_lambda,
           diff_subln_g, ffn_w_in, ffn_w_out):
    x2 = x.reshape(ROWS, D_MODEL)
    mod = _adaln_mod(c, ada_w, ada_b)
    bias = _bias_tiles(rel_bias)
    for i in range(DEPTH):
        sh_a, sc_a, g_a, sh_f, sc_f, g_f = [
            mod[i, :, j * D_MODEL:(j + 1) * D_MODEL].reshape(BATCH, 1, D_MODEL) for j in range(6)]
        if i % 2 == 0:
            qkv, kmean = _qkv_proj(x2, norm_g[i, 0], sc_a, sh_a, moba_w_qkv[i // 2].astype(BF16),
                                   HEAD_DIM ** -0.5, True)
            o = _moba_attention(qkv, kmean.reshape(BATCH * N_KV_BLOCKS, D_MODEL), bias)
            w_o = moba_w_o[i // 2]
        else:
            (qkv,) = _qkv_proj(x2, norm_g[i, 0], sc_a, sh_a, diff_w_qkv[i // 2].astype(BF16),
                               DIFF_HALF ** -0.5, False)
            o = _diff_attention(qkv, bias, diff_lambda[i // 2], diff_subln_g[i // 2], i)
            w_o = diff_w_o[i // 2]
        x2 = _out_proj(o, x2, w_o.astype(BF16), norm_g[i, 1], g_a)
        x2 = _ffn(x2, norm_g[i, 2], sc_f, sh_f, ffn_w_in[i].astype(BF16), ffn_w_out[i].astype(BF16),
                  norm_g[i, 3], g_f)
    return x2.reshape(BATCH, SEQ, D_MODEL)
```

```python
import functools
import math

import numpy as np
import jax
import jax.numpy as jnp
from jax import lax
from jax.experimental import pallas as pl
from jax.experimental.pallas import tpu as pltpu

D_MODEL = 1024
BATCH = 2
SEQ = 8192
DEPTH = 2
N_HEADS = 8
HEAD_DIM = D_MODEL // N_HEADS
DIFF_HALF = HEAD_DIM // 2
MOBA_BLOCK = 256
MOBA_TOPK = 3
D_FF = 2816
N_BUCKETS = 32
MAX_EXACT = N_BUCKETS // 2
MAX_DISTANCE = 2048
RMS_EPS = 1e-6
NEG_INF = -1e30

ROWS = BATCH * SEQ
N_KV_BLOCKS = SEQ // MOBA_BLOCK
TILE = MOBA_BLOCK
N_TILES = SEQ // TILE
VMEM_LIMIT = 48 * 1024 * 1024

F32 = jnp.float32
BF16 = jnp.bfloat16


def _bucket_of_distance():
    n = np.arange(MAX_DISTANCE + 1)
    nf = np.maximum(n, 1).astype(np.float64)
    val = np.log(nf / MAX_EXACT) / math.log(MAX_DISTANCE / MAX_EXACT) * (N_BUCKETS - MAX_EXACT)
    frac = np.abs(val - np.round(val))
    assert np.all((frac > 5e-5) | (n <= MAX_EXACT) | (n == MAX_DISTANCE))
    large = np.minimum(MAX_EXACT + np.floor(val + 1e-9).astype(np.int64), N_BUCKETS - 1)
    return np.where(n < MAX_EXACT, n, large)


_BUCKETS = _bucket_of_distance()
FAR_DISTANCE = int(np.min(np.nonzero(_BUCKETS == N_BUCKETS - 1)[0]))
N_BIAS_TILES = (FAR_DISTANCE + TILE - 1) // TILE + 1
N_TABLE_TILES = N_BIAS_TILES + 2


def _bias_bucket_tiles():
    i = np.arange(TILE)[:, None]
    j = np.arange(TILE)[None, :]
    out = []
    for t in range(-1, N_BIAS_TILES + 1):
        rel = t * TILE + i - j
        out.append(np.where(rel >= 0, _BUCKETS[np.clip(rel, 0, MAX_DISTANCE)], -1))
    return np.stack(out).astype(np.int32)


def _table_index(tiles_behind):
    return jnp.clip(tiles_behind, -1, N_BIAS_TILES) + 1


MOD_TN = 512


def _mod_kernel(ct_ref, w_ref, b_ref, o_ref):
    ct = ct_ref[...]
    cact = ct / (1.0 + jnp.exp(-ct))
    w = w_ref[0]
    for b in range(BATCH):
        row = jnp.sum(w * cact[:, b:b + 1], axis=0, keepdims=True)
        o_ref[0, b:b + 1, :] = row + b_ref[0]


def _adaln_mod(c, ada_w, ada_b):
    n_out = ada_w.shape[-1]
    return pl.pallas_call(
        _mod_kernel,
        grid=(DEPTH, n_out // MOD_TN),
        in_specs=[
            pl.BlockSpec((D_MODEL, BATCH), lambda i, n: (0, 0)),
            pl.BlockSpec((1, D_MODEL, MOD_TN), lambda i, n: (i, 0, n)),
            pl.BlockSpec((1, 1, MOD_TN), lambda i, n: (i, 0, n)),
        ],
        out_specs=pl.BlockSpec((1, BATCH, MOD_TN), lambda i, n: (i, 0, n)),
        out_shape=jax.ShapeDtypeStruct((DEPTH, BATCH, n_out), F32),
        name="adaln_mod",
    )(c.T, ada_w, ada_b.reshape(DEPTH, 1, n_out))


def _bias_kernel(tab_ref, bkt_ref, o_ref, *, buckets_present):
    h = pl.program_id(0)
    far = tab_ref[N_BUCKETS - 1, h]
    for t, present in enumerate(buckets_present):
        bkt = bkt_ref[t]
        acc = jnp.full(bkt.shape, NEG_INF, F32)
        for b in present:
            acc = jnp.where(bkt == b, tab_ref[b, h] - far, acc)
        o_ref[0, t] = acc


def _bias_tiles(rel_bias):
    bkt = _bias_bucket_tiles()
    buckets_present = tuple(tuple(int(b) for b in np.unique(tile) if b >= 0) for tile in bkt)
    return pl.pallas_call(
        functools.partial(_bias_kernel, buckets_present=buckets_present),
        grid=(N_HEADS,),
        in_specs=[
            pl.BlockSpec(memory_space=pltpu.SMEM),
            pl.BlockSpec((N_TABLE_TILES, TILE, TILE), lambda h: (0, 0, 0)),
        ],
        out_specs=pl.BlockSpec((1, N_TABLE_TILES, TILE, TILE), lambda h: (h, 0, 0, 0)),
        out_shape=jax.ShapeDtypeStruct((N_HEADS, N_TABLE_TILES, TILE, TILE), F32),
        name="bias_tiles",
    )(rel_bias, jnp.asarray(bkt))


QKV_TM = 512


def _norm_mod(x, g, sc, sh):
    y = x * lax.rsqrt(jnp.mean(x * x, axis=-1, keepdims=True) + RMS_EPS)
    return (y * g) * (1.0 + sc) + sh


def _cast_weight_once(w_ref, wb_ref):
    @pl.when(pl.program_id(0) == 0)
    def _():
        wb_ref[...] = w_ref[0].astype(BF16)


def _qkv_kernel(x_ref, g_ref, sc_ref, sh_ref, w_ref, o_ref, *rest, q_scale):
    *km_ref, wb_ref = rest
    _cast_weight_once(w_ref, wb_ref)
    hb = _norm_mod(x_ref[...], g_ref[...], sc_ref[0], sh_ref[0]).astype(BF16)
    for n in range(3):
        r = jnp.dot(hb, wb_ref[:, n * D_MODEL:(n + 1) * D_MODEL], preferred_element_type=F32)
        if n == 0:
            r = r * q_scale
        if n == 1 and km_ref:
            km_ref[0][0] = jnp.mean(r.reshape(QKV_TM // MOBA_BLOCK, MOBA_BLOCK, D_MODEL), axis=1)
        o_ref[:, n * D_MODEL:(n + 1) * D_MODEL] = r.astype(BF16)


def _qkv_proj(x2, g, sc, sh, w_layers, layer, q_scale, with_kmean):
    tiles_per_batch = SEQ // QKV_TM
    n_tiles = ROWS // QKV_TM
    vec = pl.BlockSpec((1, 1, D_MODEL), lambda i: (i // tiles_per_batch, 0, 0))
    out_shape = [jax.ShapeDtypeStruct((ROWS, 3 * D_MODEL), BF16)]
    out_specs = [pl.BlockSpec((QKV_TM, 3 * D_MODEL), lambda i: (i, 0))]
    if with_kmean:
        per_tile = QKV_TM // MOBA_BLOCK
        out_shape.append(jax.ShapeDtypeStruct((n_tiles, per_tile, D_MODEL), F32))
        out_specs.append(pl.BlockSpec((1, per_tile, D_MODEL), lambda i: (i, 0, 0)))
    return pl.pallas_call(
        functools.partial(_qkv_kernel, q_scale=q_scale),
        grid=(n_tiles,),
        in_specs=[
            pl.BlockSpec((QKV_TM, D_MODEL), lambda i: (i, 0)),
            pl.BlockSpec((1, D_MODEL), lambda i: (0, 0)),
            vec, vec,
            pl.BlockSpec((1, D_MODEL, 3 * D_MODEL), lambda i: (layer, 0, 0), pipeline_mode=pl.Buffered(1)),
        ],
        out_specs=out_specs,
        out_shape=out_shape,
        scratch_shapes=[pltpu.VMEM((D_MODEL, 3 * D_MODEL), BF16)],
        compiler_params=pltpu.CompilerParams(
            dimension_semantics=("arbitrary",), vmem_limit_bytes=VMEM_LIMIT),
        name="qkv_moba" if with_kmean else "qkv_diff",
    )(x2, g.reshape(1, D_MODEL), sc, sh, w_layers)


_NT = (((1,), (1,)), ((), ()))
GROUP = 4
GROUP_KEYS = GROUP * TILE
N_GROUPS = SEQ // GROUP_KEYS
SOFTMAX_ROWS = 64


def _flash_scratch(tile_rows, n_state_tiles):
    scores = pltpu.VMEM((tile_rows, GROUP_KEYS), F32)
    probs = pltpu.VMEM((tile_rows, GROUP_KEYS), BF16)
    rescale = pltpu.VMEM((tile_rows, 1), F32)
    running_max = pltpu.VMEM((n_state_tiles * tile_rows, 1), F32)
    acc = pltpu.VMEM((n_state_tiles * tile_rows, 2 * HEAD_DIM), F32)
    return [scores, scores, probs, probs, rescale, rescale, running_max, acc]


def _flash_group(first, count, base, grp, position_tiles, qa_ref, keys_ref, va_ref, bias_ref,
                 s0_ref, s1_ref, p0_ref, p1_ref, alpha0_ref, alpha1_ref, m_ref, acc_ref):
    tile_rows = s0_ref.shape[0]

    def rows_of(tile):
        return pl.ds(pl.multiple_of((tile - base) * tile_rows, tile_rows), tile_rows)

    def scores(tile, s_ref):
        raw = lax.dot_general(qa_ref[rows_of(tile), :], keys_ref[...], _NT, preferred_element_type=F32)
        positions = position_tiles(tile)
        for j in range(GROUP):
            biases = []
            for r, pos in enumerate(positions):
                same = [b for q, b in zip(positions[:r], biases) if q is pos]
                biases.append(same[0] if same else bias_ref[0, _table_index(pos - (grp * GROUP + j))])
                s_ref[r * TILE:(r + 1) * TILE, j * TILE:(j + 1) * TILE] = (
                    raw[r * TILE:(r + 1) * TILE, j * TILE:(j + 1) * TILE] + biases[r])

    def softmax(tile, s_ref, p_ref, alpha_ref):
        start = (tile - base) * tile_rows
        for c in range(tile_rows // SOFTMAX_ROWS):
            chunk = slice(c * SOFTMAX_ROWS, (c + 1) * SOFTMAX_ROWS)
            rows = pl.ds(pl.multiple_of(start + c * SOFTMAX_ROWS, SOFTMAX_ROWS), SOFTMAX_ROWS)
            s = s_ref[chunk, :]
            m_prev = m_ref[rows, :]
            m_new = jnp.maximum(m_prev, jnp.max(s, axis=1, keepdims=True))
            alpha_ref[chunk, :] = jnp.exp(m_prev - m_new)
            m_ref[rows, :] = m_new
            p_ref[chunk, :] = jnp.exp(s - m_new).astype(BF16)

    def accumulate(tile, p_ref, alpha_ref):
        rows = rows_of(tile)
        acc_ref[rows, :] = (alpha_ref[...] * acc_ref[rows, :]
                            + jnp.dot(p_ref[...], va_ref[...], preferred_element_type=F32))

    last = first + count - 1
    scores(first, s0_ref)
    scores(first + 1, s1_ref)
    softmax(first, s0_ref, p0_ref, alpha0_ref)

    def pair(t, carry):
        tile = first + 1 + 2 * t
        scores(tile + 1, s0_ref)
        softmax(tile, s1_ref, p1_ref, alpha1_ref)
        accumulate(tile - 1, p0_ref, alpha0_ref)
        scores(tile + 2, s1_ref)
        softmax(tile + 1, s0_ref, p0_ref, alpha0_ref)
        accumulate(tile, p1_ref, alpha1_ref)
        return carry

    lax.fori_loop(0, (count - 2) // 2, pair, 0)
    softmax(last, s1_ref, p1_ref, alpha1_ref)
    accumulate(last - 1, p0_ref, alpha0_ref)
    accumulate(last, p1_ref, alpha1_ref)


def _fill_values(v_ref, va_ref):
    va_ref[:, :HEAD_DIM] = v_ref[...]
    va_ref[:, HEAD_DIM:] = jnp.ones(v_ref.shape, BF16)


GATE_QUERIES = 4 * TILE
MOBA_PAIR = 2


def _moba_gating(q_ref, km_ref, qa_ref, m_ref, acc_ref):
    blk = lax.broadcasted_iota(jnp.int32, (N_KV_BLOCKS, GATE_QUERIES), 0)
    blkf = blk.astype(F32)
    tile_in_step = lax.broadcasted_iota(jnp.int32, (N_KV_BLOCKS, GATE_QUERIES), 1) // TILE

    def gate_step(step, carry):
        rows = pl.ds(pl.multiple_of(step * GATE_QUERIES, GATE_QUERIES), GATE_QUERIES)
        m_ref[rows, :] = jnp.full((GATE_QUERIES, 1), NEG_INF, F32)
        acc_ref[rows, :] = jnp.zeros((GATE_QUERIES, 2 * HEAD_DIM), F32)
        q = q_ref[rows, :]
        gate = lax.dot_general(km_ref[...].astype(BF16), q, _NT, preferred_element_type=F32)
        tile = step * (GATE_QUERIES // TILE) + tile_in_step
        past = blk < tile
        g = jnp.where(past, gate, NEG_INF)
        madd = jnp.full(gate.shape, NEG_INF, F32)
        for _ in range(MOBA_TOPK):
            mx = jnp.max(g, axis=0, keepdims=True)
            first = jnp.min(jnp.where(g == mx, blkf, float(N_KV_BLOCKS)), axis=0, keepdims=True)
            hit = blkf == first
            madd = jnp.where(hit & past, 0.0, madd)
            g = jnp.where(hit, -jnp.inf, g)
        madd = jnp.where(blk == tile, 0.0, madd)
        madd = jnp.concatenate([madd, jnp.zeros((HEAD_DIM - N_KV_BLOCKS, GATE_QUERIES), F32)], axis=0)
        qa_ref[rows, :HEAD_DIM] = q
        qa_ref[rows, HEAD_DIM:] = madd.T.astype(BF16)
        return carry

    lax.fori_loop(0, SEQ // GATE_QUERIES, gate_step, 0)


def _moba_kernel(q_ref, k_ref, v_ref, km_ref, bias_ref, o_ref, qa_ref, ka_ref, va_ref, *flash_refs):
    grp = pl.program_id(2)
    m_ref, acc_ref = flash_refs[-2:]

    @pl.when(grp == 0)
    def _():
        _moba_gating(q_ref, km_ref, qa_ref, m_ref, acc_ref)

    lane = lax.broadcasted_iota(jnp.int32, (GROUP_KEYS, HEAD_DIM), 1)
    key_block = grp * GROUP + lax.broadcasted_iota(jnp.int32, (GROUP_KEYS, HEAD_DIM), 0) // TILE
    ka_ref[:, :HEAD_DIM] = k_ref[...]
    ka_ref[:, HEAD_DIM:] = jnp.where(lane == key_block, 1.0, 0.0).astype(BF16)
    _fill_values(v_ref, va_ref)

    first = grp * GROUP // MOBA_PAIR
    _flash_group(first, N_TILES // MOBA_PAIR - first, 0, grp,
                 lambda tile: [tile * MOBA_PAIR + r for r in range(MOBA_PAIR)],
                 qa_ref, ka_ref, va_ref, bias_ref, *flash_refs)

    @pl.when(grp == N_GROUPS - 1)
    def _():
        def finish(tile, carry):
            rows = pl.ds(pl.multiple_of(tile * TILE, TILE), TILE)
            acc = acc_ref[rows, :]
            o_ref[rows, :] = (acc[:, :HEAD_DIM] / acc[:, HEAD_DIM:]).astype(BF16)
            return carry

        lax.fori_loop(0, N_TILES, finish, 0)


def _moba_attention(qkv, kmean, bias):
    return pl.pallas_call(
        _moba_kernel,
        grid=(BATCH, N_HEADS, N_GROUPS),
        in_specs=[
            pl.BlockSpec((SEQ, HEAD_DIM), lambda b, h, g: (b, h)),
            pl.BlockSpec((GROUP_KEYS, HEAD_DIM), lambda b, h, g: (b * N_GROUPS + g, N_HEADS + h)),
            pl.BlockSpec((GROUP_KEYS, HEAD_DIM), lambda b, h, g: (b * N_GROUPS + g, 2 * N_HEADS + h)),
            pl.BlockSpec((N_KV_BLOCKS, HEAD_DIM), lambda b, h, g: (b, h)),
            pl.BlockSpec((1, N_TABLE_TILES, TILE, TILE), lambda b, h, g: (h, 0, 0, 0)),
        ],
        out_specs=pl.BlockSpec((SEQ, HEAD_DIM), lambda b, h, g: (b, h)),
        out_shape=jax.ShapeDtypeStruct((ROWS, D_MODEL), BF16),
        scratch_shapes=[
            pltpu.VMEM((SEQ, 2 * HEAD_DIM), BF16),
            pltpu.VMEM((GROUP_KEYS, 2 * HEAD_DIM), BF16),
            pltpu.VMEM((GROUP_KEYS, 2 * HEAD_DIM), BF16),
        ] + _flash_scratch(MOBA_PAIR * TILE, N_TILES // MOBA_PAIR),
        compiler_params=pltpu.CompilerParams(
            dimension_semantics=("arbitrary", "arbitrary", "arbitrary"), vmem_limit_bytes=VMEM_LIMIT),
        name="moba_attn",
    )(qkv, qkv, qkv, kmean, bias)


DIFF_Q_SPLIT = 2
DIFF_TILES = N_TILES // DIFF_Q_SPLIT
DIFF_ROWS = SEQ // DIFF_Q_SPLIT


def _diff_last_group(q_part):
    return ((q_part + 1) * DIFF_TILES - 1) // GROUP


def _diff_kernel(q_ref, k_ref, v_ref, bias_ref, lam_ref, g_ref, o_ref, qa_ref, va_ref, *flash_refs,
                 lambda_init):
    q_part = pl.program_id(2)
    grp = pl.program_id(3)
    m_ref, acc_ref = flash_refs[-2:]
    base = q_part * DIFF_TILES
    last_grp = _diff_last_group(q_part)

    @pl.when(grp == 0)
    def _():
        lane = lax.broadcasted_iota(jnp.int32, (TILE, HEAD_DIM), 1)
        zero = jnp.zeros((TILE, HEAD_DIM), BF16)

        def split(tile, carry):
            q = q_ref[pl.ds(pl.multiple_of(tile * TILE, TILE), TILE), :]
            qa_ref[pl.ds(pl.multiple_of(2 * tile * TILE, TILE), TILE), :] = jnp.where(lane < DIFF_HALF, q, zero)
            qa_ref[pl.ds(pl.multiple_of((2 * tile + 1) * TILE, TILE), TILE), :] = (
                jnp.where(lane >= DIFF_HALF, q, zero))
            state_rows = pl.ds(pl.multiple_of(2 * tile * TILE, 2 * TILE), 2 * TILE)
            m_ref[state_rows, :] = jnp.full((2 * TILE, 1), NEG_INF, F32)
            acc_ref[state_rows, :] = jnp.zeros((2 * TILE, 2 * HEAD_DIM), F32)
            return carry

        lax.fori_loop(0, DIFF_TILES, split, 0)

    @pl.when(grp <= last_grp)
    def _():
        _fill_values(v_ref, va_ref)
        first = jnp.maximum(grp * GROUP, base)
        _flash_group(first, base + DIFF_TILES - first, base, grp, lambda tile: [tile, tile],
                     qa_ref, k_ref, va_ref, bias_ref, *flash_refs)

    @pl.when(grp == last_grp)
    def _():
        lam = lam_ref[...]
        lam_full = (jnp.exp(jnp.sum(lam[0:1] * lam[1:2], axis=1, keepdims=True))
                    - jnp.exp(jnp.sum(lam[2:3] * lam[3:4], axis=1, keepdims=True)) + lambda_init)

        def finish(tile, carry):
            acc = acc_ref[pl.ds(pl.multiple_of(2 * tile * TILE, 2 * TILE), 2 * TILE), :]
            att = acc[:, :HEAD_DIM] / acc[:, HEAD_DIM:]
            o = att[:TILE] - lam_full * att[TILE:]
            o = o * lax.rsqrt(jnp.mean(o * o, axis=-1, keepdims=True) + RMS_EPS) * g_ref[...]
            o_ref[pl.ds(pl.multiple_of(tile * TILE, TILE), TILE), :] = (o * (1.0 - lambda_init)).astype(BF16)
            return carry

        lax.fori_loop(0, DIFF_TILES, finish, 0)


def _diff_attention(qkv, bias, lam, subln_g, layer_idx):
    lambda_init = 0.8 - 0.6 * math.exp(-0.3 * layer_idx)

    def kv_rows(b, p, g):
        return b * N_GROUPS + jnp.minimum(g, _diff_last_group(p))

    return pl.pallas_call(
        functools.partial(_diff_kernel, lambda_init=lambda_init),
        grid=(BATCH, N_HEADS, DIFF_Q_SPLIT, N_GROUPS),
        in_specs=[
            pl.BlockSpec((DIFF_ROWS, HEAD_DIM), lambda b, h, p, g: (b * DIFF_Q_SPLIT + p, h)),
            pl.BlockSpec((GROUP_KEYS, HEAD_DIM), lambda b, h, p, g: (kv_rows(b, p, g), N_HEADS + h)),
            pl.BlockSpec((GROUP_KEYS, HEAD_DIM), lambda b, h, p, g: (kv_rows(b, p, g), 2 * N_HEADS + h)),
            pl.BlockSpec((1, N_TABLE_TILES, TILE, TILE), lambda b, h, p, g: (h, 0, 0, 0)),
            pl.BlockSpec((4, DIFF_HALF), lambda b, h, p, g: (0, 0)),
            pl.BlockSpec((1, HEAD_DIM), lambda b, h, p, g: (0, 0)),
        ],
        out_specs=pl.BlockSpec((DIFF_ROWS, HEAD_DIM), lambda b, h, p, g: (b * DIFF_Q_SPLIT + p, h)),
        out_shape=jax.ShapeDtypeStruct((ROWS, D_MODEL), BF16),
        scratch_shapes=[
            pltpu.VMEM((2 * DIFF_ROWS, HEAD_DIM), BF16),
            pltpu.VMEM((GROUP_KEYS, 2 * HEAD_DIM), BF16),
        ] + _flash_scratch(2 * TILE, DIFF_TILES),
        compiler_params=pltpu.CompilerParams(
            dimension_semantics=("arbitrary", "arbitrary", "arbitrary", "arbitrary"),
            vmem_limit_bytes=VMEM_LIMIT),
        name="diff_attn",
    )(qkv, qkv, qkv, bias, lam, subln_g.reshape(1, HEAD_DIM))


OUT_TM = 512


def _rms(y, g):
    return y * lax.rsqrt(jnp.mean(y * y, axis=-1, keepdims=True) + RMS_EPS) * g


def _outproj_kernel(o_ref, x_ref, w_ref, g_ref, gate_ref, out_ref, wb_ref):
    _cast_weight_once(w_ref, wb_ref)
    y = jnp.dot(o_ref[...], wb_ref[...], preferred_element_type=F32)
    out_ref[...] = x_ref[...] + gate_ref[0] * _rms(y, g_ref[...])


def _out_proj(o, x2, w_layers, layer, g, gate):
    tiles_per_batch = SEQ // OUT_TM
    return pl.pallas_call(
        _outproj_kernel,
        grid=(ROWS // OUT_TM,),
        in_specs=[
            pl.BlockSpec((OUT_TM, D_MODEL), lambda i: (i, 0)),
            pl.BlockSpec((OUT_TM, D_MODEL), lambda i: (i, 0)),
            pl.BlockSpec((1, D_MODEL, D_MODEL), lambda i: (layer, 0, 0), pipeline_mode=pl.Buffered(1)),
            pl.BlockSpec((1, D_MODEL), lambda i: (0, 0)),
            pl.BlockSpec((1, 1, D_MODEL), lambda i: (i // tiles_per_batch, 0, 0)),
        ],
        out_specs=pl.BlockSpec((OUT_TM, D_MODEL), lambda i: (i, 0)),
        out_shape=jax.ShapeDtypeStruct((ROWS, D_MODEL), F32),
        scratch_shapes=[pltpu.VMEM((D_MODEL, D_MODEL), BF16)],
        compiler_params=pltpu.CompilerParams(
            dimension_semantics=("arbitrary",), vmem_limit_bytes=VMEM_LIMIT),
        name="out_proj",
    )(o, x2, w_layers, g.reshape(1, D_MODEL), gate)


FFN_TM = 1024
FFN_TF = 256


def _ffn_kernel(x_ref, g2_ref, sc_ref, sh_ref, wg_ref, wu_ref, wo_ref, g3_ref, gate_ref, out_ref,
                h_ref, acc_ref):
    c = pl.program_id(1)

    @pl.when(c == 0)
    def _():
        h_ref[...] = _norm_mod(x_ref[...], g2_ref[...], sc_ref[0], sh_ref[0]).astype(BF16)
        acc_ref[...] = jnp.zeros(acc_ref.shape, F32)

    h = h_ref[...]
    gp = jnp.dot(h, wg_ref[0].astype(BF16), preferred_element_type=F32)
    up = jnp.dot(h, wu_ref[0].astype(BF16), preferred_element_type=F32)
    act = (gp / (1.0 + jnp.exp(-gp)) * up).astype(BF16)
    acc_ref[...] += jnp.dot(act, wo_ref[0].astype(BF16), preferred_element_type=F32)

    @pl.when(c == pl.num_programs(1) - 1)
    def _():
        out_ref[...] = x_ref[...] + gate_ref[0] * _rms(acc_ref[...], g3_ref[...])


def _ffn(x2, g2, sc, sh, w_in_layers, w_out_layers, layer, g3, gate):
    tiles_per_batch = SEQ // FFN_TM
    n_chunks = D_FF // FFN_TF
    vec = pl.BlockSpec((1, 1, D_MODEL), lambda i, c: (i // tiles_per_batch, 0, 0))
    gvec = pl.BlockSpec((1, D_MODEL), lambda i, c: (0, 0))
    return pl.pallas_call(
        _ffn_kernel,
        grid=(ROWS // FFN_TM, n_chunks),
        in_specs=[
            pl.BlockSpec((FFN_TM, D_MODEL), lambda i, c: (i, 0)),
            gvec, vec, vec,
            pl.BlockSpec((1, D_MODEL, FFN_TF), lambda i, c: (layer, 0, c)),
            pl.BlockSpec((1, D_MODEL, FFN_TF), lambda i, c: (layer, 0, n_chunks + c)),
            pl.BlockSpec((1, FFN_TF, D_MODEL), lambda i, c: (layer, c, 0)),
            gvec, vec,
        ],
        out_specs=pl.BlockSpec((FFN_TM, D_MODEL), lambda i, c: (i, 0)),
        out_shape=jax.ShapeDtypeStruct((ROWS, D_MODEL), F32),
        scratch_shapes=[
            pltpu.VMEM((FFN_TM, D_MODEL), BF16),
            pltpu.VMEM((FFN_TM, D_MODEL), F32),
        ],
        compiler_params=pltpu.CompilerParams(
            dimension_semantics=("parallel", "arbitrary"), vmem_limit_bytes=VMEM_LIMIT),
        name="ffn",
    )(x2, g2.reshape(1, D_MODEL), sc, sh, w_in_layers, w_in_layers, w_out_layers, g3.reshape(1, D_MODEL), gate)


def kernel(x, c, rel_bias, ada_w, ada_b, norm_g, moba_w_qkv, moba_w_o, diff_w_qkv, diff_w_o, diff_lambda,
           diff_subln_g, ffn_w_in, ffn_w_out):
    x2 = x.reshape(ROWS, D_MODEL)
    mod = _adaln_mod(c, ada_w, ada_b)
    bias = _bias_tiles(rel_bias)
    for i in range(DEPTH):
        sh_a, sc_a, g_a, sh_f, sc_f, g_f = [
            mod[i, :, j * D_MODEL:(j + 1) * D_MODEL].reshape(BATCH, 1, D_MODEL) for j in range(6)]
        if i % 2 == 0:
            qkv, kmean = _qkv_proj(x2, norm_g[i, 0], sc_a, sh_a, moba_w_qkv, i // 2, HEAD_DIM ** -0.5, True)
            o = _moba_attention(qkv, kmean.reshape(BATCH * N_KV_BLOCKS, D_MODEL), bias)
            w_o = moba_w_o
        else:
            (qkv,) = _qkv_proj(x2, norm_g[i, 0], sc_a, sh_a, diff_w_qkv, i // 2, DIFF_HALF ** -0.5, False)
            o = _diff_attention(qkv, bias, diff_lambda[i // 2], diff_subln_g[i // 2], i)
            w_o = diff_w_o
        x2 = _out_proj(o, x2, w_o, i // 2, norm_g[i, 1], g_a)
        x2 = _ffn(x2, norm_g[i, 2], sc_f, sh_f, ffn_w_in, ffn_w_out, i, norm_g[i, 3], g_f)
    return x2.reshape(BATCH, SEQ, D_MODEL)
```

```python
import functools
import math

import numpy as np
import jax
import jax.numpy as jnp
from jax import lax
from jax.experimental import pallas as pl
from jax.experimental.pallas import tpu as pltpu

D_MODEL = 1024
BATCH = 2
SEQ = 8192
DEPTH = 2
N_HEADS = 8
HEAD_DIM = D_MODEL // N_HEADS
DIFF_HALF = HEAD_DIM // 2
MOBA_BLOCK = 256
MOBA_TOPK = 3
D_FF = 2816
N_BUCKETS = 32
MAX_EXACT = N_BUCKETS // 2
MAX_DISTANCE = 2048
RMS_EPS = 1e-6
NEG_INF = -1e30

ROWS = BATCH * SEQ
N_KV_BLOCKS = SEQ // MOBA_BLOCK
TILE = MOBA_BLOCK
VMEM_LIMIT = 48 * 1024 * 1024

F32 = jnp.float32
BF16 = jnp.bfloat16


def _bucket_of_distance():
    n = np.arange(MAX_DISTANCE + 1)
    nf = np.maximum(n, 1).astype(np.float64)
    val = np.log(nf / MAX_EXACT) / math.log(MAX_DISTANCE / MAX_EXACT) * (N_BUCKETS - MAX_EXACT)
    frac = np.abs(val - np.round(val))
    assert np.all((frac > 5e-5) | (n <= MAX_EXACT) | (n == MAX_DISTANCE))
    large = np.minimum(MAX_EXACT + np.floor(val + 1e-9).astype(np.int64), N_BUCKETS - 1)
    return np.where(n < MAX_EXACT, n, large)


_BUCKETS = _bucket_of_distance()
FAR_DISTANCE = int(np.min(np.nonzero(_BUCKETS == N_BUCKETS - 1)[0]))
N_BIAS_TILES = (FAR_DISTANCE + TILE - 1) // TILE + 1
N_TABLE_TILES = N_BIAS_TILES + 2


def _bias_bucket_tiles():
    i = np.arange(TILE)[:, None]
    j = np.arange(TILE)[None, :]
    out = []
    for t in range(-1, N_BIAS_TILES + 1):
        rel = t * TILE + i - j
        out.append(np.where(rel >= 0, _BUCKETS[np.clip(rel, 0, MAX_DISTANCE)], -1))
    return np.stack(out).astype(np.int32)


def _table_index(tiles_behind):
    return jnp.clip(tiles_behind, -1, N_BIAS_TILES) + 1


MOD_TN = 512


def _mod_kernel(ct_ref, w_ref, b_ref, o_ref):
    ct = ct_ref[...]
    cact = ct / (1.0 + jnp.exp(-ct))
    w = w_ref[0]
    for b in range(BATCH):
        row = jnp.sum(w * cact[:, b:b + 1], axis=0, keepdims=True)
        o_ref[0, b:b + 1, :] = row + b_ref[0]


def _adaln_mod(c, ada_w, ada_b):
    n_out = ada_w.shape[-1]
    return pl.pallas_call(
        _mod_kernel,
        grid=(DEPTH, n_out // MOD_TN),
        in_specs=[
            pl.BlockSpec((D_MODEL, BATCH), lambda i, n: (0, 0)),
            pl.BlockSpec((1, D_MODEL, MOD_TN), lambda i, n: (i, 0, n)),
            pl.BlockSpec((1, 1, MOD_TN), lambda i, n: (i, 0, n)),
        ],
        out_specs=pl.BlockSpec((1, BATCH, MOD_TN), lambda i, n: (i, 0, n)),
        out_shape=jax.ShapeDtypeStruct((DEPTH, BATCH, n_out), F32),
        name="adaln_mod",
    )(c.T, ada_w, ada_b.reshape(DEPTH, 1, n_out))


def _bias_kernel(tab_ref, bkt_ref, o_ref, *, buckets_present):
    h = pl.program_id(0)
    far = tab_ref[N_BUCKETS - 1, h]
    for t, present in enumerate(buckets_present):
        bkt = bkt_ref[t]
        acc = jnp.full(bkt.shape, NEG_INF, F32)
        for b in present:
            acc = jnp.where(bkt == b, tab_ref[b, h] - far, acc)
        o_ref[0, t] = acc


def _bias_tiles(rel_bias):
    bkt = _bias_bucket_tiles()
    buckets_present = tuple(tuple(int(b) for b in np.unique(tile) if b >= 0) for tile in bkt)
    return pl.pallas_call(
        functools.partial(_bias_kernel, buckets_present=buckets_present),
        grid=(N_HEADS,),
        in_specs=[
            pl.BlockSpec(memory_space=pltpu.SMEM),
            pl.BlockSpec((N_TABLE_TILES, TILE, TILE), lambda h: (0, 0, 0)),
        ],
        out_specs=pl.BlockSpec((1, N_TABLE_TILES, TILE, TILE), lambda h: (h, 0, 0, 0)),
        out_shape=jax.ShapeDtypeStruct((N_HEADS, N_TABLE_TILES, TILE, TILE), F32),
        name="bias_tiles",
    )(rel_bias, jnp.asarray(bkt))


QKV_TM = 512


def _norm_mod(x, g, sc, sh):
    y = x * lax.rsqrt(jnp.mean(x * x, axis=-1, keepdims=True) + RMS_EPS)
    return (y * g) * (1.0 + sc) + sh


def _cast_weight_once(w_ref, wb_ref):
    @pl.when(pl.program_id(0) == 0)
    def _():
        wb_ref[...] = w_ref[0].astype(BF16)


def _qkv_kernel(x_ref, g_ref, sc_ref, sh_ref, w_ref, o_ref, *rest, q_scale):
    *km_ref, wb_ref = rest
    _cast_weight_once(w_ref, wb_ref)
    hb = _norm_mod(x_ref[...], g_ref[...], sc_ref[0], sh_ref[0]).astype(BF16)
    for n in range(3):
        r = jnp.dot(hb, wb_ref[:, n * D_MODEL:(n + 1) * D_MODEL], preferred_element_type=F32)
        if n == 0:
            r = r * q_scale
        if n == 1 and km_ref:
            km_ref[0][0] = jnp.mean(r.reshape(QKV_TM // MOBA_BLOCK, MOBA_BLOCK, D_MODEL), axis=1)
        o_ref[:, n * D_MODEL:(n + 1) * D_MODEL] = r.astype(BF16)


def _qkv_proj(x2, g, sc, sh, w_layers, layer, q_scale, with_kmean):
    tiles_per_batch = SEQ // QKV_TM
    n_tiles = ROWS // QKV_TM
    vec = pl.BlockSpec((1, 1, D_MODEL), lambda i: (i // tiles_per_batch, 0, 0))
    out_shape = [jax.ShapeDtypeStruct((ROWS, 3 * D_MODEL), BF16)]
    out_specs = [pl.BlockSpec((QKV_TM, 3 * D_MODEL), lambda i: (i, 0))]
    if with_kmean:
        per_tile = QKV_TM // MOBA_BLOCK
        out_shape.append(jax.ShapeDtypeStruct((n_tiles, per_tile, D_MODEL), F32))
        out_specs.append(pl.BlockSpec((1, per_tile, D_MODEL), lambda i: (i, 0, 0)))
    return pl.pallas_call(
        functools.partial(_qkv_kernel, q_scale=q_scale),
        grid=(n_tiles,),
        in_specs=[
            pl.BlockSpec((QKV_TM, D_MODEL), lambda i: (i, 0)),
            pl.BlockSpec((1, D_MODEL), lambda i: (0, 0)),
            vec, vec,
            pl.BlockSpec((1, D_MODEL, 3 * D_MODEL), lambda i: (layer, 0, 0), pipeline_mode=pl.Buffered(1)),
        ],
        out_specs=out_specs,
        out_shape=out_shape,
        scratch_shapes=[pltpu.VMEM((D_MODEL, 3 * D_MODEL), BF16)],
        compiler_params=pltpu.CompilerParams(
            dimension_semantics=("arbitrary",), vmem_limit_bytes=VMEM_LIMIT),
        name="qkv_moba" if with_kmean else "qkv_diff",
    )(x2, g.reshape(1, D_MODEL), sc, sh, w_layers)


_NT = (((1,), (1,)), ((), ()))
GROUP = 4
GROUP_KEYS = GROUP * TILE
N_GROUPS = SEQ // GROUP_KEYS
ROW_TILE = 2 * TILE
N_ROW_TILES = SEQ // ROW_TILE
ROW_TILES_PER_GROUP = GROUP_KEYS // ROW_TILE
SOFTMAX_ROWS = 64


def _flash_scratch():
    scores = pltpu.VMEM((ROW_TILE, GROUP_KEYS), F32)
    probs = pltpu.VMEM((ROW_TILE, GROUP_KEYS), BF16)
    rescale = pltpu.VMEM((ROW_TILE, 1), F32)
    values = pltpu.VMEM((2, GROUP_KEYS, 2 * HEAD_DIM), BF16)
    running_max = pltpu.VMEM((SEQ, 1), F32)
    acc = pltpu.VMEM((SEQ, 2 * HEAD_DIM), F32)
    return [scores, scores, probs, probs, rescale, rescale, values, running_max, acc]


def _tile_rows(tile):
    return pl.ds(pl.multiple_of(tile * ROW_TILE, ROW_TILE), ROW_TILE)


def _flash_sweep(grp, qa_ref, keys_ref, v_ref, bias_ref,
                 s0_ref, s1_ref, p0_ref, p1_ref, alpha0_ref, alpha1_ref, va_ref, m_ref, acc_ref):
    half = GROUP_KEYS // 2
    first = grp * ROW_TILES_PER_GROUP
    last = N_ROW_TILES - 1
    va_new = va_ref.at[grp % 2]
    va_old = va_ref.at[1 - grp % 2]
    va_new[:, :HEAD_DIM] = v_ref[...]
    va_new[:, HEAD_DIM:] = jnp.ones(v_ref.shape, BF16)

    def scores(tile, s_ref, n_keys=GROUP_KEYS):
        raw = lax.dot_general(qa_ref[_tile_rows(tile), :], keys_ref[:n_keys, :], _NT,
                              preferred_element_type=F32)
        for r in range(ROW_TILE // TILE):
            for j in range(n_keys // TILE):
                bias = bias_ref[0, _table_index(tile * (ROW_TILE // TILE) + r - (grp * GROUP + j))]
                s_ref[r * TILE:(r + 1) * TILE, j * TILE:(j + 1) * TILE] = (
                    raw[r * TILE:(r + 1) * TILE, j * TILE:(j + 1) * TILE] + bias)

    def softmax(tile, s_ref, p_ref, alpha_ref, n_keys=GROUP_KEYS):
        for c in range(ROW_TILE // SOFTMAX_ROWS):
            chunk = slice(c * SOFTMAX_ROWS, (c + 1) * SOFTMAX_ROWS)
            rows = pl.ds(pl.multiple_of(tile * ROW_TILE + c * SOFTMAX_ROWS, SOFTMAX_ROWS), SOFTMAX_ROWS)
            s = s_ref[chunk, :n_keys]
            m_prev = m_ref[rows, :]
            m_new = jnp.maximum(m_prev, jnp.max(s, axis=1, keepdims=True))
            alpha_ref[chunk, :] = jnp.exp(m_prev - m_new)
            m_ref[rows, :] = m_new
            p_ref[chunk, :n_keys] = jnp.exp(s - m_new).astype(BF16)

    def accumulate(tile, p_ref, alpha_ref, va, n_keys=GROUP_KEYS):
        rows = _tile_rows(tile)
        acc_ref[rows, :] = (alpha_ref[...] * acc_ref[rows, :]
                            + jnp.dot(p_ref[:, :n_keys], va[:n_keys, :], preferred_element_type=F32))

    @pl.when(grp == 0)
    def _():
        scores(first, s0_ref, half)
        scores(first + 1, s1_ref)
        softmax(first, s0_ref, p0_ref, alpha0_ref, half)

    @pl.when(grp > 0)
    def _():
        scores(first, s0_ref, half)
        softmax(last, s1_ref, p1_ref, alpha1_ref)
        accumulate(last - 1, p0_ref, alpha0_ref, va_old)
        scores(first + 1, s1_ref)
        softmax(first, s0_ref, p0_ref, alpha0_ref, half)
        accumulate(last, p1_ref, alpha1_ref, va_old)

    @pl.when(grp < N_GROUPS - 1)
    def _():
        scores(first + 2, s0_ref)
        softmax(first + 1, s1_ref, p1_ref, alpha1_ref)
        accumulate(first, p0_ref, alpha0_ref, va_new, half)
        scores(first + 3, s1_ref)
        softmax(first + 2, s0_ref, p0_ref, alpha0_ref)
        accumulate(first + 1, p1_ref, alpha1_ref, va_new)

        def pair(t, carry):
            tile = first + 3 + 2 * t
            scores(tile + 1, s0_ref)
            softmax(tile, s1_ref, p1_ref, alpha1_ref)
            accumulate(tile - 1, p0_ref, alpha0_ref, va_new)
            scores(tile + 2, s1_ref)
            softmax(tile + 1, s0_ref, p0_ref, alpha0_ref)
            accumulate(tile, p1_ref, alpha1_ref, va_new)
            return carry

        lax.fori_loop(0, (last - first - 3) // 2, pair, 0)

    @pl.when(grp == N_GROUPS - 1)
    def _():
        softmax(last, s1_ref, p1_ref, alpha1_ref)
        accumulate(last - 1, p0_ref, alpha0_ref, va_new, half)
        accumulate(last, p1_ref, alpha1_ref, va_new)


def _reset_state(rows, m_ref, acc_ref):
    m_ref[rows, :] = jnp.full((rows.size, 1), NEG_INF, F32)
    acc_ref[rows, :] = jnp.zeros((rows.size, 2 * HEAD_DIM), F32)


def _attention_of(acc):
    return acc[:, :HEAD_DIM] / acc[:, HEAD_DIM:]


GATE_QUERIES = 4 * TILE


def _moba_gating(q_ref, km_ref, qa_ref, m_ref, acc_ref):
    blk = lax.broadcasted_iota(jnp.int32, (N_KV_BLOCKS, GATE_QUERIES), 0)
    blkf = blk.astype(F32)
    tile_in_step = lax.broadcasted_iota(jnp.int32, (N_KV_BLOCKS, GATE_QUERIES), 1) // TILE

    def gate_step(step, carry):
        rows = pl.ds(pl.multiple_of(step * GATE_QUERIES, GATE_QUERIES), GATE_QUERIES)
        _reset_state(rows, m_ref, acc_ref)
        q = q_ref[rows, :]
        gate = lax.dot_general(km_ref[...].astype(BF16), q, _NT, preferred_element_type=F32)
        tile = step * (GATE_QUERIES // TILE) + tile_in_step
        past = blk < tile
        g = jnp.where(past, gate, NEG_INF)
        madd = jnp.full(gate.shape, NEG_INF, F32)
        for _ in range(MOBA_TOPK):
            mx = jnp.max(g, axis=0, keepdims=True)
            first = jnp.min(jnp.where(g == mx, blkf, float(N_KV_BLOCKS)), axis=0, keepdims=True)
            hit = blkf == first
            madd = jnp.where(hit & past, 0.0, madd)
            g = jnp.where(hit, -jnp.inf, g)
        madd = jnp.where(blk == tile, 0.0, madd)
        madd = jnp.concatenate([madd, jnp.zeros((HEAD_DIM - N_KV_BLOCKS, GATE_QUERIES), F32)], axis=0)
        qa_ref[rows, :HEAD_DIM] = q
        qa_ref[rows, HEAD_DIM:] = madd.T.astype(BF16)
        return carry

    lax.fori_loop(0, SEQ // GATE_QUERIES, gate_step, 0)


def _moba_kernel(q_ref, k_ref, v_ref, km_ref, bias_ref, o_ref, qa_ref, ka_ref, *flash_refs):
    grp = pl.program_id(2)
    m_ref, acc_ref = flash_refs[-2:]

    @pl.when(grp == 0)
    def _():
        _moba_gating(q_ref, km_ref, qa_ref, m_ref, acc_ref)

    lane = lax.broadcasted_iota(jnp.int32, (GROUP_KEYS, HEAD_DIM), 1)
    key_block = grp * GROUP + lax.broadcasted_iota(jnp.int32, (GROUP_KEYS, HEAD_DIM), 0) // TILE
    ka_ref[:, :HEAD_DIM] = k_ref[...]
    ka_ref[:, HEAD_DIM:] = jnp.where(lane == key_block, 1.0, 0.0).astype(BF16)

    _flash_sweep(grp, qa_ref, ka_ref, v_ref, bias_ref, *flash_refs)

    @pl.when(grp == N_GROUPS - 1)
    def _():
        def finish(tile, carry):
            rows = _tile_rows(tile)
            o_ref[rows, :] = _attention_of(acc_ref[rows, :]).astype(BF16)
            return carry

        lax.fori_loop(0, N_ROW_TILES, finish, 0)


def _moba_attention(qkv, kmean, bias):
    return pl.pallas_call(
        _moba_kernel,
        grid=(BATCH, N_HEADS, N_GROUPS),
        in_specs=[
            pl.BlockSpec((SEQ, HEAD_DIM), lambda b, h, g: (b, h)),
            pl.BlockSpec((GROUP_KEYS, HEAD_DIM), lambda b, h, g: (b * N_GROUPS + g, N_HEADS + h)),
            pl.BlockSpec((GROUP_KEYS, HEAD_DIM), lambda b, h, g: (b * N_GROUPS + g, 2 * N_HEADS + h)),
            pl.BlockSpec((N_KV_BLOCKS, HEAD_DIM), lambda b, h, g: (b, h)),
            pl.BlockSpec((1, N_TABLE_TILES, TILE, TILE), lambda b, h, g: (h, 0, 0, 0)),
        ],
        out_specs=pl.BlockSpec((SEQ, HEAD_DIM), lambda b, h, g: (b, h)),
        out_shape=jax.ShapeDtypeStruct((ROWS, D_MODEL), BF16),
        scratch_shapes=[
            pltpu.VMEM((SEQ, 2 * HEAD_DIM), BF16),
            pltpu.VMEM((GROUP_KEYS, 2 * HEAD_DIM), BF16),
        ] + _flash_scratch(),
        compiler_params=pltpu.CompilerParams(
            dimension_semantics=("arbitrary", "arbitrary", "arbitrary"), vmem_limit_bytes=VMEM_LIMIT),
        name="moba_attn",
    )(qkv, qkv, qkv, kmean, bias)


def _diff_kernel(q_ref, k_ref, v_ref, bias_ref, lam_ref, g_ref, o_ref, qa_ref, map0_ref, *flash_refs,
                 lambda_init):
    which = pl.program_id(2)
    grp = pl.program_id(3)
    m_ref, acc_ref = flash_refs[-2:]

    @pl.when(grp == 0)
    def _():
        lane_map = lax.broadcasted_iota(jnp.int32, (ROW_TILE, HEAD_DIM), 1) // DIFF_HALF

        def split(tile, carry):
            rows = _tile_rows(tile)
            _reset_state(rows, m_ref, acc_ref)
            q = q_ref[rows, :]
            qa_ref[rows, :] = jnp.where(lane_map == which, q, jnp.zeros_like(q))
            return carry

        lax.fori_loop(0, N_ROW_TILES, split, 0)

    _flash_sweep(grp, qa_ref, k_ref, v_ref, bias_ref, *flash_refs)

    @pl.when((grp == N_GROUPS - 1) & (which == 0))
    def _():
        def keep(tile, carry):
            rows = _tile_rows(tile)
            map0_ref[rows, :] = _attention_of(acc_ref[rows, :])
            return carry

        lax.fori_loop(0, N_ROW_TILES, keep, 0)

    @pl.when((grp == N_GROUPS - 1) & (which == 1))
    def _():
        lam = lam_ref[...]
        lam_full = (jnp.exp(jnp.sum(lam[0:1] * lam[1:2], axis=1, keepdims=True))
                    - jnp.exp(jnp.sum(lam[2:3] * lam[3:4], axis=1, keepdims=True)) + lambda_init)

        def finish(tile, carry):
            rows = _tile_rows(tile)
            o = map0_ref[rows, :] - lam_full * _attention_of(acc_ref[rows, :])
            o = o * lax.rsqrt(jnp.mean(o * o, axis=-1, keepdims=True) + RMS_EPS) * g_ref[...]
            o_ref[rows, :] = (o * (1.0 - lambda_init)).astype(BF16)
            return carry

        lax.fori_loop(0, N_ROW_TILES, finish, 0)


def _diff_attention(qkv, bias, lam, subln_g, layer_idx):
    lambda_init = 0.8 - 0.6 * math.exp(-0.3 * layer_idx)
    return pl.pallas_call(
        functools.partial(_diff_kernel, lambda_init=lambda_init),
        grid=(BATCH, N_HEADS, 2, N_GROUPS),
        in_specs=[
            pl.BlockSpec((SEQ, HEAD_DIM), lambda b, h, c, g: (b, h)),
            pl.BlockSpec((GROUP_KEYS, HEAD_DIM), lambda b, h, c, g: (b * N_GROUPS + g, N_HEADS + h)),
            pl.BlockSpec((GROUP_KEYS, HEAD_DIM), lambda b, h, c, g: (b * N_GROUPS + g, 2 * N_HEADS + h)),
            pl.BlockSpec((1, N_TABLE_TILES, TILE, TILE), lambda b, h, c, g: (h, 0, 0, 0)),
            pl.BlockSpec((4, DIFF_HALF), lambda b, h, c, g: (0, 0)),
            pl.BlockSpec((1, HEAD_DIM), lambda b, h, c, g: (0, 0)),
        ],
        out_specs=pl.BlockSpec((SEQ, HEAD_DIM), lambda b, h, c, g: (b, h)),
        out_shape=jax.ShapeDtypeStruct((ROWS, D_MODEL), BF16),
        scratch_shapes=[
            pltpu.VMEM((SEQ, HEAD_DIM), BF16),
            pltpu.VMEM((SEQ, HEAD_DIM), F32),
        ] + _flash_scratch(),
        compiler_params=pltpu.CompilerParams(
            dimension_semantics=("arbitrary", "arbitrary", "arbitrary", "arbitrary"),
            vmem_limit_bytes=VMEM_LIMIT),
        name="diff_attn",
    )(qkv, qkv, qkv, bias, lam, subln_g.reshape(1, HEAD_DIM))


OUT_TM = 512


def _rms(y, g):
    return y * lax.rsqrt(jnp.mean(y * y, axis=-1, keepdims=True) + RMS_EPS) * g


def _outproj_kernel(o_ref, x_ref, w_ref, g_ref, gate_ref, out_ref, wb_ref):
    _cast_weight_once(w_ref, wb_ref)
    y = jnp.dot(o_ref[...], wb_ref[...], preferred_element_type=F32)
    out_ref[...] = x_ref[...] + gate_ref[0] * _rms(y, g_ref[...])


def _out_proj(o, x2, w_layers, layer, g, gate):
    tiles_per_batch = SEQ // OUT_TM
    return pl.pallas_call(
        _outproj_kernel,
        grid=(ROWS // OUT_TM,),
        in_specs=[
            pl.BlockSpec((OUT_TM, D_MODEL), lambda i: (i, 0)),
            pl.BlockSpec((OUT_TM, D_MODEL), lambda i: (i, 0)),
            pl.BlockSpec((1, D_MODEL, D_MODEL), lambda i: (layer, 0, 0), pipeline_mode=pl.Buffered(1)),
            pl.BlockSpec((1, D_MODEL), lambda i: (0, 0)),
            pl.BlockSpec((1, 1, D_MODEL), lambda i: (i // tiles_per_batch, 0, 0)),
        ],
        out_specs=pl.BlockSpec((OUT_TM, D_MODEL), lambda i: (i, 0)),
        out_shape=jax.ShapeDtypeStruct((ROWS, D_MODEL), F32),
        scratch_shapes=[pltpu.VMEM((D_MODEL, D_MODEL), BF16)],
        compiler_params=pltpu.CompilerParams(
            dimension_semantics=("arbitrary",), vmem_limit_bytes=VMEM_LIMIT),
        name="out_proj",
    )(o, x2, w_layers, g.reshape(1, D_MODEL), gate)


FFN_TM = 1024
FFN_TF = 256


def _ffn_kernel(x_ref, g2_ref, sc_ref, sh_ref, wg_ref, wu_ref, wo_ref, g3_ref, gate_ref, out_ref,
                h_ref, acc_ref):
    c = pl.program_id(1)

    @pl.when(c == 0)
    def _():
        h_ref[...] = _norm_mod(x_ref[...], g2_ref[...], sc_ref[0], sh_ref[0]).astype(BF16)
        acc_ref[...] = jnp.zeros(acc_ref.shape, F32)

    h = h_ref[...]
    gp = jnp.dot(h, wg_ref[0].astype(BF16), preferred_element_type=F32)
    up = jnp.dot(h, wu_ref[0].astype(BF16), preferred_element_type=F32)
    act = (gp / (1.0 + jnp.exp(-gp)) * up).astype(BF16)
    acc_ref[...] += jnp.dot(act, wo_ref[0].astype(BF16), preferred_element_type=F32)

    @pl.when(c == pl.num_programs(1) - 1)
    def _():
        out_ref[...] = x_ref[...] + gate_ref[0] * _rms(acc_ref[...], g3_ref[...])


def _ffn(x2, g2, sc, sh, w_in_layers, w_out_layers, layer, g3, gate):
    tiles_per_batch = SEQ // FFN_TM
    n_chunks = D_FF // FFN_TF
    vec = pl.BlockSpec((1, 1, D_MODEL), lambda i, c: (i // tiles_per_batch, 0, 0))
    gvec = pl.BlockSpec((1, D_MODEL), lambda i, c: (0, 0))
    return pl.pallas_call(
        _ffn_kernel,
        grid=(ROWS // FFN_TM, n_chunks),
        in_specs=[
            pl.BlockSpec((FFN_TM, D_MODEL), lambda i, c: (i, 0)),
            gvec, vec, vec,
            pl.BlockSpec((1, D_MODEL, FFN_TF), lambda i, c: (layer, 0, c)),
            pl.BlockSpec((1, D_MODEL, FFN_TF), lambda i, c: (layer, 0, n_chunks + c)),
            pl.BlockSpec((1, FFN_TF, D_MODEL), lambda i, c: (layer, c, 0)),
            gvec, vec,
        ],
        out_specs=pl.BlockSpec((FFN_TM, D_MODEL), lambda i, c: (i, 0)),
        out_shape=jax.ShapeDtypeStruct((ROWS, D_MODEL), F32),
        scratch_shapes=[
            pltpu.VMEM((FFN_TM, D_MODEL), BF16),
            pltpu.VMEM((FFN_TM, D_MODEL), F32),
        ],
        compiler_params=pltpu.CompilerParams(
            dimension_semantics=("parallel", "arbitrary"), vmem_limit_bytes=VMEM_LIMIT),
        name="ffn",
    )(x2, g2.reshape(1, D_MODEL), sc, sh, w_in_layers, w_in_layers, w_out_layers, g3.reshape(1, D_MODEL), gate)


def kernel(x, c, rel_bias, ada_w, ada_b, norm_g, moba_w_qkv, moba_w_o, diff_w_qkv, diff_w_o, diff_lambda,
           diff_subln_g, ffn_w_in, ffn_w_out):
    x2 = x.reshape(ROWS, D_MODEL)
    mod = _adaln_mod(c, ada_w, ada_b)
    bias = _bias_tiles(rel_bias)
    for i in range(DEPTH):
        sh_a, sc_a, g_a, sh_f, sc_f, g_f = [
            mod[i, :, j * D_MODEL:(j + 1) * D_MODEL].reshape(BATCH, 1, D_MODEL) for j in range(6)]
        if i % 2 == 0:
            qkv, kmean = _qkv_proj(x2, norm_g[i, 0], sc_a, sh_a, moba_w_qkv, i // 2, HEAD_DIM ** -0.5, True)
            o = _moba_attention(qkv, kmean.reshape(BATCH * N_KV_BLOCKS, D_MODEL), bias)
            w_o = moba_w_o
        else:
            (qkv,) = _qkv_proj(x2, norm_g[i, 0], sc_a, sh_a, diff_w_qkv, i // 2, DIFF_HALF ** -0.5, False)
            o = _diff_attention(qkv, bias, diff_lambda[i // 2], diff_subln_g[i // 2], i)
            w_o = diff_w_o
        x2 = _out_proj(o, x2, w_o, i // 2, norm_g[i, 1], g_a)
        x2 = _ffn(x2, norm_g[i, 2], sc_f, sh_f, ffn_w_in, ffn_w_out, i, norm_g[i, 3], g_f)
    return x2.reshape(BATCH, SEQ, D_MODEL)
```

```python
import functools
import math

import numpy as np
import jax
import jax.numpy as jnp
from jax import lax
from jax.experimental import pallas as pl
from jax.experimental.pallas import tpu as pltpu

D_MODEL = 1024
BATCH = 2
SEQ = 8192
DEPTH = 2
N_HEADS = 8
HEAD_DIM = D_MODEL // N_HEADS
DIFF_HALF = HEAD_DIM // 2
MOBA_BLOCK = 256
MOBA_TOPK = 3
D_FF = 2816
N_BUCKETS = 32
MAX_EXACT = N_BUCKETS // 2
MAX_DISTANCE = 2048
RMS_EPS = 1e-6
NEG_INF = -1e30

ROWS = BATCH * SEQ
N_KV_BLOCKS = SEQ // MOBA_BLOCK
TILE = MOBA_BLOCK
VMEM_LIMIT = 48 * 1024 * 1024

F32 = jnp.float32
BF16 = jnp.bfloat16


def _bucket_of_distance():
    n = np.arange(MAX_DISTANCE + 1)
    nf = np.maximum(n, 1).astype(np.float64)
    val = np.log(nf / MAX_EXACT) / math.log(MAX_DISTANCE / MAX_EXACT) * (N_BUCKETS - MAX_EXACT)
    frac = np.abs(val - np.round(val))
    assert np.all((frac > 5e-5) | (n <= MAX_EXACT) | (n == MAX_DISTANCE))
    large = np.minimum(MAX_EXACT + np.floor(val + 1e-9).astype(np.int64), N_BUCKETS - 1)
    return np.where(n < MAX_EXACT, n, large)


_BUCKETS = _bucket_of_distance()
FAR_DISTANCE = int(np.min(np.nonzero(_BUCKETS == N_BUCKETS - 1)[0]))
N_BIAS_TILES = (FAR_DISTANCE + TILE - 1) // TILE + 1
N_TABLE_TILES = N_BIAS_TILES + 2


def _bias_bucket_tiles():
    i = np.arange(TILE)[:, None]
    j = np.arange(TILE)[None, :]
    out = []
    for t in range(-1, N_BIAS_TILES + 1):
        rel = t * TILE + i - j
        out.append(np.where(rel >= 0, _BUCKETS[np.clip(rel, 0, MAX_DISTANCE)], -1))
    return np.stack(out).astype(np.int32)


def _table_index(tiles_behind):
    return jnp.clip(tiles_behind, -1, N_BIAS_TILES) + 1


MOD_TN = 512


def _mod_kernel(ct_ref, w_ref, b_ref, o_ref):
    ct = ct_ref[...]
    cact = ct / (1.0 + jnp.exp(-ct))
    w = w_ref[0]
    for b in range(BATCH):
        row = jnp.sum(w * cact[:, b:b + 1], axis=0, keepdims=True)
        o_ref[0, b:b + 1, :] = row + b_ref[0]


def _adaln_mod(c, ada_w, ada_b):
    n_out = ada_w.shape[-1]
    return pl.pallas_call(
        _mod_kernel,
        grid=(DEPTH, n_out // MOD_TN),
        in_specs=[
            pl.BlockSpec((D_MODEL, BATCH), lambda i, n: (0, 0)),
            pl.BlockSpec((1, D_MODEL, MOD_TN), lambda i, n: (i, 0, n)),
            pl.BlockSpec((1, 1, MOD_TN), lambda i, n: (i, 0, n)),
        ],
        out_specs=pl.BlockSpec((1, BATCH, MOD_TN), lambda i, n: (i, 0, n)),
        out_shape=jax.ShapeDtypeStruct((DEPTH, BATCH, n_out), F32),
        name="adaln_mod",
    )(c.T, ada_w, ada_b.reshape(DEPTH, 1, n_out))


LOG2_E = math.log2(math.e)


def _bias_kernel(tab_ref, bkt_ref, o_ref, *, buckets_present):
    h = pl.program_id(0)
    far = tab_ref[N_BUCKETS - 1, h]
    for t, present in enumerate(buckets_present):
        bkt = bkt_ref[t]
        acc = jnp.full(bkt.shape, NEG_INF, F32)
        for b in present:
            acc = jnp.where(bkt == b, (tab_ref[b, h] - far) * LOG2_E, acc)
        o_ref[0, t] = acc


def _bias_tiles(rel_bias):
    bkt = _bias_bucket_tiles()
    buckets_present = tuple(tuple(int(b) for b in np.unique(tile) if b >= 0) for tile in bkt)
    return pl.pallas_call(
        functools.partial(_bias_kernel, buckets_present=buckets_present),
        grid=(N_HEADS,),
        in_specs=[
            pl.BlockSpec(memory_space=pltpu.SMEM),
            pl.BlockSpec((N_TABLE_TILES, TILE, TILE), lambda h: (0, 0, 0)),
        ],
        out_specs=pl.BlockSpec((1, N_TABLE_TILES, TILE, TILE), lambda h: (h, 0, 0, 0)),
        out_shape=jax.ShapeDtypeStruct((N_HEADS, N_TABLE_TILES, TILE, TILE), F32),
        name="bias_tiles",
    )(rel_bias, jnp.asarray(bkt))


QKV_TM = 512


def _norm_mod(x, g, sc, sh):
    y = x * lax.rsqrt(jnp.mean(x * x, axis=-1, keepdims=True) + RMS_EPS)
    return (y * g) * (1.0 + sc) + sh


def _cast_weight_once(w_ref, wb_ref):
    @pl.when(pl.program_id(0) == 0)
    def _():
        wb_ref[...] = w_ref[0].astype(BF16)


def _qkv_kernel(x_ref, g_ref, sc_ref, sh_ref, w_ref, o_ref, *rest, q_scale):
    *km_ref, wb_ref = rest
    _cast_weight_once(w_ref, wb_ref)
    hb = _norm_mod(x_ref[...], g_ref[...], sc_ref[0], sh_ref[0]).astype(BF16)
    for n in range(3):
        r = jnp.dot(hb, wb_ref[:, n * D_MODEL:(n + 1) * D_MODEL], preferred_element_type=F32)
        if n == 0:
            r = r * q_scale
        if n == 1 and km_ref:
            km_ref[0][0] = jnp.mean(r.reshape(QKV_TM // MOBA_BLOCK, MOBA_BLOCK, D_MODEL), axis=1)
        o_ref[:, n * D_MODEL:(n + 1) * D_MODEL] = r.astype(BF16)


def _qkv_proj(x2, g, sc, sh, w_layers, layer, q_scale, with_kmean):
    tiles_per_batch = SEQ // QKV_TM
    n_tiles = ROWS // QKV_TM
    vec = pl.BlockSpec((1, 1, D_MODEL), lambda i: (i // tiles_per_batch, 0, 0))
    out_shape = [jax.ShapeDtypeStruct((ROWS, 3 * D_MODEL), BF16)]
    out_specs = [pl.BlockSpec((QKV_TM, 3 * D_MODEL), lambda i: (i, 0))]
    if with_kmean:
        per_tile = QKV_TM // MOBA_BLOCK
        out_shape.append(jax.ShapeDtypeStruct((n_tiles, per_tile, D_MODEL), F32))
        out_specs.append(pl.BlockSpec((1, per_tile, D_MODEL), lambda i: (i, 0, 0)))
    return pl.pallas_call(
        functools.partial(_qkv_kernel, q_scale=q_scale),
        grid=(n_tiles,),
        in_specs=[
            pl.BlockSpec((QKV_TM, D_MODEL), lambda i: (i, 0)),
            pl.BlockSpec((1, D_MODEL), lambda i: (0, 0)),
            vec, vec,
            pl.BlockSpec((1, D_MODEL, 3 * D_MODEL), lambda i: (layer, 0, 0), pipeline_mode=pl.Buffered(1)),
        ],
        out_specs=out_specs,
        out_shape=out_shape,
        scratch_shapes=[pltpu.VMEM((D_MODEL, 3 * D_MODEL), BF16)],
        compiler_params=pltpu.CompilerParams(
            dimension_semantics=("arbitrary",), vmem_limit_bytes=VMEM_LIMIT),
        name="qkv_moba" if with_kmean else "qkv_diff",
    )(x2, g.reshape(1, D_MODEL), sc, sh, w_layers)


_NT = (((1,), (1,)), ((), ()))
GROUP = 4
GROUP_KEYS = GROUP * TILE
N_GROUPS = SEQ // GROUP_KEYS
ROW_TILE = 2 * TILE
N_ROW_TILES = SEQ // ROW_TILE
ROW_TILES_PER_GROUP = GROUP_KEYS // ROW_TILE
SOFTMAX_ROWS = 64
BIAS_FREE_FROM_ROW_TILE = -(-(N_BIAS_TILES + GROUP - 1) // (ROW_TILE // TILE))


def _flash_scratch():
    scores = pltpu.VMEM((ROW_TILE, GROUP_KEYS), F32)
    probs = pltpu.VMEM((ROW_TILE, GROUP_KEYS), BF16)
    rescale = pltpu.VMEM((ROW_TILE, 1), F32)
    values = pltpu.VMEM((2, GROUP_KEYS, 2 * HEAD_DIM), BF16)
    running_max = pltpu.VMEM((SEQ, 1), F32)
    acc = pltpu.VMEM((SEQ, 2 * HEAD_DIM), F32)
    return [scores, scores, probs, probs, rescale, rescale, values, running_max, acc]


def _tile_rows(tile):
    return pl.ds(pl.multiple_of(tile * ROW_TILE, ROW_TILE), ROW_TILE)


def _flash_sweep(grp, qa_ref, keys_ref, v_ref, bias_ref,
                 s0_ref, s1_ref, p0_ref, p1_ref, alpha0_ref, alpha1_ref, va_ref, m_ref, acc_ref):
    half = GROUP_KEYS // 2
    first = grp * ROW_TILES_PER_GROUP
    last = N_ROW_TILES - 1
    va_new = va_ref.at[grp % 2]
    va_old = va_ref.at[1 - grp % 2]
    va_new[:, :HEAD_DIM] = v_ref[...]
    va_new[:, HEAD_DIM:] = jnp.ones(v_ref.shape, BF16)

    def scores(tile, s_ref, n_keys=GROUP_KEYS, biased=True):
        raw = lax.dot_general(qa_ref[_tile_rows(tile), :], keys_ref[:n_keys, :], _NT,
                              preferred_element_type=F32)
        if not biased:
            s_ref[:, :n_keys] = raw
            return
        for r in range(ROW_TILE // TILE):
            for j in range(n_keys // TILE):
                bias = bias_ref[0, _table_index(tile * (ROW_TILE // TILE) + r - (grp * GROUP + j))]
                s_ref[r * TILE:(r + 1) * TILE, j * TILE:(j + 1) * TILE] = (
                    raw[r * TILE:(r + 1) * TILE, j * TILE:(j + 1) * TILE] + bias)

    def softmax(tile, s_ref, p_ref, alpha_ref, n_keys=GROUP_KEYS):
        for c in range(ROW_TILE // SOFTMAX_ROWS):
            chunk = slice(c * SOFTMAX_ROWS, (c + 1) * SOFTMAX_ROWS)
            rows = pl.ds(pl.multiple_of(tile * ROW_TILE + c * SOFTMAX_ROWS, SOFTMAX_ROWS), SOFTMAX_ROWS)
            s = s_ref[chunk, :n_keys]
            m_prev = m_ref[rows, :]
            m_new = jnp.maximum(m_prev, jnp.max(s, axis=1, keepdims=True))
            alpha_ref[chunk, :] = jnp.exp2(m_prev - m_new)
            m_ref[rows, :] = m_new
            p_ref[chunk, :n_keys] = jnp.exp2(s - m_new).astype(BF16)

    def accumulate(tile, p_ref, alpha_ref, va, n_keys=GROUP_KEYS):
        rows = _tile_rows(tile)
        acc_ref[rows, :] = (alpha_ref[...] * acc_ref[rows, :]
                            + jnp.dot(p_ref[:, :n_keys], va[:n_keys, :], preferred_element_type=F32))

    @pl.when(grp == 0)
    def _():
        scores(first, s0_ref, half)
        scores(first + 1, s1_ref)
        softmax(first, s0_ref, p0_ref, alpha0_ref, half)

    @pl.when(grp > 0)
    def _():
        scores(first, s0_ref, half)
        softmax(last, s1_ref, p1_ref, alpha1_ref)
        accumulate(last - 1, p0_ref, alpha0_ref, va_old)
        scores(first + 1, s1_ref)
        softmax(first, s0_ref, p0_ref, alpha0_ref, half)
        accumulate(last, p1_ref, alpha1_ref, va_old)

    @pl.when(grp < N_GROUPS - 1)
    def _():
        scores(first + 2, s0_ref)
        softmax(first + 1, s1_ref, p1_ref, alpha1_ref)
        accumulate(first, p0_ref, alpha0_ref, va_new, half)
        scores(first + 3, s1_ref)
        softmax(first + 2, s0_ref, p0_ref, alpha0_ref)
        accumulate(first + 1, p1_ref, alpha1_ref, va_new)

        def pair(t, carry, first_biased):
            tile = first + 3 + 2 * t
            scores(tile + 1, s0_ref, biased=first_biased)
            softmax(tile, s1_ref, p1_ref, alpha1_ref)
            accumulate(tile - 1, p0_ref, alpha0_ref, va_new)
            scores(tile + 2, s1_ref, biased=False)
            softmax(tile + 1, s0_ref, p0_ref, alpha0_ref)
            accumulate(tile, p1_ref, alpha1_ref, va_new)
            return carry

        assert BIAS_FREE_FROM_ROW_TILE == 5
        n_pairs = (last - first - 3) // 2

        @pl.when(n_pairs > 0)
        def _():
            pair(0, 0, True)
            lax.fori_loop(1, n_pairs, functools.partial(pair, first_biased=False), 0)

    @pl.when(grp == N_GROUPS - 1)
    def _():
        softmax(last, s1_ref, p1_ref, alpha1_ref)
        accumulate(last - 1, p0_ref, alpha0_ref, va_new, half)
        accumulate(last, p1_ref, alpha1_ref, va_new)


def _reset_state(rows, m_ref, acc_ref):
    m_ref[rows, :] = jnp.full((rows.size, 1), NEG_INF, F32)
    acc_ref[rows, :] = jnp.zeros((rows.size, 2 * HEAD_DIM), F32)


def _attention_of(acc):
    return acc[:, :HEAD_DIM] / acc[:, HEAD_DIM:]


GATE_QUERIES = 4 * TILE


def _moba_gating(q_ref, km_ref, qa_ref, m_ref, acc_ref):
    blk = lax.broadcasted_iota(jnp.int32, (N_KV_BLOCKS, GATE_QUERIES), 0)
    blkf = blk.astype(F32)
    tile_in_step = lax.broadcasted_iota(jnp.int32, (N_KV_BLOCKS, GATE_QUERIES), 1) // TILE

    def gate_step(step, carry):
        rows = pl.ds(pl.multiple_of(step * GATE_QUERIES, GATE_QUERIES), GATE_QUERIES)
        _reset_state(rows, m_ref, acc_ref)
        q = q_ref[rows, :]
        gate = lax.dot_general(km_ref[...].astype(BF16), q, _NT, preferred_element_type=F32)
        tile = step * (GATE_QUERIES // TILE) + tile_in_step
        past = blk < tile
        g = jnp.where(past, gate, NEG_INF)
        madd = jnp.full(gate.shape, NEG_INF, F32)
        for _ in range(MOBA_TOPK):
            mx = jnp.max(g, axis=0, keepdims=True)
            first = jnp.min(jnp.where(g == mx, blkf, float(N_KV_BLOCKS)), axis=0, keepdims=True)
            hit = blkf == first
            madd = jnp.where(hit & past, 0.0, madd)
            g = jnp.where(hit, -jnp.inf, g)
        madd = jnp.where(blk == tile, 0.0, madd)
        madd = jnp.concatenate([madd, jnp.zeros((HEAD_DIM - N_KV_BLOCKS, GATE_QUERIES), F32)], axis=0)
        qa_ref[rows, :HEAD_DIM] = q
        qa_ref[rows, HEAD_DIM:] = madd.T.astype(BF16)
        return carry

    lax.fori_loop(0, SEQ // GATE_QUERIES, gate_step, 0)


def _moba_kernel(q_ref, k_ref, v_ref, km_ref, bias_ref, o_ref, qa_ref, ka_ref, *flash_refs):
    grp = pl.program_id(2)
    m_ref, acc_ref = flash_refs[-2:]

    @pl.when(grp == 0)
    def _():
        _moba_gating(q_ref, km_ref, qa_ref, m_ref, acc_ref)

    lane = lax.broadcasted_iota(jnp.int32, (GROUP_KEYS, HEAD_DIM), 1)
    key_block = grp * GROUP + lax.broadcasted_iota(jnp.int32, (GROUP_KEYS, HEAD_DIM), 0) // TILE
    ka_ref[:, :HEAD_DIM] = k_ref[...]
    ka_ref[:, HEAD_DIM:] = jnp.where(lane == key_block, 1.0, 0.0).astype(BF16)

    _flash_sweep(grp, qa_ref, ka_ref, v_ref, bias_ref, *flash_refs)

    @pl.when(grp == N_GROUPS - 1)
    def _():
        def finish(tile, carry):
            rows = _tile_rows(tile)
            o_ref[rows, :] = _attention_of(acc_ref[rows, :]).astype(BF16)
            return carry

        lax.fori_loop(0, N_ROW_TILES, finish, 0)


def _moba_attention(qkv, kmean, bias):
    return pl.pallas_call(
        _moba_kernel,
        grid=(BATCH, N_HEADS, N_GROUPS),
        in_specs=[
            pl.BlockSpec((SEQ, HEAD_DIM), lambda b, h, g: (b, h)),
            pl.BlockSpec((GROUP_KEYS, HEAD_DIM), lambda b, h, g: (b * N_GROUPS + g, N_HEADS + h)),
            pl.BlockSpec((GROUP_KEYS, HEAD_DIM), lambda b, h, g: (b * N_GROUPS + g, 2 * N_HEADS + h)),
            pl.BlockSpec((N_KV_BLOCKS, HEAD_DIM), lambda b, h, g: (b, h)),
            pl.BlockSpec((1, N_TABLE_TILES, TILE, TILE), lambda b, h, g: (h, 0, 0, 0)),
        ],
        out_specs=pl.BlockSpec((SEQ, HEAD_DIM), lambda b, h, g: (b, h)),
        out_shape=jax.ShapeDtypeStruct((ROWS, D_MODEL), BF16),
        scratch_shapes=[
            pltpu.VMEM((SEQ, 2 * HEAD_DIM), BF16),
            pltpu.VMEM((GROUP_KEYS, 2 * HEAD_DIM), BF16),
        ] + _flash_scratch(),
        compiler_params=pltpu.CompilerParams(
            dimension_semantics=("arbitrary", "arbitrary", "arbitrary"), vmem_limit_bytes=VMEM_LIMIT),
        name="moba_attn",
    )(qkv, qkv, qkv, kmean, bias)


def _diff_kernel(q_ref, k_ref, v_ref, bias_ref, lam_ref, g_ref, o_ref, qa_ref, map0_ref, *flash_refs,
                 lambda_init):
    which = pl.program_id(2)
    grp = pl.program_id(3)
    m_ref, acc_ref = flash_refs[-2:]

    @pl.when(grp == 0)
    def _():
        lane_map = lax.broadcasted_iota(jnp.int32, (ROW_TILE, HEAD_DIM), 1) // DIFF_HALF

        def split(tile, carry):
            rows = _tile_rows(tile)
            _reset_state(rows, m_ref, acc_ref)
            q = q_ref[rows, :]
            qa_ref[rows, :] = jnp.where(lane_map == which, q, jnp.zeros_like(q))
            return carry

        lax.fori_loop(0, N_ROW_TILES, split, 0)

    _flash_sweep(grp, qa_ref, k_ref, v_ref, bias_ref, *flash_refs)

    @pl.when((grp == N_GROUPS - 1) & (which == 0))
    def _():
        def keep(tile, carry):
            rows = _tile_rows(tile)
            map0_ref[rows, :] = _attention_of(acc_ref[rows, :])
            return carry

        lax.fori_loop(0, N_ROW_TILES, keep, 0)

    @pl.when((grp == N_GROUPS - 1) & (which == 1))
    def _():
        lam = lam_ref[...]
        lam_full = (jnp.exp(jnp.sum(lam[0:1] * lam[1:2], axis=1, keepdims=True))
                    - jnp.exp(jnp.sum(lam[2:3] * lam[3:4], axis=1, keepdims=True)) + lambda_init)

        def finish(tile, carry):
            rows = _tile_rows(tile)
            o = map0_ref[rows, :] - lam_full * _attention_of(acc_ref[rows, :])
            o = o * lax.rsqrt(jnp.mean(o * o, axis=-1, keepdims=True) + RMS_EPS) * g_ref[...]
            o_ref[rows, :] = (o * (1.0 - lambda_init)).astype(BF16)
            return carry

        lax.fori_loop(0, N_ROW_TILES, finish, 0)


def _diff_attention(qkv, bias, lam, subln_g, layer_idx):
    lambda_init = 0.8 - 0.6 * math.exp(-0.3 * layer_idx)
    return pl.pallas_call(
        functools.partial(_diff_kernel, lambda_init=lambda_init),
        grid=(BATCH, N_HEADS, 2, N_GROUPS),
        in_specs=[
            pl.BlockSpec((SEQ, HEAD_DIM), lambda b, h, c, g: (b, h)),
            pl.BlockSpec((GROUP_KEYS, HEAD_DIM), lambda b, h, c, g: (b * N_GROUPS + g, N_HEADS + h)),
            pl.BlockSpec((GROUP_KEYS, HEAD_DIM), lambda b, h, c, g: (b * N_GROUPS + g, 2 * N_HEADS + h)),
            pl.BlockSpec((1, N_TABLE_TILES, TILE, TILE), lambda b, h, c, g: (h, 0, 0, 0)),
            pl.BlockSpec((4, DIFF_HALF), lambda b, h, c, g: (0, 0)),
            pl.BlockSpec((1, HEAD_DIM), lambda b, h, c, g: (0, 0)),
        ],
        out_specs=pl.BlockSpec((SEQ, HEAD_DIM), lambda b, h, c, g: (b, h)),
        out_shape=jax.ShapeDtypeStruct((ROWS, D_MODEL), BF16),
        scratch_shapes=[
            pltpu.VMEM((SEQ, HEAD_DIM), BF16),
            pltpu.VMEM((SEQ, HEAD_DIM), F32),
        ] + _flash_scratch(),
        compiler_params=pltpu.CompilerParams(
            dimension_semantics=("arbitrary", "arbitrary", "arbitrary", "arbitrary"),
            vmem_limit_bytes=VMEM_LIMIT),
        name="diff_attn",
    )(qkv, qkv, qkv, bias, lam, subln_g.reshape(1, HEAD_DIM))


OUT_TM = 512


def _rms(y, g):
    return y * lax.rsqrt(jnp.mean(y * y, axis=-1, keepdims=True) + RMS_EPS) * g


def _outproj_kernel(o_ref, x_ref, w_ref, g_ref, gate_ref, out_ref, wb_ref):
    _cast_weight_once(w_ref, wb_ref)
    y = jnp.dot(o_ref[...], wb_ref[...], preferred_element_type=F32)
    out_ref[...] = x_ref[...] + gate_ref[0] * _rms(y, g_ref[...])


def _out_proj(o, x2, w_layers, layer, g, gate):
    tiles_per_batch = SEQ // OUT_TM
    return pl.pallas_call(
        _outproj_kernel,
        grid=(ROWS // OUT_TM,),
        in_specs=[
            pl.BlockSpec((OUT_TM, D_MODEL), lambda i: (i, 0)),
            pl.BlockSpec((OUT_TM, D_MODEL), lambda i: (i, 0)),
            pl.BlockSpec((1, D_MODEL, D_MODEL), lambda i: (layer, 0, 0), pipeline_mode=pl.Buffered(1)),
            pl.BlockSpec((1, D_MODEL), lambda i: (0, 0)),
            pl.BlockSpec((1, 1, D_MODEL), lambda i: (i // tiles_per_batch, 0, 0)),
        ],
        out_specs=pl.BlockSpec((OUT_TM, D_MODEL), lambda i: (i, 0)),
        out_shape=jax.ShapeDtypeStruct((ROWS, D_MODEL), F32),
        scratch_shapes=[pltpu.VMEM((D_MODEL, D_MODEL), BF16)],
        compiler_params=pltpu.CompilerParams(
            dimension_semantics=("arbitrary",), vmem_limit_bytes=VMEM_LIMIT),
        name="out_proj",
    )(o, x2, w_layers, g.reshape(1, D_MODEL), gate)


FFN_TM = 1024
FFN_TF = 256


def _ffn_kernel(x_ref, g2_ref, sc_ref, sh_ref, wg_ref, wu_ref, wo_ref, g3_ref, gate_ref, out_ref,
                h_ref, acc_ref):
    c = pl.program_id(1)

    @pl.when(c == 0)
    def _():
        h_ref[...] = _norm_mod(x_ref[...], g2_ref[...], sc_ref[0], sh_ref[0]).astype(BF16)
        acc_ref[...] = jnp.zeros(acc_ref.shape, F32)

    h = h_ref[...]
    gp = jnp.dot(h, wg_ref[0].astype(BF16), preferred_element_type=F32)
    up = jnp.dot(h, wu_ref[0].astype(BF16), preferred_element_type=F32)
    act = (gp / (1.0 + jnp.exp(-gp)) * up).astype(BF16)
    acc_ref[...] += jnp.dot(act, wo_ref[0].astype(BF16), preferred_element_type=F32)

    @pl.when(c == pl.num_programs(1) - 1)
    def _():
        out_ref[...] = x_ref[...] + gate_ref[0] * _rms(acc_ref[...], g3_ref[...])


def _ffn(x2, g2, sc, sh, w_in_layers, w_out_layers, layer, g3, gate):
    tiles_per_batch = SEQ // FFN_TM
    n_chunks = D_FF // FFN_TF
    vec = pl.BlockSpec((1, 1, D_MODEL), lambda i, c: (i // tiles_per_batch, 0, 0))
    gvec = pl.BlockSpec((1, D_MODEL), lambda i, c: (0, 0))
    return pl.pallas_call(
        _ffn_kernel,
        grid=(ROWS // FFN_TM, n_chunks),
        in_specs=[
            pl.BlockSpec((FFN_TM, D_MODEL), lambda i, c: (i, 0)),
            gvec, vec, vec,
            pl.BlockSpec((1, D_MODEL, FFN_TF), lambda i, c: (layer, 0, c)),
            pl.BlockSpec((1, D_MODEL, FFN_TF), lambda i, c: (layer, 0, n_chunks + c)),
            pl.BlockSpec((1, FFN_TF, D_MODEL), lambda i, c: (layer, c, 0)),
            gvec, vec,
        ],
        out_specs=pl.BlockSpec((FFN_TM, D_MODEL), lambda i, c: (i, 0)),
        out_shape=jax.ShapeDtypeStruct((ROWS, D_MODEL), F32),
        scratch_shapes=[
            pltpu.VMEM((FFN_TM, D_MODEL), BF16),
            pltpu.VMEM((FFN_TM, D_MODEL), F32),
        ],
        compiler_params=pltpu.CompilerParams(
            dimension_semantics=("parallel", "arbitrary"), vmem_limit_bytes=VMEM_LIMIT),
        name="ffn",
    )(x2, g2.reshape(1, D_MODEL), sc, sh, w_in_layers, w_in_layers, w_out_layers, g3.reshape(1, D_MODEL), gate)


def kernel(x, c, rel_bias, ada_w, ada_b, norm_g, moba_w_qkv, moba_w_o, diff_w_qkv, diff_w_o, diff_lambda,
           diff_subln_g, ffn_w_in, ffn_w_out):
    x2 = x.reshape(ROWS, D_MODEL)
    mod = _adaln_mod(c, ada_w, ada_b)
    bias = _bias_tiles(rel_bias)
    for i in range(DEPTH):
        sh_a, sc_a, g_a, sh_f, sc_f, g_f = [
            mod[i, :, j * D_MODEL:(j + 1) * D_MODEL].reshape(BATCH, 1, D_MODEL) for j in range(6)]
        if i % 2 == 0:
            qkv, kmean = _qkv_proj(x2, norm_g[i, 0], sc_a, sh_a, moba_w_qkv, i // 2,
                                   HEAD_DIM ** -0.5 * LOG2_E, True)
            o = _moba_attention(qkv, kmean.reshape(BATCH * N_KV_BLOCKS, D_MODEL), bias)
            w_o = moba_w_o
        else:
            (qkv,) = _qkv_proj(x2, norm_g[i, 0], sc_a, sh_a, diff_w_qkv, i // 2,
                               DIFF_HALF ** -0.5 * LOG2_E, False)
            o = _diff_attention(qkv, bias, diff_lambda[i // 2], diff_subln_g[i // 2], i)
            w_o = diff_w_o
        x2 = _out_proj(o, x2, w_o, i // 2, norm_g[i, 1], g_a)
        x2 = _ffn(x2, norm_g[i, 2], sc_f, sh_f, ffn_w_in, ffn_w_out, i, norm_g[i, 3], g_f)
    return x2.reshape(BATCH, SEQ, D_MODEL)
```

```python
import functools
import math

import numpy as np
import jax
import jax.numpy as jnp
from jax import lax
from jax.experimental import pallas as pl
from jax.experimental.pallas import tpu as pltpu

D_MODEL = 1024
BATCH = 2
SEQ = 8192
DEPTH = 2
N_HEADS = 8
HEAD_DIM = D_MODEL // N_HEADS
DIFF_HALF = HEAD_DIM // 2
MOBA_BLOCK = 256
MOBA_TOPK = 3
D_FF = 2816
N_BUCKETS = 32
MAX_EXACT = N_BUCKETS // 2
MAX_DISTANCE = 2048
RMS_EPS = 1e-6
NEG_INF = -1e30

ROWS = BATCH * SEQ
N_KV_BLOCKS = SEQ // MOBA_BLOCK
TILE = MOBA_BLOCK
VMEM_LIMIT = 48 * 1024 * 1024

F32 = jnp.float32
BF16 = jnp.bfloat16


def _bucket_of_distance():
    n = np.arange(MAX_DISTANCE + 1)
    nf = np.maximum(n, 1).astype(np.float64)
    val = np.log(nf / MAX_EXACT) / math.log(MAX_DISTANCE / MAX_EXACT) * (N_BUCKETS - MAX_EXACT)
    frac = np.abs(val - np.round(val))
    assert np.all((frac > 5e-5) | (n <= MAX_EXACT) | (n == MAX_DISTANCE))
    large = np.minimum(MAX_EXACT + np.floor(val + 1e-9).astype(np.int64), N_BUCKETS - 1)
    return np.where(n < MAX_EXACT, n, large)


_BUCKETS = _bucket_of_distance()
FAR_DISTANCE = int(np.min(np.nonzero(_BUCKETS == N_BUCKETS - 1)[0]))
N_BIAS_TILES = (FAR_DISTANCE + TILE - 1) // TILE + 1
N_TABLE_TILES = N_BIAS_TILES + 2


def _bias_bucket_tiles():
    i = np.arange(TILE)[:, None]
    j = np.arange(TILE)[None, :]
    out = []
    for t in range(-1, N_BIAS_TILES + 1):
        rel = t * TILE + i - j
        out.append(np.where(rel >= 0, _BUCKETS[np.clip(rel, 0, MAX_DISTANCE)], -1))
    return np.stack(out).astype(np.int32)


def _table_index(tiles_behind):
    return jnp.clip(tiles_behind, -1, N_BIAS_TILES) + 1


MOD_TN = 512


def _mod_kernel(ct_ref, w_ref, b_ref, o_ref):
    ct = ct_ref[...]
    cact = ct / (1.0 + jnp.exp(-ct))
    w = w_ref[0]
    for b in range(BATCH):
        row = jnp.sum(w * cact[:, b:b + 1], axis=0, keepdims=True)
        o_ref[0, b:b + 1, :] = row + b_ref[0]


def _adaln_mod(c, ada_w, ada_b):
    n_out = ada_w.shape[-1]
    return pl.pallas_call(
        _mod_kernel,
        grid=(DEPTH, n_out // MOD_TN),
        in_specs=[
            pl.BlockSpec((D_MODEL, BATCH), lambda i, n: (0, 0)),
            pl.BlockSpec((1, D_MODEL, MOD_TN), lambda i, n: (i, 0, n)),
            pl.BlockSpec((1, 1, MOD_TN), lambda i, n: (i, 0, n)),
        ],
        out_specs=pl.BlockSpec((1, BATCH, MOD_TN), lambda i, n: (i, 0, n)),
        out_shape=jax.ShapeDtypeStruct((DEPTH, BATCH, n_out), F32),
        name="adaln_mod",
    )(c.T, ada_w, ada_b.reshape(DEPTH, 1, n_out))


LOG2_E = math.log2(math.e)


def _bias_kernel(tab_ref, bkt_ref, o_ref, *, buckets_present):
    h = pl.program_id(0)
    far = tab_ref[N_BUCKETS - 1, h]
    for t, present in enumerate(buckets_present):
        bkt = bkt_ref[t]
        acc = jnp.full(bkt.shape, NEG_INF, F32)
        for b in present:
            acc = jnp.where(bkt == b, (tab_ref[b, h] - far) * LOG2_E, acc)
        o_ref[0, t] = acc


def _bias_tiles(rel_bias):
    bkt = _bias_bucket_tiles()
    buckets_present = tuple(tuple(int(b) for b in np.unique(tile) if b >= 0) for tile in bkt)
    return pl.pallas_call(
        functools.partial(_bias_kernel, buckets_present=buckets_present),
        grid=(N_HEADS,),
        in_specs=[
            pl.BlockSpec(memory_space=pltpu.SMEM),
            pl.BlockSpec((N_TABLE_TILES, TILE, TILE), lambda h: (0, 0, 0)),
        ],
        out_specs=pl.BlockSpec((1, N_TABLE_TILES, TILE, TILE), lambda h: (h, 0, 0, 0)),
        out_shape=jax.ShapeDtypeStruct((N_HEADS, N_TABLE_TILES, TILE, TILE), F32),
        name="bias_tiles",
    )(rel_bias, jnp.asarray(bkt))


QKV_TM = 512


def _norm_mod(x, g, sc, sh):
    y = x * lax.rsqrt(jnp.mean(x * x, axis=-1, keepdims=True) + RMS_EPS)
    return (y * g) * (1.0 + sc) + sh


def _cast_weight_once(w_ref, wb_ref):
    @pl.when(pl.program_id(0) == 0)
    def _():
        wb_ref[...] = w_ref[0].astype(BF16)


def _qkv_kernel(x_ref, g_ref, sc_ref, sh_ref, w_ref, o_ref, *rest, q_scale):
    *km_ref, wb_ref = rest
    _cast_weight_once(w_ref, wb_ref)
    hb = _norm_mod(x_ref[...], g_ref[...], sc_ref[0], sh_ref[0]).astype(BF16)
    for n in range(3):
        r = jnp.dot(hb, wb_ref[:, n * D_MODEL:(n + 1) * D_MODEL], preferred_element_type=F32)
        if n == 0:
            r = r * q_scale
        if n == 1 and km_ref:
            km_ref[0][0] = jnp.mean(r.reshape(QKV_TM // MOBA_BLOCK, MOBA_BLOCK, D_MODEL), axis=1)
        o_ref[:, n * D_MODEL:(n + 1) * D_MODEL] = r.astype(BF16)


def _qkv_proj(x2, g, sc, sh, w_layers, layer, q_scale, with_kmean):
    tiles_per_batch = SEQ // QKV_TM
    n_tiles = ROWS // QKV_TM
    vec = pl.BlockSpec((1, 1, D_MODEL), lambda i: (i // tiles_per_batch, 0, 0))
    out_shape = [jax.ShapeDtypeStruct((ROWS, 3 * D_MODEL), BF16)]
    out_specs = [pl.BlockSpec((QKV_TM, 3 * D_MODEL), lambda i: (i, 0))]
    if with_kmean:
        per_tile = QKV_TM // MOBA_BLOCK
        out_shape.append(jax.ShapeDtypeStruct((n_tiles, per_tile, D_MODEL), F32))
        out_specs.append(pl.BlockSpec((1, per_tile, D_MODEL), lambda i: (i, 0, 0)))
    return pl.pallas_call(
        functools.partial(_qkv_kernel, q_scale=q_scale),
        grid=(n_tiles,),
        in_specs=[
            pl.BlockSpec((QKV_TM, D_MODEL), lambda i: (i, 0)),
            pl.BlockSpec((1, D_MODEL), lambda i: (0, 0)),
            vec, vec,
            pl.BlockSpec((1, D_MODEL, 3 * D_MODEL), lambda i: (layer, 0, 0), pipeline_mode=pl.Buffered(1)),
        ],
        out_specs=out_specs,
        out_shape=out_shape,
        scratch_shapes=[pltpu.VMEM((D_MODEL, 3 * D_MODEL), BF16)],
        compiler_params=pltpu.CompilerParams(
            dimension_semantics=("arbitrary",), vmem_limit_bytes=VMEM_LIMIT),
        name="qkv_moba" if with_kmean else "qkv_diff",
    )(x2, g.reshape(1, D_MODEL), sc, sh, w_layers)


_NT = (((1,), (1,)), ((), ()))
GROUP = 4
GROUP_KEYS = GROUP * TILE
N_GROUPS = SEQ // GROUP_KEYS
ROW_TILE = 2 * TILE
N_ROW_TILES = SEQ // ROW_TILE
ROW_TILES_PER_GROUP = GROUP_KEYS // ROW_TILE
SOFTMAX_ROWS = 64
BIAS_FREE_FROM_ROW_TILE = -(-(N_BIAS_TILES + GROUP - 1) // (ROW_TILE // TILE))


def _flash_scratch():
    scores = pltpu.VMEM((ROW_TILE, GROUP_KEYS), F32)
    probs = pltpu.VMEM((ROW_TILE, GROUP_KEYS), BF16)
    rescale = pltpu.VMEM((ROW_TILE, 1), F32)
    values = pltpu.VMEM((2, GROUP_KEYS, 2 * HEAD_DIM), BF16)
    running_max = pltpu.VMEM((SEQ, 1), F32)
    acc = pltpu.VMEM((SEQ, 2 * HEAD_DIM), F32)
    return [scores, scores, probs, probs, rescale, rescale, values, running_max, acc]


def _tile_rows(tile):
    return pl.ds(pl.multiple_of(tile * ROW_TILE, ROW_TILE), ROW_TILE)


def _flash_sweep(grp, qa_ref, keys_ref, v_ref, bias_ref,
                 s0_ref, s1_ref, p0_ref, p1_ref, alpha0_ref, alpha1_ref, va_ref, m_ref, acc_ref):
    half = GROUP_KEYS // 2
    first = grp * ROW_TILES_PER_GROUP
    last = N_ROW_TILES - 1
    va_new = va_ref.at[grp % 2]
    va_old = va_ref.at[1 - grp % 2]
    va_new[:, :HEAD_DIM] = v_ref[...]
    va_new[:, HEAD_DIM:] = jnp.ones(v_ref.shape, BF16)

    def scores(tile, s_ref, n_keys=GROUP_KEYS, biased=True):
        raw = lax.dot_general(qa_ref[_tile_rows(tile), :], keys_ref[:n_keys, :], _NT,
                              preferred_element_type=F32)
        if not biased:
            s_ref[:, :n_keys] = raw
            return
        for r in range(ROW_TILE // TILE):
            for j in range(n_keys // TILE):
                bias = bias_ref[0, _table_index(tile * (ROW_TILE // TILE) + r - (grp * GROUP + j))]
                s_ref[r * TILE:(r + 1) * TILE, j * TILE:(j + 1) * TILE] = (
                    raw[r * TILE:(r + 1) * TILE, j * TILE:(j + 1) * TILE] + bias)

    def softmax(tile, s_ref, p_ref, alpha_ref, n_keys=GROUP_KEYS):
        for c in range(ROW_TILE // SOFTMAX_ROWS):
            chunk = slice(c * SOFTMAX_ROWS, (c + 1) * SOFTMAX_ROWS)
            rows = pl.ds(pl.multiple_of(tile * ROW_TILE + c * SOFTMAX_ROWS, SOFTMAX_ROWS), SOFTMAX_ROWS)
            s = s_ref[chunk, :n_keys]
            m_prev = m_ref[rows, :]
            m_new = jnp.maximum(m_prev, jnp.max(s, axis=1, keepdims=True))
            alpha_ref[chunk, :] = jnp.exp2(m_prev - m_new)
            m_ref[rows, :] = m_new
            p_ref[chunk, :n_keys] = jnp.exp2(s - m_new).astype(BF16)

    def accumulate(tile, p_ref, alpha_ref, va, n_keys=GROUP_KEYS):
        rows = _tile_rows(tile)
        acc_ref[rows, :] = (alpha_ref[...] * acc_ref[rows, :]
                            + jnp.dot(p_ref[:, :n_keys], va[:n_keys, :], preferred_element_type=F32))

    def enter(with_pending):
        scores(first, s0_ref, half)
        if with_pending:
            softmax(last, s1_ref, p1_ref, alpha1_ref)
            accumulate(last - 1, p0_ref, alpha0_ref, va_old)
        scores(first + 1, s1_ref)
        softmax(first, s0_ref, p0_ref, alpha0_ref, half)
        if with_pending:
            accumulate(last, p1_ref, alpha1_ref, va_old)

    def second_pair():
        scores(first + 2, s0_ref)
        softmax(first + 1, s1_ref, p1_ref, alpha1_ref)
        accumulate(first, p0_ref, alpha0_ref, va_new, half)
        scores(first + 3, s1_ref)
        softmax(first + 2, s0_ref, p0_ref, alpha0_ref)
        accumulate(first + 1, p1_ref, alpha1_ref, va_new)

    def pair(t, first_biased=False):
        tile = first + 3 + 2 * t
        scores(tile + 1, s0_ref, biased=first_biased)
        softmax(tile, s1_ref, p1_ref, alpha1_ref)
        accumulate(tile - 1, p0_ref, alpha0_ref, va_new)
        scores(tile + 2, s1_ref, biased=False)
        softmax(tile + 1, s0_ref, p0_ref, alpha0_ref)
        accumulate(tile, p1_ref, alpha1_ref, va_new)

    @pl.when(grp == 0)
    def _():
        enter(False)
        second_pair()

    @pl.when((grp > 0) & (grp < N_GROUPS - 1))
    def _():
        enter(True)
        second_pair()

    @pl.when(grp == N_GROUPS - 1)
    def _():
        enter(True)
        softmax(last, s1_ref, p1_ref, alpha1_ref)
        accumulate(last - 1, p0_ref, alpha0_ref, va_new, half)
        accumulate(last, p1_ref, alpha1_ref, va_new)

    assert BIAS_FREE_FROM_ROW_TILE == 5
    n_pairs = jnp.maximum(last - first - 3, 0) // 2

    @pl.when(n_pairs > 0)
    def _():
        pair(0, first_biased=True)

    @pl.when((n_pairs > 0) & (n_pairs % 2 == 0))
    def _():
        pair(1)

    def two_pairs(u, carry):
        t = 2 - n_pairs % 2 + 2 * u
        pair(t)
        pair(t + 1)
        return carry

    lax.fori_loop(0, jnp.maximum(n_pairs - 1, 0) // 2, two_pairs, 0)


def _reset_state(rows, m_ref, acc_ref):
    m_ref[rows, :] = jnp.full((rows.size, 1), NEG_INF, F32)
    acc_ref[rows, :] = jnp.zeros((rows.size, 2 * HEAD_DIM), F32)


def _attention_of(acc):
    return acc[:, :HEAD_DIM] / acc[:, HEAD_DIM:]


GATE_QUERIES = 8 * TILE


def _moba_gating(q_ref, km_ref, qa_ref, m_ref, acc_ref):
    blk = lax.broadcasted_iota(jnp.int32, (N_KV_BLOCKS, GATE_QUERIES), 0)
    blkf = blk.astype(F32)
    tile_in_step = lax.broadcasted_iota(jnp.int32, (N_KV_BLOCKS, GATE_QUERIES), 1) // TILE

    def gate_step(step, carry):
        rows = pl.ds(pl.multiple_of(step * GATE_QUERIES, GATE_QUERIES), GATE_QUERIES)
        _reset_state(rows, m_ref, acc_ref)
        q = q_ref[rows, :]
        gate = lax.dot_general(km_ref[...].astype(BF16), q, _NT, preferred_element_type=F32)
        tile = step * (GATE_QUERIES // TILE) + tile_in_step
        past = blk < tile
        g = jnp.where(past, gate, NEG_INF)
        madd = jnp.full(gate.shape, NEG_INF, F32)
        for _ in range(MOBA_TOPK):
            mx = jnp.max(g, axis=0, keepdims=True)
            first = jnp.min(jnp.where(g == mx, blkf, float(N_KV_BLOCKS)), axis=0, keepdims=True)
            hit = blkf == first
            madd = jnp.where(hit & past, 0.0, madd)
            g = jnp.where(hit, -jnp.inf, g)
        madd = jnp.where(blk == tile, 0.0, madd)
        madd = jnp.concatenate([madd, jnp.zeros((HEAD_DIM - N_KV_BLOCKS, GATE_QUERIES), F32)], axis=0)
        qa_ref[rows, :HEAD_DIM] = q
        qa_ref[rows, HEAD_DIM:] = madd.T.astype(BF16)
        return carry

    lax.fori_loop(0, SEQ // GATE_QUERIES, gate_step, 0)


def _moba_kernel(q_ref, k_ref, v_ref, km_ref, bias_ref, o_ref, qa_ref, ka_ref, *flash_refs):
    grp = pl.program_id(2)
    m_ref, acc_ref = flash_refs[-2:]

    @pl.when(grp == 0)
    def _():
        _moba_gating(q_ref, km_ref, qa_ref, m_ref, acc_ref)

    lane = lax.broadcasted_iota(jnp.int32, (GROUP_KEYS, HEAD_DIM), 1)
    key_block = grp * GROUP + lax.broadcasted_iota(jnp.int32, (GROUP_KEYS, HEAD_DIM), 0) // TILE
    ka_ref[:, :HEAD_DIM] = k_ref[...]
    ka_ref[:, HEAD_DIM:] = jnp.where(lane == key_block, 1.0, 0.0).astype(BF16)

    _flash_sweep(grp, qa_ref, ka_ref, v_ref, bias_ref, *flash_refs)

    @pl.when(grp == N_GROUPS - 1)
    def _():
        def finish(tile, carry):
            rows = _tile_rows(tile)
            o_ref[rows, :] = _attention_of(acc_ref[rows, :]).astype(BF16)
            return carry

        lax.fori_loop(0, N_ROW_TILES, finish, 0)


def _moba_attention(qkv, kmean, bias):
    return pl.pallas_call(
        _moba_kernel,
        grid=(BATCH, N_HEADS, N_GROUPS),
        in_specs=[
            pl.BlockSpec((SEQ, HEAD_DIM), lambda b, h, g: (b, h)),
            pl.BlockSpec((GROUP_KEYS, HEAD_DIM), lambda b, h, g: (b * N_GROUPS + g, N_HEADS + h)),
            pl.BlockSpec((GROUP_KEYS, HEAD_DIM), lambda b, h, g: (b * N_GROUPS + g, 2 * N_HEADS + h)),
            pl.BlockSpec((N_KV_BLOCKS, HEAD_DIM), lambda b, h, g: (b, h)),
            pl.BlockSpec((1, N_TABLE_TILES, TILE, TILE), lambda b, h, g: (h, 0, 0, 0)),
        ],
        out_specs=pl.BlockSpec((SEQ, HEAD_DIM), lambda b, h, g: (b, h)),
        out_shape=jax.ShapeDtypeStruct((ROWS, D_MODEL), BF16),
        scratch_shapes=[
            pltpu.VMEM((SEQ, 2 * HEAD_DIM), BF16),
            pltpu.VMEM((GROUP_KEYS, 2 * HEAD_DIM), BF16),
        ] + _flash_scratch(),
        compiler_params=pltpu.CompilerParams(
            dimension_semantics=("arbitrary", "arbitrary", "arbitrary"), vmem_limit_bytes=VMEM_LIMIT),
        name="moba_attn",
    )(qkv, qkv, qkv, kmean, bias)


def _diff_kernel(q_ref, k_ref, v_ref, bias_ref, lam_ref, g_ref, o_ref, qa_ref, map0_ref, *flash_refs,
                 lambda_init):
    which = pl.program_id(2)
    grp = pl.program_id(3)
    m_ref, acc_ref = flash_refs[-2:]

    @pl.when(grp == 0)
    def _():
        lane_map = lax.broadcasted_iota(jnp.int32, (ROW_TILE, HEAD_DIM), 1) // DIFF_HALF

        def split(tile, carry):
            rows = _tile_rows(tile)
            _reset_state(rows, m_ref, acc_ref)
            q = q_ref[rows, :]
            qa_ref[rows, :] = jnp.where(lane_map == which, q, jnp.zeros_like(q))
            return carry

        lax.fori_loop(0, N_ROW_TILES, split, 0)

    _flash_sweep(grp, qa_ref, k_ref, v_ref, bias_ref, *flash_refs)

    @pl.when((grp == N_GROUPS - 1) & (which == 0))
    def _():
        def keep(tile, carry):
            rows = _tile_rows(tile)
            map0_ref[rows, :] = _attention_of(acc_ref[rows, :])
            return carry

        lax.fori_loop(0, N_ROW_TILES, keep, 0)

    @pl.when((grp == N_GROUPS - 1) & (which == 1))
    def _():
        lam = lam_ref[...]
        lam_full = (jnp.exp(jnp.sum(lam[0:1] * lam[1:2], axis=1, keepdims=True))
                    - jnp.exp(jnp.sum(lam[2:3] * lam[3:4], axis=1, keepdims=True)) + lambda_init)

        def finish(tile, carry):
            rows = _tile_rows(tile)
            o = map0_ref[rows, :] - lam_full * _attention_of(acc_ref[rows, :])
            o = o * lax.rsqrt(jnp.mean(o * o, axis=-1, keepdims=True) + RMS_EPS) * g_ref[...]
            o_ref[rows, :] = (o * (1.0 - lambda_init)).astype(BF16)
            return carry

        lax.fori_loop(0, N_ROW_TILES, finish, 0)


def _diff_attention(qkv, bias, lam, subln_g, layer_idx):
    lambda_init = 0.8 - 0.6 * math.exp(-0.3 * layer_idx)
    return pl.pallas_call(
        functools.partial(_diff_kernel, lambda_init=lambda_init),
        grid=(BATCH, N_HEADS, 2, N_GROUPS),
        in_specs=[
            pl.BlockSpec((SEQ, HEAD_DIM), lambda b, h, c, g: (b, h)),
            pl.BlockSpec((GROUP_KEYS, HEAD_DIM), lambda b, h, c, g: (b * N_GROUPS + g, N_HEADS + h)),
            pl.BlockSpec((GROUP_KEYS, HEAD_DIM), lambda b, h, c, g: (b * N_GROUPS + g, 2 * N_HEADS + h)),
            pl.BlockSpec((1, N_TABLE_TILES, TILE, TILE), lambda b, h, c, g: (h, 0, 0, 0)),
            pl.BlockSpec((4, DIFF_HALF), lambda b, h, c, g: (0, 0)),
            pl.BlockSpec((1, HEAD_DIM), lambda b, h, c, g: (0, 0)),
        ],
        out_specs=pl.BlockSpec((SEQ, HEAD_DIM), lambda b, h, c, g: (b, h)),
        out_shape=jax.ShapeDtypeStruct((ROWS, D_MODEL), BF16),
        scratch_shapes=[
            pltpu.VMEM((SEQ, HEAD_DIM), BF16),
            pltpu.VMEM((SEQ, HEAD_DIM), F32),
        ] + _flash_scratch(),
        compiler_params=pltpu.CompilerParams(
            dimension_semantics=("arbitrary", "arbitrary", "arbitrary", "arbitrary"),
            vmem_limit_bytes=VMEM_LIMIT),
        name="diff_attn",
    )(qkv, qkv, qkv, bias, lam, subln_g.reshape(1, HEAD_DIM))


OUT_TM = 512


def _rms(y, g):
    return y * lax.rsqrt(jnp.mean(y * y, axis=-1, keepdims=True) + RMS_EPS) * g


def _outproj_kernel(o_ref, x_ref, w_ref, g_ref, gate_ref, out_ref, wb_ref):
    _cast_weight_once(w_ref, wb_ref)
    y = jnp.dot(o_ref[...], wb_ref[...], preferred_element_type=F32)
    out_ref[...] = x_ref[...] + gate_ref[0] * _rms(y, g_ref[...])


def _out_proj(o, x2, w_layers, layer, g, gate):
    tiles_per_batch = SEQ // OUT_TM
    return pl.pallas_call(
        _outproj_kernel,
        grid=(ROWS // OUT_TM,),
        in_specs=[
            pl.BlockSpec((OUT_TM, D_MODEL), lambda i: (i, 0)),
            pl.BlockSpec((OUT_TM, D_MODEL), lambda i: (i, 0)),
            pl.BlockSpec((1, D_MODEL, D_MODEL), lambda i: (layer, 0, 0), pipeline_mode=pl.Buffered(1)),
            pl.BlockSpec((1, D_MODEL), lambda i: (0, 0)),
            pl.BlockSpec((1, 1, D_MODEL), lambda i: (i // tiles_per_batch, 0, 0)),
        ],
        out_specs=pl.BlockSpec((OUT_TM, D_MODEL), lambda i: (i, 0)),
        out_shape=jax.ShapeDtypeStruct((ROWS, D_MODEL), F32),
        scratch_shapes=[pltpu.VMEM((D_MODEL, D_MODEL), BF16)],
        compiler_params=pltpu.CompilerParams(
            dimension_semantics=("arbitrary",), vmem_limit_bytes=VMEM_LIMIT),
        name="out_proj",
    )(o, x2, w_layers, g.reshape(1, D_MODEL), gate)


FFN_TM = 1024
FFN_TF = 256


def _ffn_kernel(x_ref, g2_ref, sc_ref, sh_ref, wg_ref, wu_ref, wo_ref, g3_ref, gate_ref, out_ref,
                h_ref, acc_ref):
    c = pl.program_id(1)

    @pl.when(c == 0)
    def _():
        h_ref[...] = _norm_mod(x_ref[...], g2_ref[...], sc_ref[0], sh_ref[0]).astype(BF16)
        acc_ref[...] = jnp.zeros(acc_ref.shape, F32)

    h = h_ref[...]
    gp = jnp.dot(h, wg_ref[0].astype(BF16), preferred_element_type=F32)
    up = jnp.dot(h, wu_ref[0].astype(BF16), preferred_element_type=F32)
    act = (gp / (1.0 + jnp.exp(-gp)) * up).astype(BF16)
    acc_ref[...] += jnp.dot(act, wo_ref[0].astype(BF16), preferred_element_type=F32)

    @pl.when(c == pl.num_programs(1) - 1)
    def _():
        out_ref[...] = x_ref[...] + gate_ref[0] * _rms(acc_ref[...], g3_ref[...])


def _ffn(x2, g2, sc, sh, w_in_layers, w_out_layers, layer, g3, gate):
    tiles_per_batch = SEQ // FFN_TM
    n_chunks = D_FF // FFN_TF
    vec = pl.BlockSpec((1, 1, D_MODEL), lambda i, c: (i // tiles_per_batch, 0, 0))
    gvec = pl.BlockSpec((1, D_MODEL), lambda i, c: (0, 0))
    return pl.pallas_call(
        _ffn_kernel,
        grid=(ROWS // FFN_TM, n_chunks),
        in_specs=[
            pl.BlockSpec((FFN_TM, D_MODEL), lambda i, c: (i, 0)),
            gvec, vec, vec,
            pl.BlockSpec((1, D_MODEL, FFN_TF), lambda i, c: (layer, 0, c)),
            pl.BlockSpec((1, D_MODEL, FFN_TF), lambda i, c: (layer, 0, n_chunks + c)),
            pl.BlockSpec((1, FFN_TF, D_MODEL), lambda i, c: (layer, c, 0)),
            gvec, vec,
        ],
        out_specs=pl.BlockSpec((FFN_TM, D_MODEL), lambda i, c: (i, 0)),
        out_shape=jax.ShapeDtypeStruct((ROWS, D_MODEL), F32),
        scratch_shapes=[
            pltpu.VMEM((FFN_TM, D_MODEL), BF16),
            pltpu.VMEM((FFN_TM, D_MODEL), F32),
        ],
        compiler_params=pltpu.CompilerParams(
            dimension_semantics=("parallel", "arbitrary"), vmem_limit_bytes=VMEM_LIMIT),
        name="ffn",
    )(x2, g2.reshape(1, D_MODEL), sc, sh, w_in_layers, w_in_layers, w_out_layers, g3.reshape(1, D_MODEL), gate)


def kernel(x, c, rel_bias, ada_w, ada_b, norm_g, moba_w_qkv, moba_w_o, diff_w_qkv, diff_w_o, diff_lambda,
           diff_subln_g, ffn_w_in, ffn_w_out):
    x2 = x.reshape(ROWS, D_MODEL)
    mod = _adaln_mod(c, ada_w, ada_b)
    bias = _bias_tiles(rel_bias)
    for i in range(DEPTH):
        sh_a, sc_a, g_a, sh_f, sc_f, g_f = [
            mod[i, :, j * D_MODEL:(j + 1) * D_MODEL].reshape(BATCH, 1, D_MODEL) for j in range(6)]
        if i % 2 == 0:
            qkv, kmean = _qkv_proj(x2, norm_g[i, 0], sc_a, sh_a, moba_w_qkv, i // 2,
                                   HEAD_DIM ** -0.5 * LOG2_E, True)
            o = _moba_attention(qkv, kmean.reshape(BATCH * N_KV_BLOCKS, D_MODEL), bias)
            w_o = moba_w_o
        else:
            (qkv,) = _qkv_proj(x2, norm_g[i, 0], sc_a, sh_a, diff_w_qkv, i // 2,
                               DIFF_HALF ** -0.5 * LOG2_E, False)
            o = _diff_attention(qkv, bias, diff_lambda[i // 2], diff_subln_g[i // 2], i)
            w_o = diff_w_o
        x2 = _out_proj(o, x2, w_o, i // 2, norm_g[i, 1], g_a)
        x2 = _ffn(x2, norm_g[i, 2], sc_f, sh_f, ffn_w_in, ffn_w_out, i, norm_g[i, 3], g_f)
    return x2.reshape(BATCH, SEQ, D_MODEL)
```

```python
import functools
import math

import numpy as np
import jax
import jax.numpy as jnp
from jax import lax
from jax.experimental import pallas as pl
from jax.experimental.pallas import tpu as pltpu

D_MODEL = 1024
BATCH = 2
SEQ = 8192
DEPTH = 2
N_HEADS = 8
HEAD_DIM = D_MODEL // N_HEADS
DIFF_HALF = HEAD_DIM // 2
MOBA_BLOCK = 256
MOBA_TOPK = 3
D_FF = 2816
N_BUCKETS = 32
MAX_EXACT = N_BUCKETS // 2
MAX_DISTANCE = 2048
RMS_EPS = 1e-6
NEG_INF = -1e30

ROWS = BATCH * SEQ
N_KV_BLOCKS = SEQ // MOBA_BLOCK
TILE = MOBA_BLOCK
VMEM_LIMIT = 48 * 1024 * 1024

F32 = jnp.float32
BF16 = jnp.bfloat16


def _bucket_of_distance():
    n = np.arange(MAX_DISTANCE + 1)
    nf = np.maximum(n, 1).astype(np.float64)
    val = np.log(nf / MAX_EXACT) / math.log(MAX_DISTANCE / MAX_EXACT) * (N_BUCKETS - MAX_EXACT)
    frac = np.abs(val - np.round(val))
    assert np.all((frac > 5e-5) | (n <= MAX_EXACT) | (n == MAX_DISTANCE))
    large = np.minimum(MAX_EXACT + np.floor(val + 1e-9).astype(np.int64), N_BUCKETS - 1)
    return np.where(n < MAX_EXACT, n, large)


_BUCKETS = _bucket_of_distance()
FAR_DISTANCE = int(np.min(np.nonzero(_BUCKETS == N_BUCKETS - 1)[0]))
N_BIAS_TILES = (FAR_DISTANCE + TILE - 1) // TILE + 1
N_TABLE_TILES = N_BIAS_TILES + 2


def _bias_bucket_tiles():
    i = np.arange(TILE)[:, None]
    j = np.arange(TILE)[None, :]
    out = []
    for t in range(-1, N_BIAS_TILES + 1):
        rel = t * TILE + i - j
        out.append(np.where(rel >= 0, _BUCKETS[np.clip(rel, 0, MAX_DISTANCE)], -1))
    return np.stack(out).astype(np.int32)


def _table_index(tiles_behind):
    return jnp.clip(tiles_behind, -1, N_BIAS_TILES) + 1


MOD_TN = 512


def _mod_kernel(ct_ref, w_ref, b_ref, o_ref):
    ct = ct_ref[...]
    cact = ct / (1.0 + jnp.exp(-ct))
    w = w_ref[0]
    for b in range(BATCH):
        row = jnp.sum(w * cact[:, b:b + 1], axis=0, keepdims=True)
        o_ref[0, b:b + 1, :] = row + b_ref[0]


def _adaln_mod(c, ada_w, ada_b):
    n_out = ada_w.shape[-1]
    return pl.pallas_call(
        _mod_kernel,
        grid=(DEPTH, n_out // MOD_TN),
        in_specs=[
            pl.BlockSpec((D_MODEL, BATCH), lambda i, n: (0, 0)),
            pl.BlockSpec((1, D_MODEL, MOD_TN), lambda i, n: (i, 0, n)),
            pl.BlockSpec((1, 1, MOD_TN), lambda i, n: (i, 0, n)),
        ],
        out_specs=pl.BlockSpec((1, BATCH, MOD_TN), lambda i, n: (i, 0, n)),
        out_shape=jax.ShapeDtypeStruct((DEPTH, BATCH, n_out), F32),
        name="adaln_mod",
    )(c.T, ada_w, ada_b.reshape(DEPTH, 1, n_out))


LOG2_E = math.log2(math.e)


def _bias_kernel(tab_ref, bkt_ref, o_ref, *, buckets_present):
    h = pl.program_id(0)
    far = tab_ref[N_BUCKETS - 1, h]
    for t, present in enumerate(buckets_present):
        bkt = bkt_ref[t]
        acc = jnp.full(bkt.shape, NEG_INF, F32)
        for b in present:
            acc = jnp.where(bkt == b, (tab_ref[b, h] - far) * LOG2_E, acc)
        o_ref[0, t] = acc


def _bias_tiles(rel_bias):
    bkt = _bias_bucket_tiles()
    buckets_present = tuple(tuple(int(b) for b in np.unique(tile) if b >= 0) for tile in bkt)
    return pl.pallas_call(
        functools.partial(_bias_kernel, buckets_present=buckets_present),
        grid=(N_HEADS,),
        in_specs=[
            pl.BlockSpec(memory_space=pltpu.SMEM),
            pl.BlockSpec((N_TABLE_TILES, TILE, TILE), lambda h: (0, 0, 0)),
        ],
        out_specs=pl.BlockSpec((1, N_TABLE_TILES, TILE, TILE), lambda h: (h, 0, 0, 0)),
        out_shape=jax.ShapeDtypeStruct((N_HEADS, N_TABLE_TILES, TILE, TILE), F32),
        name="bias_tiles",
    )(rel_bias, jnp.asarray(bkt))


QKV_TM = 512


def _norm_mod(x, g, sc, sh):
    y = x * lax.rsqrt(jnp.mean(x * x, axis=-1, keepdims=True) + RMS_EPS)
    return (y * g) * (1.0 + sc) + sh


def _cast_weight_once(w_ref, wb_ref):
    @pl.when(pl.program_id(0) == 0)
    def _():
        wb_ref[...] = w_ref[0].astype(BF16)


def _qkv_kernel(x_ref, g_ref, sc_ref, sh_ref, w_ref, o_ref, *rest, q_scale):
    *km_ref, wb_ref = rest
    _cast_weight_once(w_ref, wb_ref)
    hb = _norm_mod(x_ref[...], g_ref[...], sc_ref[0], sh_ref[0]).astype(BF16)
    for n in range(3):
        r = jnp.dot(hb, wb_ref[:, n * D_MODEL:(n + 1) * D_MODEL], preferred_element_type=F32)
        if n == 0:
            r = r * q_scale
        if n == 1 and km_ref:
            km_ref[0][0] = jnp.mean(r.reshape(QKV_TM // MOBA_BLOCK, MOBA_BLOCK, D_MODEL), axis=1)
        o_ref[:, n * D_MODEL:(n + 1) * D_MODEL] = r.astype(BF16)


def _qkv_proj(x2, g, sc, sh, w_layers, layer, q_scale, with_kmean):
    tiles_per_batch = SEQ // QKV_TM
    n_tiles = ROWS // QKV_TM
    vec = pl.BlockSpec((1, 1, D_MODEL), lambda i: (i // tiles_per_batch, 0, 0))
    out_shape = [jax.ShapeDtypeStruct((ROWS, 3 * D_MODEL), BF16)]
    out_specs = [pl.BlockSpec((QKV_TM, 3 * D_MODEL), lambda i: (i, 0))]
    if with_kmean:
        per_tile = QKV_TM // MOBA_BLOCK
        out_shape.append(jax.ShapeDtypeStruct((n_tiles, per_tile, D_MODEL), F32))
        out_specs.append(pl.BlockSpec((1, per_tile, D_MODEL), lambda i: (i, 0, 0)))
    return pl.pallas_call(
        functools.partial(_qkv_kernel, q_scale=q_scale),
        grid=(n_tiles,),
        in_specs=[
            pl.BlockSpec((QKV_TM, D_MODEL), lambda i: (i, 0)),
            pl.BlockSpec((1, D_MODEL), lambda i: (0, 0)),
            vec, vec,
            pl.BlockSpec((1, D_MODEL, 3 * D_MODEL), lambda i: (layer, 0, 0), pipeline_mode=pl.Buffered(1)),
        ],
        out_specs=out_specs,
        out_shape=out_shape,
        scratch_shapes=[pltpu.VMEM((D_MODEL, 3 * D_MODEL), BF16)],
        compiler_params=pltpu.CompilerParams(
            dimension_semantics=("arbitrary",), vmem_limit_bytes=VMEM_LIMIT),
        name="qkv_moba" if with_kmean else "qkv_diff",
    )(x2, g.reshape(1, D_MODEL), sc, sh, w_layers)


_NT = (((1,), (1,)), ((), ()))
GROUP = 4
GROUP_KEYS = GROUP * TILE
N_GROUPS = SEQ // GROUP_KEYS
GROUPS_PER_STEP = 4
N_STEPS = N_GROUPS // GROUPS_PER_STEP
STEP_KEYS = GROUPS_PER_STEP * GROUP_KEYS
ROW_TILE = 2 * TILE
N_ROW_TILES = SEQ // ROW_TILE
ROW_TILES_PER_GROUP = GROUP_KEYS // ROW_TILE
SOFTMAX_ROWS = 64
BIAS_FREE_FROM_ROW_TILE = -(-(N_BIAS_TILES + GROUP - 1) // (ROW_TILE // TILE))


def _flash_scratch():
    scores = pltpu.VMEM((ROW_TILE, GROUP_KEYS), F32)
    probs = pltpu.VMEM((ROW_TILE, GROUP_KEYS), BF16)
    rescale = pltpu.VMEM((ROW_TILE, 1), F32)
    values = pltpu.VMEM((2, GROUP_KEYS, 2 * HEAD_DIM), BF16)
    running_max = pltpu.VMEM((SEQ, 1), F32)
    acc = pltpu.VMEM((SEQ, 2 * HEAD_DIM), F32)
    return [scores, scores, probs, probs, rescale, rescale, values, running_max, acc]


def _tile_rows(tile):
    return pl.ds(pl.multiple_of(tile * ROW_TILE, ROW_TILE), ROW_TILE)


def _flash_sweep(grp, qa_ref, group_keys, group_values, bias_ref,
                 s0_ref, s1_ref, p0_ref, p1_ref, alpha0_ref, alpha1_ref, va_ref, m_ref, acc_ref):
    half = GROUP_KEYS // 2
    first = grp * ROW_TILES_PER_GROUP
    last = N_ROW_TILES - 1
    va_new = va_ref.at[grp % 2]
    va_old = va_ref.at[1 - grp % 2]
    va_new[:, :HEAD_DIM] = group_values()
    va_new[:, HEAD_DIM:] = jnp.ones((GROUP_KEYS, HEAD_DIM), BF16)

    def scores(tile, s_ref, n_keys=GROUP_KEYS, biased=True):
        raw = lax.dot_general(qa_ref[_tile_rows(tile), :], group_keys(n_keys), _NT,
                              preferred_element_type=F32)
        if not biased:
            s_ref[:, :n_keys] = raw
            return
        for r in range(ROW_TILE // TILE):
            for j in range(n_keys // TILE):
                bias = bias_ref[0, _table_index(tile * (ROW_TILE // TILE) + r - (grp * GROUP + j))]
                s_ref[r * TILE:(r + 1) * TILE, j * TILE:(j + 1) * TILE] = (
                    raw[r * TILE:(r + 1) * TILE, j * TILE:(j + 1) * TILE] + bias)

    def softmax(tile, s_ref, p_ref, alpha_ref, n_keys=GROUP_KEYS):
        for c in range(ROW_TILE // SOFTMAX_ROWS):
            chunk = slice(c * SOFTMAX_ROWS, (c + 1) * SOFTMAX_ROWS)
            rows = pl.ds(pl.multiple_of(tile * ROW_TILE + c * SOFTMAX_ROWS, SOFTMAX_ROWS), SOFTMAX_ROWS)
            s = s_ref[chunk, :n_keys]
            m_prev = m_ref[rows, :]
            m_new = jnp.maximum(m_prev, jnp.max(s, axis=1, keepdims=True))
            alpha_ref[chunk, :] = jnp.exp2(m_prev - m_new)
            m_ref[rows, :] = m_new
            p_ref[chunk, :n_keys] = jnp.exp2(s - m_new).astype(BF16)

    def accumulate(tile, p_ref, alpha_ref, va, n_keys=GROUP_KEYS):
        rows = _tile_rows(tile)
        acc_ref[rows, :] = (alpha_ref[...] * acc_ref[rows, :]
                            + jnp.dot(p_ref[:, :n_keys], va[:n_keys, :], preferred_element_type=F32))

    def enter(with_pending):
        scores(first, s0_ref, half)
        if with_pending:
            softmax(last, s1_ref, p1_ref, alpha1_ref)
            accumulate(last - 1, p0_ref, alpha0_ref, va_old)
        scores(first + 1, s1_ref)
        softmax(first, s0_ref, p0_ref, alpha0_ref, half)
        if with_pending:
            accumulate(last, p1_ref, alpha1_ref, va_old)

    def second_pair():
        scores(first + 2, s0_ref)
        softmax(first + 1, s1_ref, p1_ref, alpha1_ref)
        accumulate(first, p0_ref, alpha0_ref, va_new, half)
        scores(first + 3, s1_ref)
        softmax(first + 2, s0_ref, p0_ref, alpha0_ref)
        accumulate(first + 1, p1_ref, alpha1_ref, va_new)

    def pair(t, first_biased=False):
        tile = first + 3 + 2 * t
        scores(tile + 1, s0_ref, biased=first_biased)
        softmax(tile, s1_ref, p1_ref, alpha1_ref)
        accumulate(tile - 1, p0_ref, alpha0_ref, va_new)
        scores(tile + 2, s1_ref, biased=False)
        softmax(tile + 1, s0_ref, p0_ref, alpha0_ref)
        accumulate(tile, p1_ref, alpha1_ref, va_new)

    @pl.when(grp == 0)
    def _():
        enter(False)
        second_pair()

    @pl.when((grp > 0) & (grp < N_GROUPS - 1))
    def _():
        enter(True)
        second_pair()

    @pl.when(grp == N_GROUPS - 1)
    def _():
        enter(True)
        softmax(last, s1_ref, p1_ref, alpha1_ref)
        accumulate(last - 1, p0_ref, alpha0_ref, va_new, half)
        accumulate(last, p1_ref, alpha1_ref, va_new)

    assert BIAS_FREE_FROM_ROW_TILE == 5
    n_pairs = jnp.maximum(last - first - 3, 0) // 2

    @pl.when(n_pairs > 0)
    def _():
        pair(0, first_biased=True)

    @pl.when((n_pairs > 0) & (n_pairs % 2 == 0))
    def _():
        pair(1)

    def two_pairs(u, carry):
        t = 2 - n_pairs % 2 + 2 * u
        pair(t)
        pair(t + 1)
        return carry

    lax.fori_loop(0, jnp.maximum(n_pairs - 1, 0) // 2, two_pairs, 0)


def _reset_state(rows, m_ref, acc_ref):
    m_ref[rows, :] = jnp.full((rows.size, 1), NEG_INF, F32)
    acc_ref[rows, :] = jnp.zeros((rows.size, 2 * HEAD_DIM), F32)


def _attention_of(acc):
    return acc[:, :HEAD_DIM] / acc[:, HEAD_DIM:]


GATE_QUERIES = 8 * TILE


def _moba_gating(q_ref, km_ref, qa_ref, m_ref, acc_ref):
    blk = lax.broadcasted_iota(jnp.int32, (N_KV_BLOCKS, GATE_QUERIES), 0)
    blkf = blk.astype(F32)
    tile_in_step = lax.broadcasted_iota(jnp.int32, (N_KV_BLOCKS, GATE_QUERIES), 1) // TILE

    def gate_step(step, carry):
        rows = pl.ds(pl.multiple_of(step * GATE_QUERIES, GATE_QUERIES), GATE_QUERIES)
        _reset_state(rows, m_ref, acc_ref)
        q = q_ref[rows, :]
        gate = lax.dot_general(km_ref[...].astype(BF16), q, _NT, preferred_element_type=F32)
        tile = step * (GATE_QUERIES // TILE) + tile_in_step
        past = blk < tile
        g = jnp.where(past, gate, NEG_INF)
        madd = jnp.full(gate.shape, NEG_INF, F32)
        for _ in range(MOBA_TOPK):
            mx = jnp.max(g, axis=0, keepdims=True)
            first = jnp.min(jnp.where(g == mx, blkf, float(N_KV_BLOCKS)), axis=0, keepdims=True)
            hit = blkf == first
            madd = jnp.where(hit & past, 0.0, madd)
            g = jnp.where(hit, -jnp.inf, g)
        madd = jnp.where(blk == tile, 0.0, madd)
        madd = jnp.concatenate([madd, jnp.zeros((HEAD_DIM - N_KV_BLOCKS, GATE_QUERIES), F32)], axis=0)
        qa_ref[rows, :HEAD_DIM] = q
        qa_ref[rows, HEAD_DIM:] = madd.T.astype(BF16)
        return carry

    lax.fori_loop(0, SEQ // GATE_QUERIES, gate_step, 0)


def _step_rows(group_in_step, n_rows=GROUP_KEYS):
    return pl.ds(pl.multiple_of(group_in_step * GROUP_KEYS, GROUP_KEYS), n_rows)


def _moba_kernel(q_ref, k_ref, v_ref, km_ref, bias_ref, o_ref, qa_ref, ka_ref, *flash_refs):
    step = pl.program_id(2)
    m_ref, acc_ref = flash_refs[-2:]

    @pl.when(step == 0)
    def _():
        _moba_gating(q_ref, km_ref, qa_ref, m_ref, acc_ref)

    lane = lax.broadcasted_iota(jnp.int32, (GROUP_KEYS, HEAD_DIM), 1)
    block_in_group = lax.broadcasted_iota(jnp.int32, (GROUP_KEYS, HEAD_DIM), 0) // TILE

    def group(i, carry):
        grp = step * GROUPS_PER_STEP + i
        ka_ref[:, :HEAD_DIM] = k_ref[_step_rows(i), :]
        ka_ref[:, HEAD_DIM:] = jnp.where(lane == grp * GROUP + block_in_group, 1.0, 0.0).astype(BF16)
        _flash_sweep(grp, qa_ref, lambda n: ka_ref[:n, :], lambda: v_ref[_step_rows(i), :], bias_ref,
                     *flash_refs)
        return carry

    lax.fori_loop(0, GROUPS_PER_STEP, group, 0)

    @pl.when(step == N_STEPS - 1)
    def _():
        def finish(tile, carry):
            rows = _tile_rows(tile)
            o_ref[rows, :] = _attention_of(acc_ref[rows, :]).astype(BF16)
            return carry

        lax.fori_loop(0, N_ROW_TILES, finish, 0)


def _moba_attention(qkv, kmean, bias):
    return pl.pallas_call(
        _moba_kernel,
        grid=(BATCH, N_HEADS, N_STEPS),
        in_specs=[
            pl.BlockSpec((SEQ, HEAD_DIM), lambda b, h, g: (b, h)),
            pl.BlockSpec((STEP_KEYS, HEAD_DIM), lambda b, h, g: (b * N_STEPS + g, N_HEADS + h)),
            pl.BlockSpec((STEP_KEYS, HEAD_DIM), lambda b, h, g: (b * N_STEPS + g, 2 * N_HEADS + h)),
            pl.BlockSpec((N_KV_BLOCKS, HEAD_DIM), lambda b, h, g: (b, h)),
            pl.BlockSpec((1, N_TABLE_TILES, TILE, TILE), lambda b, h, g: (h, 0, 0, 0)),
        ],
        out_specs=pl.BlockSpec((SEQ, HEAD_DIM), lambda b, h, g: (b, h)),
        out_shape=jax.ShapeDtypeStruct((ROWS, D_MODEL), BF16),
        scratch_shapes=[
            pltpu.VMEM((SEQ, 2 * HEAD_DIM), BF16),
            pltpu.VMEM((GROUP_KEYS, 2 * HEAD_DIM), BF16),
        ] + _flash_scratch(),
        compiler_params=pltpu.CompilerParams(
            dimension_semantics=("arbitrary", "arbitrary", "arbitrary"), vmem_limit_bytes=VMEM_LIMIT),
        name="moba_attn",
    )(qkv, qkv, qkv, kmean, bias)


def _diff_kernel(q_ref, k_ref, v_ref, bias_ref, lam_ref, g_ref, o_ref, qa_ref, map0_ref, *flash_refs,
                 lambda_init):
    which = pl.program_id(2)
    step = pl.program_id(3)
    m_ref, acc_ref = flash_refs[-2:]

    @pl.when(step == 0)
    def _():
        lane_map = lax.broadcasted_iota(jnp.int32, (ROW_TILE, HEAD_DIM), 1) // DIFF_HALF

        def split(tile, carry):
            rows = _tile_rows(tile)
            _reset_state(rows, m_ref, acc_ref)
            q = q_ref[rows, :]
            qa_ref[rows, :] = jnp.where(lane_map == which, q, jnp.zeros_like(q))
            return carry

        lax.fori_loop(0, N_ROW_TILES, split, 0)

    def group(i, carry):
        _flash_sweep(step * GROUPS_PER_STEP + i, qa_ref, lambda n: k_ref[_step_rows(i, n), :],
                     lambda: v_ref[_step_rows(i), :], bias_ref, *flash_refs)
        return carry

    lax.fori_loop(0, GROUPS_PER_STEP, group, 0)

    @pl.when((step == N_STEPS - 1) & (which == 0))
    def _():
        def keep(tile, carry):
            rows = _tile_rows(tile)
            map0_ref[rows, :] = _attention_of(acc_ref[rows, :])
            return carry

        lax.fori_loop(0, N_ROW_TILES, keep, 0)

    @pl.when((step == N_STEPS - 1) & (which == 1))
    def _():
        lam = lam_ref[...]
        lam_full = (jnp.exp(jnp.sum(lam[0:1] * lam[1:2], axis=1, keepdims=True))
                    - jnp.exp(jnp.sum(lam[2:3] * lam[3:4], axis=1, keepdims=True)) + lambda_init)

        def finish(tile, carry):
            rows = _tile_rows(tile)
            o = map0_ref[rows, :] - lam_full * _attention_of(acc_ref[rows, :])
            o = o * lax.rsqrt(jnp.mean(o * o, axis=-1, keepdims=True) + RMS_EPS) * g_ref[...]
            o_ref[rows, :] = (o * (1.0 - lambda_init)).astype(BF16)
            return carry

        lax.fori_loop(0, N_ROW_TILES, finish, 0)


def _diff_attention(qkv, bias, lam, subln_g, layer_idx):
    lambda_init = 0.8 - 0.6 * math.exp(-0.3 * layer_idx)
    return pl.pallas_call(
        functools.partial(_diff_kernel, lambda_init=lambda_init),
        grid=(BATCH, N_HEADS, 2, N_STEPS),
        in_specs=[
            pl.BlockSpec((SEQ, HEAD_DIM), lambda b, h, c, g: (b, h)),
            pl.BlockSpec((STEP_KEYS, HEAD_DIM), lambda b, h, c, g: (b * N_STEPS + g, N_HEADS + h)),
            pl.BlockSpec((STEP_KEYS, HEAD_DIM), lambda b, h, c, g: (b * N_STEPS + g, 2 * N_HEADS + h)),
            pl.BlockSpec((1, N_TABLE_TILES, TILE, TILE), lambda b, h, c, g: (h, 0, 0, 0)),
            pl.BlockSpec((4, DIFF_HALF), lambda b, h, c, g: (0, 0)),
            pl.BlockSpec((1, HEAD_DIM), lambda b, h, c, g: (0, 0)),
        ],
        out_specs=pl.BlockSpec((SEQ, HEAD_DIM), lambda b, h, c, g: (b, h)),
        out_shape=jax.ShapeDtypeStruct((ROWS, D_MODEL), BF16),
        scratch_shapes=[
            pltpu.VMEM((SEQ, HEAD_DIM), BF16),
            pltpu.VMEM((SEQ, HEAD_DIM), F32),
        ] + _flash_scratch(),
        compiler_params=pltpu.CompilerParams(
            dimension_semantics=("arbitrary", "arbitrary", "arbitrary", "arbitrary"),
            vmem_limit_bytes=VMEM_LIMIT),
        name="diff_attn",
    )(qkv, qkv, qkv, bias, lam, subln_g.reshape(1, HEAD_DIM))


OUT_TM = 512


def _rms(y, g):
    return y * lax.rsqrt(jnp.mean(y * y, axis=-1, keepdims=True) + RMS_EPS) * g


def _outproj_kernel(o_ref, x_ref, w_ref, g_ref, gate_ref, out_ref, wb_ref):
    _cast_weight_once(w_ref, wb_ref)
    y = jnp.dot(o_ref[...], wb_ref[...], preferred_element_type=F32)
    out_ref[...] = x_ref[...] + gate_ref[0] * _rms(y, g_ref[...])


def _out_proj(o, x2, w_layers, layer, g, gate):
    tiles_per_batch = SEQ // OUT_TM
    return pl.pallas_call(
        _outproj_kernel,
        grid=(ROWS // OUT_TM,),
        in_specs=[
            pl.BlockSpec((OUT_TM, D_MODEL), lambda i: (i, 0)),
            pl.BlockSpec((OUT_TM, D_MODEL), lambda i: (i, 0)),
            pl.BlockSpec((1, D_MODEL, D_MODEL), lambda i: (layer, 0, 0), pipeline_mode=pl.Buffered(1)),
            pl.BlockSpec((1, D_MODEL), lambda i: (0, 0)),
            pl.BlockSpec((1, 1, D_MODEL), lambda i: (i // tiles_per_batch, 0, 0)),
        ],
        out_specs=pl.BlockSpec((OUT_TM, D_MODEL), lambda i: (i, 0)),
        out_shape=jax.ShapeDtypeStruct((ROWS, D_MODEL), F32),
        scratch_shapes=[pltpu.VMEM((D_MODEL, D_MODEL), BF16)],
        compiler_params=pltpu.CompilerParams(
            dimension_semantics=("arbitrary",), vmem_limit_bytes=VMEM_LIMIT),
        name="out_proj",
    )(o, x2, w_layers, g.reshape(1, D_MODEL), gate)


FFN_TM = 1024
FFN_TF = 256


def _ffn_kernel(x_ref, g2_ref, sc_ref, sh_ref, wg_ref, wu_ref, wo_ref, g3_ref, gate_ref, out_ref,
                h_ref, acc_ref):
    c = pl.program_id(1)

    @pl.when(c == 0)
    def _():
        h_ref[...] = _norm_mod(x_ref[...], g2_ref[...], sc_ref[0], sh_ref[0]).astype(BF16)
        acc_ref[...] = jnp.zeros(acc_ref.shape, F32)

    h = h_ref[...]
    gp = jnp.dot(h, wg_ref[0].astype(BF16), preferred_element_type=F32)
    up = jnp.dot(h, wu_ref[0].astype(BF16), preferred_element_type=F32)
    act = (gp / (1.0 + jnp.exp(-gp)) * up).astype(BF16)
    acc_ref[...] += jnp.dot(act, wo_ref[0].astype(BF16), preferred_element_type=F32)

    @pl.when(c == pl.num_programs(1) - 1)
    def _():
        out_ref[...] = x_ref[...] + gate_ref[0] * _rms(acc_ref[...], g3_ref[...])


def _ffn(x2, g2, sc, sh, w_in_layers, w_out_layers, layer, g3, gate):
    tiles_per_batch = SEQ // FFN_TM
    n_chunks = D_FF // FFN_TF
    vec = pl.BlockSpec((1, 1, D_MODEL), lambda i, c: (i // tiles_per_batch, 0, 0))
    gvec = pl.BlockSpec((1, D_MODEL), lambda i, c: (0, 0))
    return pl.pallas_call(
        _ffn_kernel,
        grid=(ROWS // FFN_TM, n_chunks),
        in_specs=[
            pl.BlockSpec((FFN_TM, D_MODEL), lambda i, c: (i, 0)),
            gvec, vec, vec,
            pl.BlockSpec((1, D_MODEL, FFN_TF), lambda i, c: (layer, 0, c)),
            pl.BlockSpec((1, D_MODEL, FFN_TF), lambda i, c: (layer, 0, n_chunks + c)),
            pl.BlockSpec((1, FFN_TF, D_MODEL), lambda i, c: (layer, c, 0)),
            gvec, vec,
        ],
        out_specs=pl.BlockSpec((FFN_TM, D_MODEL), lambda i, c: (i, 0)),
        out_shape=jax.ShapeDtypeStruct((ROWS, D_MODEL), F32),
        scratch_shapes=[
            pltpu.VMEM((FFN_TM, D_MODEL), BF16),
            pltpu.VMEM((FFN_TM, D_MODEL), F32),
        ],
        compiler_params=pltpu.CompilerParams(
            dimension_semantics=("parallel", "arbitrary"), vmem_limit_bytes=VMEM_LIMIT),
        name="ffn",
    )(x2, g2.reshape(1, D_MODEL), sc, sh, w_in_layers, w_in_layers, w_out_layers, g3.reshape(1, D_MODEL), gate)


def kernel(x, c, rel_bias, ada_w, ada_b, norm_g, moba_w_qkv, moba_w_o, diff_w_qkv, diff_w_o, diff_lambda,
           diff_subln_g, ffn_w_in, ffn_w_out):
    x2 = x.reshape(ROWS, D_MODEL)
    mod = _adaln_mod(c, ada_w, ada_b)
    bias = _bias_tiles(rel_bias)
    for i in range(DEPTH):
        sh_a, sc_a, g_a, sh_f, sc_f, g_f = [
            mod[i, :, j * D_MODEL:(j + 1) * D_MODEL].reshape(BATCH, 1, D_MODEL) for j in range(6)]
        if i % 2 == 0:
            qkv, kmean = _qkv_proj(x2, norm_g[i, 0], sc_a, sh_a, moba_w_qkv, i // 2,
                                   HEAD_DIM ** -0.5 * LOG2_E, True)
            o = _moba_attention(qkv, kmean.reshape(BATCH * N_KV_BLOCKS, D_MODEL), bias)
            w_o = moba_w_o
        else:
            (qkv,) = _qkv_proj(x2, norm_g[i, 0], sc_a, sh_a, diff_w_qkv, i // 2,
                               DIFF_HALF ** -0.5 * LOG2_E, False)
            o = _diff_attention(qkv, bias, diff_lambda[i // 2], diff_subln_g[i // 2], i)
            w_o = diff_w_o
        x2 = _out_proj(o, x2, w_o, i // 2, norm_g[i, 1], g_a)
        x2 = _ffn(x2, norm_g[i, 2], sc_f, sh_f, ffn_w_in, ffn_w_out, i, norm_g[i, 3], g_f)
    return x2.reshape(BATCH, SEQ, D_MODEL)
```

```python
import functools
import math

import numpy as np
import jax
import jax.numpy as jnp
from jax import lax
from jax.experimental import pallas as pl
from jax.experimental.pallas import tpu as pltpu

D_MODEL = 1024
BATCH = 2
SEQ = 8192
DEPTH = 2
N_HEADS = 8
HEAD_DIM = D_MODEL // N_HEADS
DIFF_HALF = HEAD_DIM // 2
MOBA_BLOCK = 256
MOBA_TOPK = 3
D_FF = 2816
N_BUCKETS = 32
MAX_EXACT = N_BUCKETS // 2
MAX_DISTANCE = 2048
RMS_EPS = 1e-6
NEG_INF = -1e30

ROWS = BATCH * SEQ
N_KV_BLOCKS = SEQ // MOBA_BLOCK
TILE = MOBA_BLOCK
VMEM_LIMIT = 48 * 1024 * 1024

F32 = jnp.float32
BF16 = jnp.bfloat16


def _bucket_of_distance():
    n = np.arange(MAX_DISTANCE + 1)
    nf = np.maximum(n, 1).astype(np.float64)
    val = np.log(nf / MAX_EXACT) / math.log(MAX_DISTANCE / MAX_EXACT) * (N_BUCKETS - MAX_EXACT)
    frac = np.abs(val - np.round(val))
    assert np.all((frac > 5e-5) | (n <= MAX_EXACT) | (n == MAX_DISTANCE))
    large = np.minimum(MAX_EXACT + np.floor(val + 1e-9).astype(np.int64), N_BUCKETS - 1)
    return np.where(n < MAX_EXACT, n, large)


_BUCKETS = _bucket_of_distance()
FAR_DISTANCE = int(np.min(np.nonzero(_BUCKETS == N_BUCKETS - 1)[0]))
N_BIAS_TILES = (FAR_DISTANCE + TILE - 1) // TILE + 1
N_TABLE_TILES = N_BIAS_TILES + 2


def _bias_bucket_tiles():
    i = np.arange(TILE)[:, None]
    j = np.arange(TILE)[None, :]
    out = []
    for t in range(-1, N_BIAS_TILES + 1):
        rel = t * TILE + i - j
        out.append(np.where(rel >= 0, _BUCKETS[np.clip(rel, 0, MAX_DISTANCE)], -1))
    return np.stack(out).astype(np.int32)


def _table_index(tiles_behind):
    return jnp.clip(tiles_behind, -1, N_BIAS_TILES) + 1


MOD_TN = 512


def _mod_kernel(ct_ref, w_ref, b_ref, o_ref):
    ct = ct_ref[...]
    cact = ct / (1.0 + jnp.exp(-ct))
    w = w_ref[0]
    for b in range(BATCH):
        row = jnp.sum(w * cact[:, b:b + 1], axis=0, keepdims=True)
        o_ref[0, b:b + 1, :] = row + b_ref[0]


def _adaln_mod(c, ada_w, ada_b):
    n_out = ada_w.shape[-1]
    return pl.pallas_call(
        _mod_kernel,
        grid=(DEPTH, n_out // MOD_TN),
        in_specs=[
            pl.BlockSpec((D_MODEL, BATCH), lambda i, n: (0, 0)),
            pl.BlockSpec((1, D_MODEL, MOD_TN), lambda i, n: (i, 0, n)),
            pl.BlockSpec((1, 1, MOD_TN), lambda i, n: (i, 0, n)),
        ],
        out_specs=pl.BlockSpec((1, BATCH, MOD_TN), lambda i, n: (i, 0, n)),
        out_shape=jax.ShapeDtypeStruct((DEPTH, BATCH, n_out), F32),
        name="adaln_mod",
    )(c.T, ada_w, ada_b.reshape(DEPTH, 1, n_out))


LOG2_E = math.log2(math.e)


def _bias_kernel(tab_ref, bkt_ref, o_ref, *, buckets_present):
    h = pl.program_id(0)
    far = tab_ref[N_BUCKETS - 1, h]
    for t, present in enumerate(buckets_present):
        bkt = bkt_ref[t]
        acc = jnp.full(bkt.shape, NEG_INF, F32)
        for b in present:
            acc = jnp.where(bkt == b, (tab_ref[b, h] - far) * LOG2_E, acc)
        o_ref[0, t] = acc


def _bias_tiles(rel_bias):
    bkt = _bias_bucket_tiles()
    buckets_present = tuple(tuple(int(b) for b in np.unique(tile) if b >= 0) for tile in bkt)
    return pl.pallas_call(
        functools.partial(_bias_kernel, buckets_present=buckets_present),
        grid=(N_HEADS,),
        in_specs=[
            pl.BlockSpec(memory_space=pltpu.SMEM),
            pl.BlockSpec((N_TABLE_TILES, TILE, TILE), lambda h: (0, 0, 0)),
        ],
        out_specs=pl.BlockSpec((1, N_TABLE_TILES, TILE, TILE), lambda h: (h, 0, 0, 0)),
        out_shape=jax.ShapeDtypeStruct((N_HEADS, N_TABLE_TILES, TILE, TILE), F32),
        name="bias_tiles",
    )(rel_bias, jnp.asarray(bkt))


QKV_TM = 512


def _norm_mod(x, g, sc, sh):
    y = x * lax.rsqrt(jnp.mean(x * x, axis=-1, keepdims=True) + RMS_EPS)
    return (y * g) * (1.0 + sc) + sh


def _cast_weight_once(w_ref, wb_ref):
    @pl.when(pl.program_id(0) == 0)
    def _():
        wb_ref[...] = w_ref[0].astype(BF16)


def _qkv_kernel(x_ref, g_ref, sc_ref, sh_ref, w_ref, o_ref, *rest, q_scale):
    *km_ref, wb_ref = rest
    _cast_weight_once(w_ref, wb_ref)
    hb = _norm_mod(x_ref[...], g_ref[...], sc_ref[0], sh_ref[0]).astype(BF16)
    for n in range(3):
        r = jnp.dot(hb, wb_ref[:, n * D_MODEL:(n + 1) * D_MODEL], preferred_element_type=F32)
        if n == 0:
            r = r * q_scale
        if n == 1 and km_ref:
            km_ref[0][0] = jnp.mean(r.reshape(QKV_TM // MOBA_BLOCK, MOBA_BLOCK, D_MODEL), axis=1)
        o_ref[:, n * D_MODEL:(n + 1) * D_MODEL] = r.astype(BF16)


def _qkv_proj(x2, g, sc, sh, w_layers, layer, q_scale, with_kmean):
    tiles_per_batch = SEQ // QKV_TM
    n_tiles = ROWS // QKV_TM
    vec = pl.BlockSpec((1, 1, D_MODEL), lambda i: (i // tiles_per_batch, 0, 0))
    out_shape = [jax.ShapeDtypeStruct((ROWS, 3 * D_MODEL), BF16)]
    out_specs = [pl.BlockSpec((QKV_TM, 3 * D_MODEL), lambda i: (i, 0))]
    if with_kmean:
        per_tile = QKV_TM // MOBA_BLOCK
        out_shape.append(jax.ShapeDtypeStruct((n_tiles, per_tile, D_MODEL), F32))
        out_specs.append(pl.BlockSpec((1, per_tile, D_MODEL), lambda i: (i, 0, 0)))
    return pl.pallas_call(
        functools.partial(_qkv_kernel, q_scale=q_scale),
        grid=(n_tiles,),
        in_specs=[
            pl.BlockSpec((QKV_TM, D_MODEL), lambda i: (i, 0)),
            pl.BlockSpec((1, D_MODEL), lambda i: (0, 0)),
            vec, vec,
            pl.BlockSpec((1, D_MODEL, 3 * D_MODEL), lambda i: (layer, 0, 0), pipeline_mode=pl.Buffered(1)),
        ],
        out_specs=out_specs,
        out_shape=out_shape,
        scratch_shapes=[pltpu.VMEM((D_MODEL, 3 * D_MODEL), BF16)],
        compiler_params=pltpu.CompilerParams(
            dimension_semantics=("arbitrary",), vmem_limit_bytes=VMEM_LIMIT),
        name="qkv_moba" if with_kmean else "qkv_diff",
    )(x2, g.reshape(1, D_MODEL), sc, sh, w_layers)


_NT = (((1,), (1,)), ((), ()))
GROUP = 4
GROUP_KEYS = GROUP * TILE
N_GROUPS = SEQ // GROUP_KEYS
GROUPS_PER_STEP = 4
N_STEPS = N_GROUPS // GROUPS_PER_STEP
STEP_KEYS = GROUPS_PER_STEP * GROUP_KEYS
ROW_TILE = 2 * TILE
N_ROW_TILES = SEQ // ROW_TILE
ROW_TILES_PER_GROUP = GROUP_KEYS // ROW_TILE
SOFTMAX_ROWS = 64
BIAS_FREE_FROM_ROW_TILE = -(-(N_BIAS_TILES + GROUP - 1) // (ROW_TILE // TILE))


def _flash_scratch():
    scores = pltpu.VMEM((ROW_TILE, GROUP_KEYS), F32)
    probs = pltpu.VMEM((ROW_TILE, GROUP_KEYS), BF16)
    rescale = pltpu.VMEM((ROW_TILE, 1), F32)
    values = pltpu.VMEM((2, GROUP_KEYS, 2 * HEAD_DIM), BF16)
    running_max = pltpu.VMEM((SEQ, 1), F32)
    acc = pltpu.VMEM((SEQ, 2 * HEAD_DIM), F32)
    return [scores, scores, probs, probs, rescale, rescale, values, running_max, acc]


def _tile_rows(tile):
    return pl.ds(pl.multiple_of(tile * ROW_TILE, ROW_TILE), ROW_TILE)


def _flash_sweep(grp, qa_ref, group_keys, group_values, bias_ref,
                 s0_ref, s1_ref, p0_ref, p1_ref, alpha0_ref, alpha1_ref, va_ref, m_ref, acc_ref):
    half = GROUP_KEYS // 2
    first = grp * ROW_TILES_PER_GROUP
    last = N_ROW_TILES - 1
    va_new = va_ref.at[grp % 2]
    va_old = va_ref.at[1 - grp % 2]
    va_new[:, :HEAD_DIM] = group_values()
    va_new[:, HEAD_DIM:] = jnp.ones((GROUP_KEYS, HEAD_DIM), BF16)

    def scores(tile, s_ref, n_keys=GROUP_KEYS, biased=True):
        raw = lax.dot_general(qa_ref[_tile_rows(tile), :], group_keys(n_keys), _NT,
                              preferred_element_type=F32)
        if not biased:
            s_ref[:, :n_keys] = raw
            return
        for r in range(ROW_TILE // TILE):
            for j in range(n_keys // TILE):
                bias = bias_ref[0, _table_index(tile * (ROW_TILE // TILE) + r - (grp * GROUP + j))]
                s_ref[r * TILE:(r + 1) * TILE, j * TILE:(j + 1) * TILE] = (
                    raw[r * TILE:(r + 1) * TILE, j * TILE:(j + 1) * TILE] + bias)

    def softmax(tile, s_ref, p_ref, alpha_ref, n_keys=GROUP_KEYS):
        for c in range(ROW_TILE // SOFTMAX_ROWS):
            chunk = slice(c * SOFTMAX_ROWS, (c + 1) * SOFTMAX_ROWS)
            rows = pl.ds(pl.multiple_of(tile * ROW_TILE + c * SOFTMAX_ROWS, SOFTMAX_ROWS), SOFTMAX_ROWS)
            s = s_ref[chunk, :n_keys]
            m_prev = m_ref[rows, :]
            m_new = jnp.maximum(m_prev, jnp.max(s, axis=1, keepdims=True))
            alpha_ref[chunk, :] = jnp.exp2(m_prev - m_new)
            m_ref[rows, :] = m_new
            p_ref[chunk, :n_keys] = jnp.exp2(s - m_new).astype(BF16)

    def accumulate(tile, p_ref, alpha_ref, va, n_keys=GROUP_KEYS):
        rows = _tile_rows(tile)
        acc_ref[rows, :] = (alpha_ref[...] * acc_ref[rows, :]
                            + jnp.dot(p_ref[:, :n_keys], va[:n_keys, :], preferred_element_type=F32))

    def enter(with_pending):
        scores(first, s0_ref, half)
        if with_pending:
            softmax(last, s1_ref, p1_ref, alpha1_ref)
            accumulate(last - 1, p0_ref, alpha0_ref, va_old)
        scores(first + 1, s1_ref)
        softmax(first, s0_ref, p0_ref, alpha0_ref, half)
        if with_pending:
            accumulate(last, p1_ref, alpha1_ref, va_old)

    def second_pair():
        scores(first + 2, s0_ref)
        softmax(first + 1, s1_ref, p1_ref, alpha1_ref)
        accumulate(first, p0_ref, alpha0_ref, va_new, half)
        scores(first + 3, s1_ref)
        softmax(first + 2, s0_ref, p0_ref, alpha0_ref)
        accumulate(first + 1, p1_ref, alpha1_ref, va_new)

    def pair(t, first_biased=False):
        tile = first + 3 + 2 * t
        scores(tile + 1, s0_ref, biased=first_biased)
        softmax(tile, s1_ref, p1_ref, alpha1_ref)
        accumulate(tile - 1, p0_ref, alpha0_ref, va_new)
        scores(tile + 2, s1_ref, biased=False)
        softmax(tile + 1, s0_ref, p0_ref, alpha0_ref)
        accumulate(tile, p1_ref, alpha1_ref, va_new)

    @pl.when(grp == 0)
    def _():
        enter(False)
        second_pair()

    @pl.when((grp > 0) & (grp < N_GROUPS - 1))
    def _():
        enter(True)
        second_pair()

    @pl.when(grp == N_GROUPS - 1)
    def _():
        enter(True)
        softmax(last, s1_ref, p1_ref, alpha1_ref)
        accumulate(last - 1, p0_ref, alpha0_ref, va_new, half)
        accumulate(last, p1_ref, alpha1_ref, va_new)

    assert BIAS_FREE_FROM_ROW_TILE == 5
    n_pairs = jnp.maximum(last - first - 3, 0) // 2

    @pl.when(n_pairs > 0)
    def _():
        pair(0, first_biased=True)

    @pl.when((n_pairs > 0) & (n_pairs % 2 == 0))
    def _():
        pair(1)

    def two_pairs(u, carry):
        t = 2 - n_pairs % 2 + 2 * u
        pair(t)
        pair(t + 1)
        return carry

    lax.fori_loop(0, jnp.maximum(n_pairs - 1, 0) // 2, two_pairs, 0)


def _reset_state(rows, m_ref, acc_ref):
    m_ref[rows, :] = jnp.full((rows.size, 1), NEG_INF, F32)
    acc_ref[rows, :] = jnp.zeros((rows.size, 2 * HEAD_DIM), F32)


def _attention_of(acc):
    return acc[:, :HEAD_DIM] / acc[:, HEAD_DIM:]


GATE_QUERIES = 8 * TILE


def _moba_gating(q_ref, km_ref, qa_ref, m_ref, acc_ref):
    blk = lax.broadcasted_iota(jnp.int32, (N_KV_BLOCKS, GATE_QUERIES), 0)
    blkf = blk.astype(F32)
    tile_in_step = lax.broadcasted_iota(jnp.int32, (N_KV_BLOCKS, GATE_QUERIES), 1) // TILE

    def gate_step(step, carry):
        rows = pl.ds(pl.multiple_of(step * GATE_QUERIES, GATE_QUERIES), GATE_QUERIES)
        _reset_state(rows, m_ref, acc_ref)
        q = q_ref[rows, :]
        gate = lax.dot_general(km_ref[...].astype(BF16), q, _NT, preferred_element_type=F32)
        tile = step * (GATE_QUERIES // TILE) + tile_in_step
        past = blk < tile
        g = jnp.where(past, gate, NEG_INF)
        madd = jnp.full(gate.shape, NEG_INF, F32)
        for _ in range(MOBA_TOPK):
            mx = jnp.max(g, axis=0, keepdims=True)
            first = jnp.min(jnp.where(g == mx, blkf, float(N_KV_BLOCKS)), axis=0, keepdims=True)
            hit = blkf == first
            madd = jnp.where(hit & past, 0.0, madd)
            g = jnp.where(hit, -jnp.inf, g)
        madd = jnp.where(blk == tile, 0.0, madd)
        madd = jnp.concatenate([madd, jnp.zeros((HEAD_DIM - N_KV_BLOCKS, GATE_QUERIES), F32)], axis=0)
        qa_ref[rows, :HEAD_DIM] = q
        qa_ref[rows, HEAD_DIM:] = madd.T.astype(BF16)
        return carry

    lax.fori_loop(0, SEQ // GATE_QUERIES, gate_step, 0, unroll=2)


def _step_rows(group_in_step, n_rows=GROUP_KEYS):
    return pl.ds(pl.multiple_of(group_in_step * GROUP_KEYS, GROUP_KEYS), n_rows)


def _moba_kernel(q_ref, k_ref, v_ref, km_ref, bias_ref, o_ref, qa_ref, ka_ref, *flash_refs):
    step = pl.program_id(2)
    m_ref, acc_ref = flash_refs[-2:]

    @pl.when(step == 0)
    def _():
        _moba_gating(q_ref, km_ref, qa_ref, m_ref, acc_ref)

    lane = lax.broadcasted_iota(jnp.int32, (GROUP_KEYS, HEAD_DIM), 1)
    block_in_group = lax.broadcasted_iota(jnp.int32, (GROUP_KEYS, HEAD_DIM), 0) // TILE

    def group(i, carry):
        grp = step * GROUPS_PER_STEP + i
        ka_ref[:, :HEAD_DIM] = k_ref[_step_rows(i), :]
        ka_ref[:, HEAD_DIM:] = jnp.where(lane == grp * GROUP + block_in_group, 1.0, 0.0).astype(BF16)
        _flash_sweep(grp, qa_ref, lambda n: ka_ref[:n, :], lambda: v_ref[_step_rows(i), :], bias_ref,
                     *flash_refs)
        return carry

    lax.fori_loop(0, GROUPS_PER_STEP, group, 0)

    @pl.when(step == N_STEPS - 1)
    def _():
        def finish(tile, carry):
            rows = _tile_rows(tile)
            o_ref[rows, :] = _attention_of(acc_ref[rows, :]).astype(BF16)
            return carry

        lax.fori_loop(0, N_ROW_TILES, finish, 0)


def _moba_attention(qkv, kmean, bias):
    return pl.pallas_call(
        _moba_kernel,
        grid=(BATCH, N_HEADS, N_STEPS),
        in_specs=[
            pl.BlockSpec((SEQ, HEAD_DIM), lambda b, h, g: (b, h)),
            pl.BlockSpec((STEP_KEYS, HEAD_DIM), lambda b, h, g: (b * N_STEPS + g, N_HEADS + h)),
            pl.BlockSpec((STEP_KEYS, HEAD_DIM), lambda b, h, g: (b * N_STEPS + g, 2 * N_HEADS + h)),
            pl.BlockSpec((N_KV_BLOCKS, HEAD_DIM), lambda b, h, g: (b, h)),
            pl.BlockSpec((1, N_TABLE_TILES, TILE, TILE), lambda b, h, g: (h, 0, 0, 0)),
        ],
        out_specs=pl.BlockSpec((SEQ, HEAD_DIM), lambda b, h, g: (b, h)),
        out_shape=jax.ShapeDtypeStruct((ROWS, D_MODEL), BF16),
        scratch_shapes=[
            pltpu.VMEM((SEQ, 2 * HEAD_DIM), BF16),
            pltpu.VMEM((GROUP_KEYS, 2 * HEAD_DIM), BF16),
        ] + _flash_scratch(),
        compiler_params=pltpu.CompilerParams(
            dimension_semantics=("arbitrary", "arbitrary", "arbitrary"), vmem_limit_bytes=VMEM_LIMIT),
        name="moba_attn",
    )(qkv, qkv, qkv, kmean, bias)


def _diff_kernel(q_ref, k_ref, v_ref, bias_ref, lam_ref, g_ref, o_ref, qa_ref, map0_ref, *flash_refs,
                 lambda_init):
    which = pl.program_id(2)
    step = pl.program_id(3)
    m_ref, acc_ref = flash_refs[-2:]

    @pl.when(step == 0)
    def _():
        lane_map = lax.broadcasted_iota(jnp.int32, (ROW_TILE, HEAD_DIM), 1) // DIFF_HALF

        def split(tile, carry):
            rows = _tile_rows(tile)
            _reset_state(rows, m_ref, acc_ref)
            q = q_ref[rows, :]
            qa_ref[rows, :] = jnp.where(lane_map == which, q, jnp.zeros_like(q))
            return carry

        lax.fori_loop(0, N_ROW_TILES, split, 0)

    def group(i, carry):
        _flash_sweep(step * GROUPS_PER_STEP + i, qa_ref, lambda n: k_ref[_step_rows(i, n), :],
                     lambda: v_ref[_step_rows(i), :], bias_ref, *flash_refs)
        return carry

    lax.fori_loop(0, GROUPS_PER_STEP, group, 0)

    @pl.when((step == N_STEPS - 1) & (which == 0))
    def _():
        def keep(tile, carry):
            rows = _tile_rows(tile)
            map0_ref[rows, :] = _attention_of(acc_ref[rows, :])
            return carry

        lax.fori_loop(0, N_ROW_TILES, keep, 0)

    @pl.when((step == N_STEPS - 1) & (which == 1))
    def _():
        lam = lam_ref[...]
        lam_full = (jnp.exp(jnp.sum(lam[0:1] * lam[1:2], axis=1, keepdims=True))
                    - jnp.exp(jnp.sum(lam[2:3] * lam[3:4], axis=1, keepdims=True)) + lambda_init)

        def finish(tile, carry):
            rows = _tile_rows(tile)
            o = map0_ref[rows, :] - lam_full * _attention_of(acc_ref[rows, :])
            o = o * lax.rsqrt(jnp.mean(o * o, axis=-1, keepdims=True) + RMS_EPS) * g_ref[...]
            o_ref[rows, :] = (o * (1.0 - lambda_init)).astype(BF16)
            return carry

        lax.fori_loop(0, N_ROW_TILES, finish, 0, unroll=2)


def _diff_attention(qkv, bias, lam, subln_g, layer_idx):
    lambda_init = 0.8 - 0.6 * math.exp(-0.3 * layer_idx)
    return pl.pallas_call(
        functools.partial(_diff_kernel, lambda_init=lambda_init),
        grid=(BATCH, N_HEADS, 2, N_STEPS),
        in_specs=[
            pl.BlockSpec((SEQ, HEAD_DIM), lambda b, h, c, g: (b, h)),
            pl.BlockSpec((STEP_KEYS, HEAD_DIM), lambda b, h, c, g: (b * N_STEPS + g, N_HEADS + h)),
            pl.BlockSpec((STEP_KEYS, HEAD_DIM), lambda b, h, c, g: (b * N_STEPS + g, 2 * N_HEADS + h)),
            pl.BlockSpec((1, N_TABLE_TILES, TILE, TILE), lambda b, h, c, g: (h, 0, 0, 0)),
            pl.BlockSpec((4, DIFF_HALF), lambda b, h, c, g: (0, 0)),
            pl.BlockSpec((1, HEAD_DIM), lambda b, h, c, g: (0, 0)),
        ],
        out_specs=pl.BlockSpec((SEQ, HEAD_DIM), lambda b, h, c, g: (b, h)),
        out_shape=jax.ShapeDtypeStruct((ROWS, D_MODEL), BF16),
        scratch_shapes=[
            pltpu.VMEM((SEQ, HEAD_DIM), BF16),
            pltpu.VMEM((SEQ, HEAD_DIM), F32),
        ] + _flash_scratch(),
        compiler_params=pltpu.CompilerParams(
            dimension_semantics=("arbitrary", "arbitrary", "arbitrary", "arbitrary"),
            vmem_limit_bytes=VMEM_LIMIT),
        name="diff_attn",
    )(qkv, qkv, qkv, bias, lam, subln_g.reshape(1, HEAD_DIM))


FFN_TM = 1024
FFN_TF = 256


def _rms(y, g):
    return y * lax.rsqrt(jnp.mean(y * y, axis=-1, keepdims=True) + RMS_EPS) * g


def _mixer_out_ffn_kernel(o_ref, x_ref, wa_ref, g1_ref, gate_a_ref, g2_ref, sc_ref, sh_ref, wg_ref, wu_ref,
                          wo_ref, g3_ref, gate_f_ref, out_ref, xn_ref, h_ref, acc_ref):
    c = pl.program_id(1)
    last = pl.num_programs(1) - 1

    def chunk(h):
        gp = jnp.dot(h, wg_ref[0].astype(BF16), preferred_element_type=F32)
        up = jnp.dot(h, wu_ref[0].astype(BF16), preferred_element_type=F32)
        act = (gp / (1.0 + jnp.exp(-gp)) * up).astype(BF16)
        return jnp.dot(act, wo_ref[0].astype(BF16), preferred_element_type=F32)

    @pl.when(c == 0)
    def _():
        y = jnp.dot(o_ref[...], wa_ref[...], preferred_element_type=F32)
        xn = x_ref[...] + gate_a_ref[0] * _rms(y, g1_ref[...])
        xn_ref[...] = xn
        h = _norm_mod(xn, g2_ref[...], sc_ref[0], sh_ref[0]).astype(BF16)
        h_ref[...] = h
        acc_ref[...] = chunk(h)

    @pl.when((c > 0) & (c < last))
    def _():
        acc_ref[...] += chunk(h_ref[...])

    @pl.when(c == last)
    def _():
        y = acc_ref[...] + chunk(h_ref[...])
        out_ref[...] = xn_ref[...] + gate_f_ref[0] * _rms(y, g3_ref[...])


def _mixer_out_ffn(o, x2, w_attn_out_bf16, g1, gate_a, g2, sc, sh, w_in_layers, w_out_layers, layer, g3, gate_f):
    tiles_per_batch = SEQ // FFN_TM
    n_chunks = D_FF // FFN_TF
    vec = pl.BlockSpec((1, 1, D_MODEL), lambda i, c: (i // tiles_per_batch, 0, 0))
    gvec = pl.BlockSpec((1, D_MODEL), lambda i, c: (0, 0))
    return pl.pallas_call(
        _mixer_out_ffn_kernel,
        grid=(ROWS // FFN_TM, n_chunks),
        in_specs=[
            pl.BlockSpec((FFN_TM, D_MODEL), lambda i, c: (i, 0)),
            pl.BlockSpec((FFN_TM, D_MODEL), lambda i, c: (i, 0)),
            pl.BlockSpec((D_MODEL, D_MODEL), lambda i, c: (0, 0), pipeline_mode=pl.Buffered(1)),
            gvec, vec,
            gvec, vec, vec,
            pl.BlockSpec((1, D_MODEL, FFN_TF), lambda i, c: (layer, 0, c)),
            pl.BlockSpec((1, D_MODEL, FFN_TF), lambda i, c: (layer, 0, n_chunks + c)),
            pl.BlockSpec((1, FFN_TF, D_MODEL), lambda i, c: (layer, c, 0)),
            gvec, vec,
        ],
        out_specs=pl.BlockSpec((FFN_TM, D_MODEL), lambda i, c: (i, 0)),
        out_shape=jax.ShapeDtypeStruct((ROWS, D_MODEL), F32),
        scratch_shapes=[
            pltpu.VMEM((FFN_TM, D_MODEL), F32),
            pltpu.VMEM((FFN_TM, D_MODEL), BF16),
            pltpu.VMEM((FFN_TM, D_MODEL), F32),
        ],
        compiler_params=pltpu.CompilerParams(
            dimension_semantics=("parallel", "arbitrary"), vmem_limit_bytes=VMEM_LIMIT),
        name="mixer_out_ffn",
    )(o, x2, w_attn_out_bf16, g1.reshape(1, D_MODEL), gate_a, g2.reshape(1, D_MODEL), sc, sh,
      w_in_layers, w_in_layers, w_out_layers, g3.reshape(1, D_MODEL), gate_f)


def kernel(x, c, rel_bias, ada_w, ada_b, norm_g, moba_w_qkv, moba_w_o, diff_w_qkv, diff_w_o, diff_lambda,
           diff_subln_g, ffn_w_in, ffn_w_out):
    x2 = x.reshape(ROWS, D_MODEL)
    mod = _adaln_mod(c, ada_w, ada_b)
    bias = _bias_tiles(rel_bias)
    for i in range(DEPTH):
        sh_a, sc_a, g_a, sh_f, sc_f, g_f = [
            mod[i, :, j * D_MODEL:(j + 1) * D_MODEL].reshape(BATCH, 1, D_MODEL) for j in range(6)]
        if i % 2 == 0:
            qkv, kmean = _qkv_proj(x2, norm_g[i, 0], sc_a, sh_a, moba_w_qkv, i // 2,
                                   HEAD_DIM ** -0.5 * LOG2_E, True)
            o = _moba_attention(qkv, kmean.reshape(BATCH * N_KV_BLOCKS, D_MODEL), bias)
            w_o = moba_w_o
        else:
            (qkv,) = _qkv_proj(x2, norm_g[i, 0], sc_a, sh_a, diff_w_qkv, i // 2,
                               DIFF_HALF ** -0.5 * LOG2_E, False)
            o = _diff_attention(qkv, bias, diff_lambda[i // 2], diff_subln_g[i // 2], i)
            w_o = diff_w_o
        x2 = _mixer_out_ffn(o, x2, w_o[i // 2].astype(BF16), norm_g[i, 1], g_a, norm_g[i, 2], sc_f, sh_f,
                            ffn_w_in, ffn_w_out, i, norm_g[i, 3], g_f)
    return x2.reshape(BATCH, SEQ, D_MODEL)
```

```python
import functools
import math

import numpy as np
import jax
import jax.numpy as jnp
from jax import lax
from jax.experimental import pallas as pl
from jax.experimental.pallas import tpu as pltpu

D_MODEL = 1024
BATCH = 2
SEQ = 8192
DEPTH = 2
N_HEADS = 8
HEAD_DIM = D_MODEL // N_HEADS
DIFF_HALF = HEAD_DIM // 2
MOBA_BLOCK = 256
MOBA_TOPK = 3
D_FF = 2816
N_BUCKETS = 32
MAX_EXACT = N_BUCKETS // 2
MAX_DISTANCE = 2048
RMS_EPS = 1e-6
NEG_INF = -1e30

ROWS = BATCH * SEQ
N_KV_BLOCKS = SEQ // MOBA_BLOCK
TILE = MOBA_BLOCK
VMEM_LIMIT = 48 * 1024 * 1024

F32 = jnp.float32
BF16 = jnp.bfloat16


def _bucket_of_distance():
    n = np.arange(MAX_DISTANCE + 1)
    nf = np.maximum(n, 1).astype(np.float64)
    val = np.log(nf / MAX_EXACT) / math.log(MAX_DISTANCE / MAX_EXACT) * (N_BUCKETS - MAX_EXACT)
    frac = np.abs(val - np.round(val))
    assert np.all((frac > 5e-5) | (n <= MAX_EXACT) | (n == MAX_DISTANCE))
    large = np.minimum(MAX_EXACT + np.floor(val + 1e-9).astype(np.int64), N_BUCKETS - 1)
    return np.where(n < MAX_EXACT, n, large)


_BUCKETS = _bucket_of_distance()
FAR_DISTANCE = int(np.min(np.nonzero(_BUCKETS == N_BUCKETS - 1)[0]))
N_BIAS_TILES = (FAR_DISTANCE + TILE - 1) // TILE + 1
N_TABLE_TILES = N_BIAS_TILES + 2


def _bias_bucket_tiles():
    i = np.arange(TILE)[:, None]
    j = np.arange(TILE)[None, :]
    out = []
    for t in range(-1, N_BIAS_TILES + 1):
        rel = t * TILE + i - j
        out.append(np.where(rel >= 0, _BUCKETS[np.clip(rel, 0, MAX_DISTANCE)], -1))
    return np.stack(out).astype(np.int32)


def _table_index(tiles_behind):
    return jnp.clip(tiles_behind, -1, N_BIAS_TILES) + 1


MOD_TN = 1024


def _mod_kernel(ct_ref, w_ref, b_ref, o_ref):
    ct = ct_ref[...]
    cact = ct / (1.0 + jnp.exp(-ct))
    w = w_ref[0]
    for b in range(BATCH):
        row = jnp.sum(w * cact[:, b:b + 1], axis=0, keepdims=True)
        o_ref[0, b:b + 1, :] = row + b_ref[0]


def _adaln_mod(c, ada_w, ada_b):
    n_out = ada_w.shape[-1]
    return pl.pallas_call(
        _mod_kernel,
        grid=(DEPTH, n_out // MOD_TN),
        in_specs=[
            pl.BlockSpec((D_MODEL, BATCH), lambda i, n: (0, 0)),
            pl.BlockSpec((1, D_MODEL, MOD_TN), lambda i, n: (i, 0, n)),
            pl.BlockSpec((1, 1, MOD_TN), lambda i, n: (i, 0, n)),
        ],
        out_specs=pl.BlockSpec((1, BATCH, MOD_TN), lambda i, n: (i, 0, n)),
        out_shape=jax.ShapeDtypeStruct((DEPTH, BATCH, n_out), F32),
        name="adaln_mod",
    )(c.T, ada_w, ada_b.reshape(DEPTH, 1, n_out))


LOG2_E = math.log2(math.e)


def _bias_kernel(tab_ref, bkt_ref, o_ref, *, buckets_present):
    h = pl.program_id(0)
    far = tab_ref[N_BUCKETS - 1, h]
    for t, present in enumerate(buckets_present):
        bkt = bkt_ref[t]
        acc = jnp.full(bkt.shape, NEG_INF, F32)
        for b in present:
            acc = jnp.where(bkt == b, (tab_ref[b, h] - far) * LOG2_E, acc)
        o_ref[0, t] = acc


def _bias_tiles(rel_bias):
    bkt = _bias_bucket_tiles()
    buckets_present = tuple(tuple(int(b) for b in np.unique(tile) if b >= 0) for tile in bkt)
    return pl.pallas_call(
        functools.partial(_bias_kernel, buckets_present=buckets_present),
        grid=(N_HEADS,),
        in_specs=[
            pl.BlockSpec(memory_space=pltpu.SMEM),
            pl.BlockSpec((N_TABLE_TILES, TILE, TILE), lambda h: (0, 0, 0)),
        ],
        out_specs=pl.BlockSpec((1, N_TABLE_TILES, TILE, TILE), lambda h: (h, 0, 0, 0)),
        out_shape=jax.ShapeDtypeStruct((N_HEADS, N_TABLE_TILES, TILE, TILE), F32),
        name="bias_tiles",
    )(rel_bias, jnp.asarray(bkt))


QKV_TM = 1024


def _norm_mod(x, g, sc, sh):
    y = x * lax.rsqrt(jnp.mean(x * x, axis=-1, keepdims=True) + RMS_EPS)
    return (y * g) * (1.0 + sc) + sh


def _cast_weight_once(w_ref, wb_ref):
    @pl.when(pl.program_id(0) == 0)
    def _():
        wb_ref[...] = w_ref[0].astype(BF16)


GAIN_MIXER_IN, GAIN_MIXER_OUT, GAIN_FFN_IN, GAIN_FFN_OUT = range(4)
MOD_SHIFT_MIXER, MOD_SCALE_MIXER, MOD_GATE_MIXER, MOD_SHIFT_FFN, MOD_SCALE_FFN, MOD_GATE_FFN = range(6)


def _gain(gains_ref, which):
    return gains_ref[0, which:which + 1, :]


def _mod(mod_ref, which):
    return mod_ref[0, 0, which:which + 1, :]


def _layer_vector_specs(layer, rows_per_tile):
    tiles_per_batch = SEQ // rows_per_tile
    return (pl.BlockSpec((1, 4, D_MODEL), lambda i, *_: (layer, 0, 0)),
            pl.BlockSpec((1, 1, 6, D_MODEL), lambda i, *_: (layer, i // tiles_per_batch, 0, 0)))


def _qkv_kernel(x_ref, gains_ref, mod_ref, w_ref, o_ref, *rest, q_scale):
    *km_ref, wb_ref = rest
    _cast_weight_once(w_ref, wb_ref)
    hb = _norm_mod(x_ref[...], _gain(gains_ref, GAIN_MIXER_IN), _mod(mod_ref, MOD_SCALE_MIXER),
                   _mod(mod_ref, MOD_SHIFT_MIXER)).astype(BF16)
    for n in range(3):
        r = jnp.dot(hb, wb_ref[:, n * D_MODEL:(n + 1) * D_MODEL], preferred_element_type=F32)
        if n == 0:
            r = r * q_scale
        if n == 1 and km_ref:
            km_ref[0][0] = jnp.mean(r.reshape(QKV_TM // MOBA_BLOCK, MOBA_BLOCK, D_MODEL), axis=1)
        o_ref[:, n * D_MODEL:(n + 1) * D_MODEL] = r.astype(BF16)


def _qkv_proj(x2, gains, mod, layer, w_layers, w_layer, q_scale, with_kmean):
    n_tiles = ROWS // QKV_TM
    out_shape = [jax.ShapeDtypeStruct((ROWS, 3 * D_MODEL), BF16)]
    out_specs = [pl.BlockSpec((QKV_TM, 3 * D_MODEL), lambda i: (i, 0))]
    if with_kmean:
        per_tile = QKV_TM // MOBA_BLOCK
        out_shape.append(jax.ShapeDtypeStruct((n_tiles, per_tile, D_MODEL), F32))
        out_specs.append(pl.BlockSpec((1, per_tile, D_MODEL), lambda i: (i, 0, 0)))
    return pl.pallas_call(
        functools.partial(_qkv_kernel, q_scale=q_scale),
        grid=(n_tiles,),
        in_specs=[
            pl.BlockSpec((QKV_TM, D_MODEL), lambda i: (i, 0)),
            *_layer_vector_specs(layer, QKV_TM),
            pl.BlockSpec((1, D_MODEL, 3 * D_MODEL), lambda i: (w_layer, 0, 0), pipeline_mode=pl.Buffered(1)),
        ],
        out_specs=out_specs,
        out_shape=out_shape,
        scratch_shapes=[pltpu.VMEM((D_MODEL, 3 * D_MODEL), BF16)],
        compiler_params=pltpu.CompilerParams(
            dimension_semantics=("arbitrary",), vmem_limit_bytes=VMEM_LIMIT),
        name="qkv_moba" if with_kmean else "qkv_diff",
    )(x2, gains, mod, w_layers)


_NT = (((1,), (1,)), ((), ()))
GROUP = 4
GROUP_KEYS = GROUP * TILE
N_GROUPS = SEQ // GROUP_KEYS
GROUPS_PER_STEP = 4
N_STEPS = N_GROUPS // GROUPS_PER_STEP
STEP_KEYS = GROUPS_PER_STEP * GROUP_KEYS
ROW_TILE = 2 * TILE
N_ROW_TILES = SEQ // ROW_TILE
ROW_TILES_PER_GROUP = GROUP_KEYS // ROW_TILE
SOFTMAX_ROWS = 64
BIAS_FREE_FROM_ROW_TILE = -(-(N_BIAS_TILES + GROUP - 1) // (ROW_TILE // TILE))


def _flash_scratch():
    scores = pltpu.VMEM((ROW_TILE, GROUP_KEYS), F32)
    probs = pltpu.VMEM((ROW_TILE, GROUP_KEYS), BF16)
    rescale = pltpu.VMEM((ROW_TILE, 1), F32)
    values = pltpu.VMEM((2, GROUP_KEYS, 2 * HEAD_DIM), BF16)
    running_max = pltpu.VMEM((SEQ, 1), F32)
    acc = pltpu.VMEM((SEQ, 2 * HEAD_DIM), F32)
    return [scores, scores, probs, probs, rescale, rescale, values, running_max, acc]


def _tile_rows(tile):
    return pl.ds(pl.multiple_of(tile * ROW_TILE, ROW_TILE), ROW_TILE)


def _flash_sweep(grp, qa_ref, group_keys, group_values, bias_ref,
                 s0_ref, s1_ref, p0_ref, p1_ref, alpha0_ref, alpha1_ref, va_ref, m_ref, acc_ref):
    half = GROUP_KEYS // 2
    first = grp * ROW_TILES_PER_GROUP
    last = N_ROW_TILES - 1
    va_new = va_ref.at[grp % 2]
    va_old = va_ref.at[1 - grp % 2]
    va_new[:, :HEAD_DIM] = group_values()

    @pl.when(grp < 2)
    def _():
        va_new[:, HEAD_DIM:] = jnp.ones((GROUP_KEYS, HEAD_DIM), BF16)

    def scores(tile, s_ref, n_keys=GROUP_KEYS, biased=True):
        raw = lax.dot_general(qa_ref[_tile_rows(tile), :], group_keys(n_keys), _NT,
                              preferred_element_type=F32)
        if not biased:
            s_ref[:, :n_keys] = raw
            return
        for r in range(ROW_TILE // TILE):
            for j in range(n_keys // TILE):
                bias = bias_ref[0, _table_index(tile * (ROW_TILE // TILE) + r - (grp * GROUP + j))]
                s_ref[r * TILE:(r + 1) * TILE, j * TILE:(j + 1) * TILE] = (
                    raw[r * TILE:(r + 1) * TILE, j * TILE:(j + 1) * TILE] + bias)

    def softmax(tile, s_ref, p_ref, alpha_ref, n_keys=GROUP_KEYS):
        for c in range(ROW_TILE // SOFTMAX_ROWS):
            chunk = slice(c * SOFTMAX_ROWS, (c + 1) * SOFTMAX_ROWS)
            rows = pl.ds(pl.multiple_of(tile * ROW_TILE + c * SOFTMAX_ROWS, SOFTMAX_ROWS), SOFTMAX_ROWS)
            s = s_ref[chunk, :n_keys]
            m_prev = m_ref[rows, :]
            m_new = jnp.maximum(m_prev, jnp.max(s, axis=1, keepdims=True))
            alpha_ref[chunk, :] = jnp.exp2(m_prev - m_new)
            m_ref[rows, :] = m_new
            p_ref[chunk, :n_keys] = jnp.exp2(s - m_new).astype(BF16)

    def accumulate(tile, p_ref, alpha_ref, va, n_keys=GROUP_KEYS):
        rows = _tile_rows(tile)
        acc_ref[rows, :] = (alpha_ref[...] * acc_ref[rows, :]
                            + jnp.dot(p_ref[:, :n_keys], va[:n_keys, :], preferred_element_type=F32))

    def enter(with_pending):
        scores(first, s0_ref, half)
        if with_pending:
            softmax(last, s1_ref, p1_ref, alpha1_ref)
            accumulate(last - 1, p0_ref, alpha0_ref, va_old)
        scores(first + 1, s1_ref)
        softmax(first, s0_ref, p0_ref, alpha0_ref, half)
        if with_pending:
            accumulate(last, p1_ref, alpha1_ref, va_old)

    def second_pair():
        scores(first + 2, s0_ref)
        softmax(first + 1, s1_ref, p1_ref, alpha1_ref)
        accumulate(first, p0_ref, alpha0_ref, va_new, half)
        scores(first + 3, s1_ref)
        softmax(first + 2, s0_ref, p0_ref, alpha0_ref)
        accumulate(first + 1, p1_ref, alpha1_ref, va_new)

    def pair(t, first_biased=False):
        tile = first + 3 + 2 * t
        scores(tile + 1, s0_ref, biased=first_biased)
        softmax(tile, s1_ref, p1_ref, alpha1_ref)
        accumulate(tile - 1, p0_ref, alpha0_ref, va_new)
        scores(tile + 2, s1_ref, biased=False)
        softmax(tile + 1, s0_ref, p0_ref, alpha0_ref)
        accumulate(tile, p1_ref, alpha1_ref, va_new)

    @pl.when(grp == 0)
    def _():
        enter(False)
        second_pair()

    @pl.when((grp > 0) & (grp < N_GROUPS - 1))
    def _():
        enter(True)
        second_pair()

    @pl.when(grp == N_GROUPS - 1)
    def _():
        enter(True)
        softmax(last, s1_ref, p1_ref, alpha1_ref)
        accumulate(last - 1, p0_ref, alpha0_ref, va_new, half)
        accumulate(last, p1_ref, alpha1_ref, va_new)

    assert BIAS_FREE_FROM_ROW_TILE == 5
    n_pairs = jnp.maximum(last - first - 3, 0) // 2

    @pl.when(n_pairs > 0)
    def _():
        pair(0, first_biased=True)

    @pl.when((n_pairs > 0) & (n_pairs % 2 == 0))
    def _():
        pair(1)

    def two_pairs(u, carry):
        t = 2 - n_pairs % 2 + 2 * u
        pair(t)
        pair(t + 1)
        return carry

    lax.fori_loop(0, jnp.maximum(n_pairs - 1, 0) // 2, two_pairs, 0)


def _reset_state(rows, m_ref, acc_ref):
    m_ref[rows, :] = jnp.full((rows.size, 1), NEG_INF, F32)
    acc_ref[rows, :] = jnp.zeros((rows.size, 2 * HEAD_DIM), F32)


def _attention_of(acc):
    return acc[:, :HEAD_DIM] / acc[:, HEAD_DIM:]


GATE_QUERIES = 8 * TILE


def _moba_gating(q_ref, km_ref, qa_ref, m_ref, acc_ref):
    blk = lax.broadcasted_iota(jnp.int32, (N_KV_BLOCKS, GATE_QUERIES), 0)
    blkf = blk.astype(F32)
    tile_in_step = lax.broadcasted_iota(jnp.int32, (N_KV_BLOCKS, GATE_QUERIES), 1) // TILE

    def gate_step(step, carry):
        rows = pl.ds(pl.multiple_of(step * GATE_QUERIES, GATE_QUERIES), GATE_QUERIES)
        _reset_state(rows, m_ref, acc_ref)
        q = q_ref[rows, :]
        gate = lax.dot_general(km_ref[...].astype(BF16), q, _NT, preferred_element_type=F32)
        tile = step * (GATE_QUERIES // TILE) + tile_in_step
        past = blk < tile
        g = jnp.where(past, gate, NEG_INF)
        madd = jnp.full(gate.shape, NEG_INF, F32)
        for _ in range(MOBA_TOPK):
            mx = jnp.max(g, axis=0, keepdims=True)
            first = jnp.min(jnp.where(g == mx, blkf, float(N_KV_BLOCKS)), axis=0, keepdims=True)
            hit = blkf == first
            madd = jnp.where(hit & past, 0.0, madd)
            g = jnp.where(hit, -jnp.inf, g)
        madd = jnp.where(blk == tile, 0.0, madd)
        madd = jnp.concatenate([madd, jnp.zeros((HEAD_DIM - N_KV_BLOCKS, GATE_QUERIES), F32)], axis=0)
        qa_ref[rows, :HEAD_DIM] = q
        qa_ref[rows, HEAD_DIM:] = madd.T.astype(BF16)
        return carry

    lax.fori_loop(0, SEQ // GATE_QUERIES, gate_step, 0, unroll=2)


def _step_rows(group_in_step, n_rows=GROUP_KEYS):
    return pl.ds(pl.multiple_of(group_in_step * GROUP_KEYS, GROUP_KEYS), n_rows)


def _moba_kernel(q_ref, k_ref, v_ref, km_ref, bias_ref, o_ref, qa_ref, ka_ref, *flash_refs):
    step = pl.program_id(2)
    m_ref, acc_ref = flash_refs[-2:]

    @pl.when(step == 0)
    def _():
        _moba_gating(q_ref, km_ref, qa_ref, m_ref, acc_ref)

    lane = lax.broadcasted_iota(jnp.int32, (GROUP_KEYS, HEAD_DIM), 1)
    block_in_group = lax.broadcasted_iota(jnp.int32, (GROUP_KEYS, HEAD_DIM), 0) // TILE

    def group(i, carry):
        grp = step * GROUPS_PER_STEP + i
        ka_ref[:, :HEAD_DIM] = k_ref[_step_rows(i), :]
        ka_ref[:, HEAD_DIM:] = jnp.where(lane == grp * GROUP + block_in_group, 1.0, 0.0).astype(BF16)
        _flash_sweep(grp, qa_ref, lambda n: ka_ref[:n, :], lambda: v_ref[_step_rows(i), :], bias_ref,
                     *flash_refs)
        return carry

    lax.fori_loop(0, GROUPS_PER_STEP, group, 0)

    @pl.when(step == N_STEPS - 1)
    def _():
        def finish(tile, carry):
            rows = _tile_rows(tile)
            o_ref[rows, :] = _attention_of(acc_ref[rows, :]).astype(BF16)
            return carry

        lax.fori_loop(0, N_ROW_TILES, finish, 0)


def _moba_attention(qkv, kmean, bias):
    return pl.pallas_call(
        _moba_kernel,
        grid=(BATCH, N_HEADS, N_STEPS),
        in_specs=[
            pl.BlockSpec((SEQ, HEAD_DIM), lambda b, h, g: (b, h)),
            pl.BlockSpec((STEP_KEYS, HEAD_DIM), lambda b, h, g: (b * N_STEPS + g, N_HEADS + h)),
            pl.BlockSpec((STEP_KEYS, HEAD_DIM), lambda b, h, g: (b * N_STEPS + g, 2 * N_HEADS + h)),
            pl.BlockSpec((N_KV_BLOCKS, HEAD_DIM), lambda b, h, g: (b, h)),
            pl.BlockSpec((1, N_TABLE_TILES, TILE, TILE), lambda b, h, g: (h, 0, 0, 0)),
        ],
        out_specs=pl.BlockSpec((SEQ, HEAD_DIM), lambda b, h, g: (b, h)),
        out_shape=jax.ShapeDtypeStruct((ROWS, D_MODEL), BF16),
        scratch_shapes=[
            pltpu.VMEM((SEQ, 2 * HEAD_DIM), BF16),
            pltpu.VMEM((GROUP_KEYS, 2 * HEAD_DIM), BF16),
        ] + _flash_scratch(),
        compiler_params=pltpu.CompilerParams(
            dimension_semantics=("arbitrary", "arbitrary", "arbitrary"), vmem_limit_bytes=VMEM_LIMIT),
        name="moba_attn",
    )(qkv, qkv, qkv, kmean, bias)


def _diff_kernel(q_ref, k_ref, v_ref, bias_ref, lam_ref, g_ref, o_ref, qa_ref, map0_ref, *flash_refs,
                 lambda_init):
    which = pl.program_id(2)
    step = pl.program_id(3)
    m_ref, acc_ref = flash_refs[-2:]

    @pl.when(step == 0)
    def _():
        lane_map = lax.broadcasted_iota(jnp.int32, (ROW_TILE, HEAD_DIM), 1) // DIFF_HALF

        def split(tile, carry):
            rows = _tile_rows(tile)
            _reset_state(rows, m_ref, acc_ref)
            q = q_ref[rows, :]
            qa_ref[rows, :] = jnp.where(lane_map == which, q, jnp.zeros_like(q))
            return carry

        lax.fori_loop(0, N_ROW_TILES, split, 0)

    def group(i, carry):
        _flash_sweep(step * GROUPS_PER_STEP + i, qa_ref, lambda n: k_ref[_step_rows(i, n), :],
                     lambda: v_ref[_step_rows(i), :], bias_ref, *flash_refs)
        return carry

    lax.fori_loop(0, GROUPS_PER_STEP, group, 0)

    @pl.when((step == N_STEPS - 1) & (which == 0))
    def _():
        def keep(tile, carry):
            rows = _tile_rows(tile)
            map0_ref[rows, :] = _attention_of(acc_ref[rows, :])
            return carry

        lax.fori_loop(0, N_ROW_TILES, keep, 0)

    @pl.when((step == N_STEPS - 1) & (which == 1))
    def _():
        lam = lam_ref[...]
        lam_full = (jnp.exp(jnp.sum(lam[0:1] * lam[1:2], axis=1, keepdims=True))
                    - jnp.exp(jnp.sum(lam[2:3] * lam[3:4], axis=1, keepdims=True)) + lambda_init)

        def finish(tile, carry):
            rows = _tile_rows(tile)
            o = map0_ref[rows, :] - lam_full * _attention_of(acc_ref[rows, :])
            o = o * lax.rsqrt(jnp.mean(o * o, axis=-1, keepdims=True) + RMS_EPS) * g_ref[...]
            o_ref[rows, :] = (o * (1.0 - lambda_init)).astype(BF16)
            return carry

        lax.fori_loop(0, N_ROW_TILES, finish, 0, unroll=2)


def _diff_attention(qkv, bias, lam, subln_g, layer_idx):
    lambda_init = 0.8 - 0.6 * math.exp(-0.3 * layer_idx)
    return pl.pallas_call(
        functools.partial(_diff_kernel, lambda_init=lambda_init),
        grid=(BATCH, N_HEADS, 2, N_STEPS),
        in_specs=[
            pl.BlockSpec((SEQ, HEAD_DIM), lambda b, h, c, g: (b, h)),
            pl.BlockSpec((STEP_KEYS, HEAD_DIM), lambda b, h, c, g: (b * N_STEPS + g, N_HEADS + h)),
            pl.BlockSpec((STEP_KEYS, HEAD_DIM), lambda b, h, c, g: (b * N_STEPS + g, 2 * N_HEADS + h)),
            pl.BlockSpec((1, N_TABLE_TILES, TILE, TILE), lambda b, h, c, g: (h, 0, 0, 0)),
            pl.BlockSpec((4, DIFF_HALF), lambda b, h, c, g: (0, 0)),
            pl.BlockSpec((1, HEAD_DIM), lambda b, h, c, g: (0, 0)),
        ],
        out_specs=pl.BlockSpec((SEQ, HEAD_DIM), lambda b, h, c, g: (b, h)),
        out_shape=jax.ShapeDtypeStruct((ROWS, D_MODEL), BF16),
        scratch_shapes=[
            pltpu.VMEM((SEQ, HEAD_DIM), BF16),
            pltpu.VMEM((SEQ, HEAD_DIM), F32),
        ] + _flash_scratch(),
        compiler_params=pltpu.CompilerParams(
            dimension_semantics=("arbitrary", "arbitrary", "arbitrary", "arbitrary"),
            vmem_limit_bytes=VMEM_LIMIT),
        name="diff_attn",
    )(qkv, qkv, qkv, bias, lam, subln_g.reshape(1, HEAD_DIM))


FFN_TM = 1024
FFN_TF = 256


def _rms(y, g):
    return y * lax.rsqrt(jnp.mean(y * y, axis=-1, keepdims=True) + RMS_EPS) * g


def _mixer_out_ffn_kernel(o_ref, x_ref, wa_ref, gains_ref, mod_ref, wg_ref, wu_ref, wo_ref, out_ref,
                          xn_ref, h_ref, acc_ref):
    c = pl.program_id(1)
    last = pl.num_programs(1) - 1

    def chunk(h):
        gp = jnp.dot(h, wg_ref[0].astype(BF16), preferred_element_type=F32)
        up = jnp.dot(h, wu_ref[0].astype(BF16), preferred_element_type=F32)
        act = (gp / (1.0 + jnp.exp(-gp)) * up).astype(BF16)
        return jnp.dot(act, wo_ref[0].astype(BF16), preferred_element_type=F32)

    @pl.when(c == 0)
    def _():
        y = jnp.dot(o_ref[...], wa_ref[...], preferred_element_type=F32)
        xn = x_ref[...] + _mod(mod_ref, MOD_GATE_MIXER) * _rms(y, _gain(gains_ref, GAIN_MIXER_OUT))
        xn_ref[...] = xn
        h = _norm_mod(xn, _gain(gains_ref, GAIN_FFN_IN), _mod(mod_ref, MOD_SCALE_FFN),
                      _mod(mod_ref, MOD_SHIFT_FFN)).astype(BF16)
        h_ref[...] = h
        acc_ref[...] = chunk(h)

    @pl.when((c > 0) & (c < last))
    def _():
        acc_ref[...] += chunk(h_ref[...])

    @pl.when(c == last)
    def _():
        y = acc_ref[...] + chunk(h_ref[...])
        out_ref[...] = xn_ref[...] + _mod(mod_ref, MOD_GATE_FFN) * _rms(y, _gain(gains_ref, GAIN_FFN_OUT))


def _mixer_out_ffn(o, x2, w_attn_out_bf16, gains, mod, layer, w_in_layers, w_out_layers):
    n_chunks = D_FF // FFN_TF
    return pl.pallas_call(
        _mixer_out_ffn_kernel,
        grid=(ROWS // FFN_TM, n_chunks),
        in_specs=[
            pl.BlockSpec((FFN_TM, D_MODEL), lambda i, c: (i, 0)),
            pl.BlockSpec((FFN_TM, D_MODEL), lambda i, c: (i, 0)),
            pl.BlockSpec((D_MODEL, D_MODEL), lambda i, c: (0, 0), pipeline_mode=pl.Buffered(1)),
            *_layer_vector_specs(layer, FFN_TM),
            pl.BlockSpec((1, D_MODEL, FFN_TF), lambda i, c: (layer, 0, c)),
            pl.BlockSpec((1, D_MODEL, FFN_TF), lambda i, c: (layer, 0, n_chunks + c)),
            pl.BlockSpec((1, FFN_TF, D_MODEL), lambda i, c: (layer, c, 0)),
        ],
        out_specs=pl.BlockSpec((FFN_TM, D_MODEL), lambda i, c: (i, 0)),
        out_shape=jax.ShapeDtypeStruct((ROWS, D_MODEL), F32),
        scratch_shapes=[
            pltpu.VMEM((FFN_TM, D_MODEL), F32),
            pltpu.VMEM((FFN_TM, D_MODEL), BF16),
            pltpu.VMEM((FFN_TM, D_MODEL), F32),
        ],
        compiler_params=pltpu.CompilerParams(
            dimension_semantics=("parallel", "arbitrary"), vmem_limit_bytes=VMEM_LIMIT),
        name="mixer_out_ffn",
    )(o, x2, w_attn_out_bf16, gains, mod, w_in_layers, w_in_layers, w_out_layers)


def kernel(x, c, rel_bias, ada_w, ada_b, norm_g, moba_w_qkv, moba_w_o, diff_w_qkv, diff_w_o, diff_lambda,
           diff_subln_g, ffn_w_in, ffn_w_out):
    x2 = x.reshape(ROWS, D_MODEL)
    mod = _adaln_mod(c, ada_w, ada_b).reshape(DEPTH, BATCH, 6, D_MODEL)
    bias = _bias_tiles(rel_bias)
    for i in range(DEPTH):
        if i % 2 == 0:
            qkv, kmean = _qkv_proj(x2, norm_g, mod, i, moba_w_qkv, i // 2, HEAD_DIM ** -0.5 * LOG2_E, True)
            o = _moba_attention(qkv, kmean.reshape(BATCH * N_KV_BLOCKS, D_MODEL), bias)
            w_o = moba_w_o
        else:
            (qkv,) = _qkv_proj(x2, norm_g, mod, i, diff_w_qkv, i // 2, DIFF_HALF ** -0.5 * LOG2_E, False)
            o = _diff_attention(qkv, bias, diff_lambda[i // 2], diff_subln_g[i // 2], i)
            w_o = diff_w_o
        x2 = _mixer_out_ffn(o, x2, w_o[i // 2].astype(BF16), norm_g, mod, i, ffn_w_in, ffn_w_out)
    return x2.reshape(BATCH, SEQ, D_MODEL)
```

```python
import functools
import math

import numpy as np
import jax
import jax.numpy as jnp
from jax import lax
from jax.experimental import pallas as pl
from jax.experimental.pallas import tpu as pltpu

D_MODEL = 1024
BATCH = 2
SEQ = 8192
DEPTH = 2
N_HEADS = 8
HEAD_DIM = D_MODEL // N_HEADS
DIFF_HALF = HEAD_DIM // 2
MOBA_BLOCK = 256
MOBA_TOPK = 3
D_FF = 2816
N_BUCKETS = 32
MAX_EXACT = N_BUCKETS // 2
MAX_DISTANCE = 2048
RMS_EPS = 1e-6
NEG_INF = -1e30

ROWS = BATCH * SEQ
N_KV_BLOCKS = SEQ // MOBA_BLOCK
TILE = MOBA_BLOCK
VMEM_LIMIT = 48 * 1024 * 1024

F32 = jnp.float32
BF16 = jnp.bfloat16


def _bucket_of_distance():
    n = np.arange(MAX_DISTANCE + 1)
    nf = np.maximum(n, 1).astype(np.float64)
    val = np.log(nf / MAX_EXACT) / math.log(MAX_DISTANCE / MAX_EXACT) * (N_BUCKETS - MAX_EXACT)
    frac = np.abs(val - np.round(val))
    assert np.all((frac > 5e-5) | (n <= MAX_EXACT) | (n == MAX_DISTANCE))
    large = np.minimum(MAX_EXACT + np.floor(val + 1e-9).astype(np.int64), N_BUCKETS - 1)
    return np.where(n < MAX_EXACT, n, large)


_BUCKETS = _bucket_of_distance()
FAR_DISTANCE = int(np.min(np.nonzero(_BUCKETS == N_BUCKETS - 1)[0]))
N_BIAS_TILES = (FAR_DISTANCE + TILE - 1) // TILE + 1
N_TABLE_TILES = N_BIAS_TILES + 2


def _bias_bucket_tiles():
    i = np.arange(TILE)[:, None]
    j = np.arange(TILE)[None, :]
    out = []
    for t in range(-1, N_BIAS_TILES + 1):
        rel = t * TILE + i - j
        out.append(np.where(rel >= 0, _BUCKETS[np.clip(rel, 0, MAX_DISTANCE)], -1))
    return np.stack(out).astype(np.int32)


def _table_index(tiles_behind):
    return jnp.clip(tiles_behind, -1, N_BIAS_TILES) + 1


MOD_TN = 1024


def _mod_kernel(ct_ref, w_ref, b_ref, o_ref):
    ct = ct_ref[...]
    cact = ct / (1.0 + jnp.exp(-ct))
    w = w_ref[0]
    for b in range(BATCH):
        row = jnp.sum(w * cact[:, b:b + 1], axis=0, keepdims=True)
        o_ref[0, b:b + 1, :] = row + b_ref[0]


def _adaln_mod(c, ada_w, ada_b):
    n_out = ada_w.shape[-1]
    return pl.pallas_call(
        _mod_kernel,
        grid=(DEPTH, n_out // MOD_TN),
        in_specs=[
            pl.BlockSpec((D_MODEL, BATCH), lambda i, n: (0, 0)),
            pl.BlockSpec((1, D_MODEL, MOD_TN), lambda i, n: (i, 0, n)),
            pl.BlockSpec((1, 1, MOD_TN), lambda i, n: (i, 0, n)),
        ],
        out_specs=pl.BlockSpec((1, BATCH, MOD_TN), lambda i, n: (i, 0, n)),
        out_shape=jax.ShapeDtypeStruct((DEPTH, BATCH, n_out), F32),
        name="adaln_mod",
    )(c.T, ada_w, ada_b.reshape(DEPTH, 1, n_out))


LOG2_E = math.log2(math.e)


def _bias_kernel(tab_ref, bkt_ref, o_ref, *, buckets_present):
    h = pl.program_id(0)
    far = tab_ref[N_BUCKETS - 1, h]
    for t, present in enumerate(buckets_present):
        bkt = bkt_ref[t]
        acc = jnp.full(bkt.shape, NEG_INF, F32)
        for b in present:
            acc = jnp.where(bkt == b, (tab_ref[b, h] - far) * LOG2_E, acc)
        o_ref[0, t] = acc


def _bias_tiles(rel_bias):
    bkt = _bias_bucket_tiles()
    buckets_present = tuple(tuple(int(b) for b in np.unique(tile) if b >= 0) for tile in bkt)
    return pl.pallas_call(
        functools.partial(_bias_kernel, buckets_present=buckets_present),
        grid=(N_HEADS,),
        in_specs=[
            pl.BlockSpec(memory_space=pltpu.SMEM),
            pl.BlockSpec((N_TABLE_TILES, TILE, TILE), lambda h: (0, 0, 0)),
        ],
        out_specs=pl.BlockSpec((1, N_TABLE_TILES, TILE, TILE), lambda h: (h, 0, 0, 0)),
        out_shape=jax.ShapeDtypeStruct((N_HEADS, N_TABLE_TILES, TILE, TILE), F32),
        name="bias_tiles",
    )(rel_bias, jnp.asarray(bkt))


QKV_TM = 1024


def _norm_mod(x, g, sc, sh):
    y = x * lax.rsqrt(jnp.mean(x * x, axis=-1, keepdims=True) + RMS_EPS)
    return (y * g) * (1.0 + sc) + sh


def _cast_weight_once(w_ref, wb_ref):
    @pl.when(pl.program_id(0) == 0)
    def _():
        wb_ref[...] = w_ref[0].astype(BF16)


GAIN_MIXER_IN, GAIN_MIXER_OUT, GAIN_FFN_IN, GAIN_FFN_OUT = range(4)
MOD_SHIFT_MIXER, MOD_SCALE_MIXER, MOD_GATE_MIXER, MOD_SHIFT_FFN, MOD_SCALE_FFN, MOD_GATE_FFN = range(6)


def _gain(gains_ref, which):
    return gains_ref[0, which:which + 1, :]


def _mod(mod_ref, which):
    return mod_ref[0, 0, which:which + 1, :]


def _layer_vector_specs(layer, rows_per_tile):
    tiles_per_batch = SEQ // rows_per_tile
    return (pl.BlockSpec((1, 4, D_MODEL), lambda i, *_: (layer, 0, 0)),
            pl.BlockSpec((1, 1, 6, D_MODEL), lambda i, *_: (layer, i // tiles_per_batch, 0, 0)))


def _qkv_kernel(x_ref, gains_ref, mod_ref, w_ref, o_ref, *rest, q_scale):
    *km_ref, wb_ref = rest
    _cast_weight_once(w_ref, wb_ref)
    hb = _norm_mod(x_ref[...], _gain(gains_ref, GAIN_MIXER_IN), _mod(mod_ref, MOD_SCALE_MIXER),
                   _mod(mod_ref, MOD_SHIFT_MIXER)).astype(BF16)
    for n in range(3):
        r = jnp.dot(hb, wb_ref[:, n * D_MODEL:(n + 1) * D_MODEL], preferred_element_type=F32)
        if n == 0:
            r = r * q_scale
        if n == 1 and km_ref:
            km_ref[0][0] = jnp.mean(r.reshape(QKV_TM // MOBA_BLOCK, MOBA_BLOCK, D_MODEL), axis=1)
        o_ref[:, n * D_MODEL:(n + 1) * D_MODEL] = r.astype(BF16)


def _qkv_proj(x2, gains, mod, layer, w_layers, w_layer, q_scale, with_kmean):
    n_tiles = ROWS // QKV_TM
    out_shape = [jax.ShapeDtypeStruct((ROWS, 3 * D_MODEL), BF16)]
    out_specs = [pl.BlockSpec((QKV_TM, 3 * D_MODEL), lambda i: (i, 0))]
    if with_kmean:
        per_tile = QKV_TM // MOBA_BLOCK
        out_shape.append(jax.ShapeDtypeStruct((n_tiles, per_tile, D_MODEL), F32))
        out_specs.append(pl.BlockSpec((1, per_tile, D_MODEL), lambda i: (i, 0, 0)))
    return pl.pallas_call(
        functools.partial(_qkv_kernel, q_scale=q_scale),
        grid=(n_tiles,),
        in_specs=[
            pl.BlockSpec((QKV_TM, D_MODEL), lambda i: (i, 0)),
            *_layer_vector_specs(layer, QKV_TM),
            pl.BlockSpec((1, D_MODEL, 3 * D_MODEL), lambda i: (w_layer, 0, 0), pipeline_mode=pl.Buffered(1)),
        ],
        out_specs=out_specs,
        out_shape=out_shape,
        scratch_shapes=[pltpu.VMEM((D_MODEL, 3 * D_MODEL), BF16)],
        compiler_params=pltpu.CompilerParams(
            dimension_semantics=("arbitrary",), vmem_limit_bytes=VMEM_LIMIT),
        name="qkv_moba" if with_kmean else "qkv_diff",
    )(x2, gains, mod, w_layers)


_NT = (((1,), (1,)), ((), ()))
GROUP = 4
GROUP_KEYS = GROUP * TILE
N_GROUPS = SEQ // GROUP_KEYS
GROUPS_PER_STEP = 4
N_STEPS = N_GROUPS // GROUPS_PER_STEP
STEP_KEYS = GROUPS_PER_STEP * GROUP_KEYS
ROW_TILE = 2 * TILE
N_ROW_TILES = SEQ // ROW_TILE
ROW_TILES_PER_GROUP = GROUP_KEYS // ROW_TILE
SOFTMAX_ROWS = 64
BIAS_FREE_FROM_ROW_TILE = -(-(N_BIAS_TILES + GROUP - 1) // (ROW_TILE // TILE))


def _flash_scratch():
    scores = pltpu.VMEM((ROW_TILE, GROUP_KEYS), F32)
    probs = pltpu.VMEM((ROW_TILE, GROUP_KEYS), BF16)
    rescale = pltpu.VMEM((ROW_TILE, 1), F32)
    values = pltpu.VMEM((2, GROUP_KEYS, 2 * HEAD_DIM), BF16)
    running_max = pltpu.VMEM((SEQ, 1), F32)
    acc = pltpu.VMEM((SEQ, 2 * HEAD_DIM), F32)
    return [scores, scores, probs, probs, rescale, rescale, values, running_max, acc]


def _tile_rows(tile):
    return pl.ds(pl.multiple_of(tile * ROW_TILE, ROW_TILE), ROW_TILE)


def _flash_sweep(grp, qa_ref, group_keys, group_values, bias_ref,
                 s0_ref, s1_ref, p0_ref, p1_ref, alpha0_ref, alpha1_ref, va_ref, m_ref, acc_ref):
    half = GROUP_KEYS // 2
    first = grp * ROW_TILES_PER_GROUP
    last = N_ROW_TILES - 1
    va_new = va_ref.at[grp % 2]
    va_old = va_ref.at[1 - grp % 2]
    va_new[:, :HEAD_DIM] = group_values()

    @pl.when(grp < 2)
    def _():
        va_new[:, HEAD_DIM:] = jnp.ones((GROUP_KEYS, HEAD_DIM), BF16)

    def scores(tile, s_ref, n_keys=GROUP_KEYS, biased=True):
        raw = lax.dot_general(qa_ref[_tile_rows(tile), :], group_keys(n_keys), _NT,
                              preferred_element_type=F32)
        if not biased:
            s_ref[:, :n_keys] = raw
            return
        for r in range(ROW_TILE // TILE):
            for j in range(n_keys // TILE):
                bias = bias_ref[0, _table_index(tile * (ROW_TILE // TILE) + r - (grp * GROUP + j))]
                s_ref[r * TILE:(r + 1) * TILE, j * TILE:(j + 1) * TILE] = (
                    raw[r * TILE:(r + 1) * TILE, j * TILE:(j + 1) * TILE] + bias)

    def softmax(tile, s_ref, p_ref, alpha_ref, n_keys=GROUP_KEYS):
        for c in range(ROW_TILE // SOFTMAX_ROWS):
            chunk = slice(c * SOFTMAX_ROWS, (c + 1) * SOFTMAX_ROWS)
            rows = pl.ds(pl.multiple_of(tile * ROW_TILE + c * SOFTMAX_ROWS, SOFTMAX_ROWS), SOFTMAX_ROWS)
            s = s_ref[chunk, :n_keys]
            m_prev = m_ref[rows, :]
            m_new = jnp.maximum(m_prev, jnp.max(s, axis=1, keepdims=True))
            alpha_ref[chunk, :] = jnp.exp2(m_prev - m_new)
            m_ref[rows, :] = m_new
            p_ref[chunk, :n_keys] = jnp.exp2(s - m_new).astype(BF16)

    def accumulate(tile, p_ref, alpha_ref, va, n_keys=GROUP_KEYS):
        rows = _tile_rows(tile)
        acc_ref[rows, :] = (alpha_ref[...] * acc_ref[rows, :]
                            + jnp.dot(p_ref[:, :n_keys], va[:n_keys, :], preferred_element_type=F32))

    def enter(with_pending):
        scores(first, s0_ref, half)
        if with_pending:
            softmax(last, s1_ref, p1_ref, alpha1_ref)
            accumulate(last - 1, p0_ref, alpha0_ref, va_old)
        scores(first + 1, s1_ref)
        softmax(first, s0_ref, p0_ref, alpha0_ref, half)
        if with_pending:
            accumulate(last, p1_ref, alpha1_ref, va_old)

    def second_pair():
        scores(first + 2, s0_ref)
        softmax(first + 1, s1_ref, p1_ref, alpha1_ref)
        accumulate(first, p0_ref, alpha0_ref, va_new, half)
        scores(first + 3, s1_ref)
        softmax(first + 2, s0_ref, p0_ref, alpha0_ref)
        accumulate(first + 1, p1_ref, alpha1_ref, va_new)

    def pair(t, first_biased=False, values_first=False):
        tile = first + 3 + 2 * t
        if values_first:
            accumulate(tile - 1, p0_ref, alpha0_ref, va_new)
        scores(tile + 1, s0_ref, biased=first_biased)
        softmax(tile, s1_ref, p1_ref, alpha1_ref)
        if not values_first:
            accumulate(tile - 1, p0_ref, alpha0_ref, va_new)
        else:
            accumulate(tile, p1_ref, alpha1_ref, va_new)
        scores(tile + 2, s1_ref, biased=False)
        softmax(tile + 1, s0_ref, p0_ref, alpha0_ref)
        if not values_first:
            accumulate(tile, p1_ref, alpha1_ref, va_new)

    @pl.when(grp == 0)
    def _():
        enter(False)
        second_pair()

    @pl.when((grp > 0) & (grp < N_GROUPS - 1))
    def _():
        enter(True)
        second_pair()

    @pl.when(grp == N_GROUPS - 1)
    def _():
        enter(True)
        softmax(last, s1_ref, p1_ref, alpha1_ref)
        accumulate(last - 1, p0_ref, alpha0_ref, va_new, half)
        accumulate(last, p1_ref, alpha1_ref, va_new)

    assert BIAS_FREE_FROM_ROW_TILE == 5
    n_pairs = jnp.maximum(last - first - 3, 0) // 2

    @pl.when(n_pairs > 0)
    def _():
        pair(0, first_biased=True)

    @pl.when((n_pairs > 0) & (n_pairs % 2 == 0))
    def _():
        pair(1)

    def two_pairs(u, carry):
        t = 2 - n_pairs % 2 + 2 * u
        pair(t, values_first=True)
        pair(t + 1, values_first=True)
        return carry

    lax.fori_loop(0, jnp.maximum(n_pairs - 1, 0) // 2, two_pairs, 0)


def _reset_state(rows, m_ref, acc_ref):
    m_ref[rows, :] = jnp.full((rows.size, 1), NEG_INF, F32)
    acc_ref[rows, :] = jnp.zeros((rows.size, 2 * HEAD_DIM), F32)


def _attention_of(acc):
    return acc[:, :HEAD_DIM] / acc[:, HEAD_DIM:]


GATE_QUERIES = 8 * TILE


def _moba_gating(q_ref, km_ref, qa_ref, m_ref, acc_ref):
    blk = lax.broadcasted_iota(jnp.int32, (N_KV_BLOCKS, GATE_QUERIES), 0)
    blkf = blk.astype(F32)
    tile_in_step = lax.broadcasted_iota(jnp.int32, (N_KV_BLOCKS, GATE_QUERIES), 1) // TILE

    def gate_step(step, carry):
        rows = pl.ds(pl.multiple_of(step * GATE_QUERIES, GATE_QUERIES), GATE_QUERIES)
        _reset_state(rows, m_ref, acc_ref)
        q = q_ref[rows, :]
        gate = lax.dot_general(km_ref[...].astype(BF16), q, _NT, preferred_element_type=F32)
        tile = step * (GATE_QUERIES // TILE) + tile_in_step
        past = blk < tile
        g = jnp.where(past, gate, NEG_INF)
        madd = jnp.full(gate.shape, NEG_INF, F32)
        for _ in range(MOBA_TOPK):
            mx = jnp.max(g, axis=0, keepdims=True)
            first = jnp.min(jnp.where(g == mx, blkf, float(N_KV_BLOCKS)), axis=0, keepdims=True)
            hit = blkf == first
            madd = jnp.where(hit & past, 0.0, madd)
            g = jnp.where(hit, -jnp.inf, g)
        madd = jnp.where(blk == tile, 0.0, madd)
        madd = jnp.concatenate([madd, jnp.zeros((HEAD_DIM - N_KV_BLOCKS, GATE_QUERIES), F32)], axis=0)
        qa_ref[rows, :HEAD_DIM] = q
        qa_ref[rows, HEAD_DIM:] = madd.T.astype(BF16)
        return carry

    lax.fori_loop(0, SEQ // GATE_QUERIES, gate_step, 0, unroll=2)


def _step_rows(group_in_step, n_rows=GROUP_KEYS):
    return pl.ds(pl.multiple_of(group_in_step * GROUP_KEYS, GROUP_KEYS), n_rows)


def _moba_kernel(q_ref, k_ref, v_ref, km_ref, bias_ref, o_ref, qa_ref, ka_ref, *flash_refs):
    step = pl.program_id(2)
    m_ref, acc_ref = flash_refs[-2:]

    @pl.when(step == 0)
    def _():
        _moba_gating(q_ref, km_ref, qa_ref, m_ref, acc_ref)

    lane = lax.broadcasted_iota(jnp.int32, (GROUP_KEYS, HEAD_DIM), 1)
    block_in_group = lax.broadcasted_iota(jnp.int32, (GROUP_KEYS, HEAD_DIM), 0) // TILE

    def group(i, carry):
        grp = step * GROUPS_PER_STEP + i
        ka_ref[:, :HEAD_DIM] = k_ref[_step_rows(i), :]
        ka_ref[:, HEAD_DIM:] = jnp.where(lane == grp * GROUP + block_in_group, 1.0, 0.0).astype(BF16)
        _flash_sweep(grp, qa_ref, lambda n: ka_ref[:n, :], lambda: v_ref[_step_rows(i), :], bias_ref,
                     *flash_refs)
        return carry

    lax.fori_loop(0, GROUPS_PER_STEP, group, 0)

    @pl.when(step == N_STEPS - 1)
    def _():
        def finish(tile, carry):
            rows = _tile_rows(tile)
            o_ref[rows, :] = _attention_of(acc_ref[rows, :]).astype(BF16)
            return carry

        lax.fori_loop(0, N_ROW_TILES, finish, 0)


def _moba_attention(qkv, kmean, bias):
    return pl.pallas_call(
        _moba_kernel,
        grid=(BATCH, N_HEADS, N_STEPS),
        in_specs=[
            pl.BlockSpec((SEQ, HEAD_DIM), lambda b, h, g: (b, h)),
            pl.BlockSpec((STEP_KEYS, HEAD_DIM), lambda b, h, g: (b * N_STEPS + g, N_HEADS + h)),
            pl.BlockSpec((STEP_KEYS, HEAD_DIM), lambda b, h, g: (b * N_STEPS + g, 2 * N_HEADS + h)),
            pl.BlockSpec((N_KV_BLOCKS, HEAD_DIM), lambda b, h, g: (b, h)),
            pl.BlockSpec((1, N_TABLE_TILES, TILE, TILE), lambda b, h, g: (h, 0, 0, 0)),
        ],
        out_specs=pl.BlockSpec((SEQ, HEAD_DIM), lambda b, h, g: (b, h)),
        out_shape=jax.ShapeDtypeStruct((ROWS, D_MODEL), BF16),
        scratch_shapes=[
            pltpu.VMEM((SEQ, 2 * HEAD_DIM), BF16),
            pltpu.VMEM((GROUP_KEYS, 2 * HEAD_DIM), BF16),
        ] + _flash_scratch(),
        compiler_params=pltpu.CompilerParams(
            dimension_semantics=("arbitrary", "arbitrary", "arbitrary"), vmem_limit_bytes=VMEM_LIMIT),
        name="moba_attn",
    )(qkv, qkv, qkv, kmean, bias)


def _diff_kernel(q_ref, k_ref, v_ref, bias_ref, lam_ref, g_ref, o_ref, qa_ref, map0_ref, *flash_refs,
                 lambda_init):
    which = pl.program_id(2)
    step = pl.program_id(3)
    m_ref, acc_ref = flash_refs[-2:]

    @pl.when(step == 0)
    def _():
        lane_map = lax.broadcasted_iota(jnp.int32, (ROW_TILE, HEAD_DIM), 1) // DIFF_HALF

        def split(tile, carry):
            rows = _tile_rows(tile)
            _reset_state(rows, m_ref, acc_ref)
            q = q_ref[rows, :]
            qa_ref[rows, :] = jnp.where(lane_map == which, q, jnp.zeros_like(q))
            return carry

        lax.fori_loop(0, N_ROW_TILES, split, 0)

    def group(i, carry):
        _flash_sweep(step * GROUPS_PER_STEP + i, qa_ref, lambda n: k_ref[_step_rows(i, n), :],
                     lambda: v_ref[_step_rows(i), :], bias_ref, *flash_refs)
        return carry

    lax.fori_loop(0, GROUPS_PER_STEP, group, 0)

    @pl.when((step == N_STEPS - 1) & (which == 0))
    def _():
        def keep(tile, carry):
            rows = _tile_rows(tile)
            map0_ref[rows, :] = _attention_of(acc_ref[rows, :])
            return carry

        lax.fori_loop(0, N_ROW_TILES, keep, 0)

    @pl.when((step == N_STEPS - 1) & (which == 1))
    def _():
        lam = lam_ref[...]
        lam_full = (jnp.exp(jnp.sum(lam[0:1] * lam[1:2], axis=1, keepdims=True))
                    - jnp.exp(jnp.sum(lam[2:3] * lam[3:4], axis=1, keepdims=True)) + lambda_init)

        def finish(tile, carry):
            rows = _tile_rows(tile)
            o = map0_ref[rows, :] - lam_full * _attention_of(acc_ref[rows, :])
            o = o * lax.rsqrt(jnp.mean(o * o, axis=-1, keepdims=True) + RMS_EPS) * g_ref[...]
            o_ref[rows, :] = (o * (1.0 - lambda_init)).astype(BF16)
            return carry

        lax.fori_loop(0, N_ROW_TILES, finish, 0, unroll=2)


def _diff_attention(qkv, bias, lam, subln_g, layer_idx):
    lambda_init = 0.8 - 0.6 * math.exp(-0.3 * layer_idx)
    return pl.pallas_call(
        functools.partial(_diff_kernel, lambda_init=lambda_init),
        grid=(BATCH, N_HEADS, 2, N_STEPS),
        in_specs=[
            pl.BlockSpec((SEQ, HEAD_DIM), lambda b, h, c, g: (b, h)),
            pl.BlockSpec((STEP_KEYS, HEAD_DIM), lambda b, h, c, g: (b * N_STEPS + g, N_HEADS + h)),
            pl.BlockSpec((STEP_KEYS, HEAD_DIM), lambda b, h, c, g: (b * N_STEPS + g, 2 * N_HEADS + h)),
            pl.BlockSpec((1, N_TABLE_TILES, TILE, TILE), lambda b, h, c, g: (h, 0, 0, 0)),
            pl.BlockSpec((4, DIFF_HALF), lambda b, h, c, g: (0, 0)),
            pl.BlockSpec((1, HEAD_DIM), lambda b, h, c, g: (0, 0)),
        ],
        out_specs=pl.BlockSpec((SEQ, HEAD_DIM), lambda b, h, c, g: (b, h)),
        out_shape=jax.ShapeDtypeStruct((ROWS, D_MODEL), BF16),
        scratch_shapes=[
            pltpu.VMEM((SEQ, HEAD_DIM), BF16),
            pltpu.VMEM((SEQ, HEAD_DIM), F32),
        ] + _flash_scratch(),
        compiler_params=pltpu.CompilerParams(
            dimension_semantics=("arbitrary", "arbitrary", "arbitrary", "arbitrary"),
            vmem_limit_bytes=VMEM_LIMIT),
        name="diff_attn",
    )(qkv, qkv, qkv, bias, lam, subln_g.reshape(1, HEAD_DIM))


FFN_TM = 1024
FFN_TF = 256


def _rms(y, g):
    return y * lax.rsqrt(jnp.mean(y * y, axis=-1, keepdims=True) + RMS_EPS) * g


def _mixer_out_ffn_kernel(o_ref, x_ref, wa_ref, gains_ref, mod_ref, wg_ref, wu_ref, wo_ref, out_ref,
                          xn_ref, h_ref, acc_ref):
    c = pl.program_id(1)
    last = pl.num_programs(1) - 1

    def chunk(h):
        gp = jnp.dot(h, wg_ref[0].astype(BF16), preferred_element_type=F32)
        up = jnp.dot(h, wu_ref[0].astype(BF16), preferred_element_type=F32)
        act = (gp / (1.0 + jnp.exp(-gp)) * up).astype(BF16)
        return jnp.dot(act, wo_ref[0].astype(BF16), preferred_element_type=F32)

    @pl.when(c == 0)
    def _():
        y = jnp.dot(o_ref[...], wa_ref[...], preferred_element_type=F32)
        xn = x_ref[...] + _mod(mod_ref, MOD_GATE_MIXER) * _rms(y, _gain(gains_ref, GAIN_MIXER_OUT))
        xn_ref[...] = xn
        h = _norm_mod(xn, _gain(gains_ref, GAIN_FFN_IN), _mod(mod_ref, MOD_SCALE_FFN),
                      _mod(mod_ref, MOD_SHIFT_FFN)).astype(BF16)
        h_ref[...] = h
        acc_ref[...] = chunk(h)

    @pl.when((c > 0) & (c < last))
    def _():
        acc_ref[...] += chunk(h_ref[...])

    @pl.when(c == last)
    def _():
        y = acc_ref[...] + chunk(h_ref[...])
        out_ref[...] = xn_ref[...] + _mod(mod_ref, MOD_GATE_FFN) * _rms(y, _gain(gains_ref, GAIN_FFN_OUT))


def _mixer_out_ffn(o, x2, w_attn_out_bf16, gains, mod, layer, w_in_layers, w_out_layers):
    n_chunks = D_FF // FFN_TF
    return pl.pallas_call(
        _mixer_out_ffn_kernel,
        grid=(ROWS // FFN_TM, n_chunks),
        in_specs=[
            pl.BlockSpec((FFN_TM, D_MODEL), lambda i, c: (i, 0)),
            pl.BlockSpec((FFN_TM, D_MODEL), lambda i, c: (i, 0)),
            pl.BlockSpec((D_MODEL, D_MODEL), lambda i, c: (0, 0), pipeline_mode=pl.Buffered(1)),
            *_layer_vector_specs(layer, FFN_TM),
            pl.BlockSpec((1, D_MODEL, FFN_TF), lambda i, c: (layer, 0, c)),
            pl.BlockSpec((1, D_MODEL, FFN_TF), lambda i, c: (layer, 0, n_chunks + c)),
            pl.BlockSpec((1, FFN_TF, D_MODEL), lambda i, c: (layer, c, 0)),
        ],
        out_specs=pl.BlockSpec((FFN_TM, D_MODEL), lambda i, c: (i, 0)),
        out_shape=jax.ShapeDtypeStruct((ROWS, D_MODEL), F32),
        scratch_shapes=[
            pltpu.VMEM((FFN_TM, D_MODEL), F32),
            pltpu.VMEM((FFN_TM, D_MODEL), BF16),
            pltpu.VMEM((FFN_TM, D_MODEL), F32),
        ],
        compiler_params=pltpu.CompilerParams(
            dimension_semantics=("parallel", "arbitrary"), vmem_limit_bytes=VMEM_LIMIT),
        name="mixer_out_ffn",
    )(o, x2, w_attn_out_bf16, gains, mod, w_in_layers, w_in_layers, w_out_layers)


def kernel(x, c, rel_bias, ada_w, ada_b, norm_g, moba_w_qkv, moba_w_o, diff_w_qkv, diff_w_o, diff_lambda,
           diff_subln_g, ffn_w_in, ffn_w_out):
    x2 = x.reshape(ROWS, D_MODEL)
    mod = _adaln_mod(c, ada_w, ada_b).reshape(DEPTH, BATCH, 6, D_MODEL)
    bias = _bias_tiles(rel_bias)
    for i in range(DEPTH):
        if i % 2 == 0:
            qkv, kmean = _qkv_proj(x2, norm_g, mod, i, moba_w_qkv, i // 2, HEAD_DIM ** -0.5 * LOG2_E, True)
            o = _moba_attention(qkv, kmean.reshape(BATCH * N_KV_BLOCKS, D_MODEL), bias)
            w_o = moba_w_o
        else:
            (qkv,) = _qkv_proj(x2, norm_g, mod, i, diff_w_qkv, i // 2, DIFF_HALF ** -0.5 * LOG2_E, False)
            o = _diff_attention(qkv, bias, diff_lambda[i // 2], diff_subln_g[i // 2], i)
            w_o = diff_w_o
        x2 = _mixer_out_ffn(o, x2, w_o[i // 2].astype(BF16), norm_g, mod, i, ffn_w_in, ffn_w_out)
    return x2.reshape(BATCH, SEQ, D_MODEL)
```

```python
import functools
import math

import numpy as np
import jax
import jax.numpy as jnp
from jax import lax
from jax.experimental import pallas as pl
from jax.experimental.pallas import tpu as pltpu

D_MODEL = 1024
BATCH = 2
SEQ = 8192
DEPTH = 2
N_HEADS = 8
HEAD_DIM = D_MODEL // N_HEADS
DIFF_HALF = HEAD_DIM // 2
MOBA_BLOCK = 256
MOBA_TOPK = 3
D_FF = 2816
N_BUCKETS = 32
MAX_EXACT = N_BUCKETS // 2
MAX_DISTANCE = 2048
RMS_EPS = 1e-6
NEG_INF = -1e30

ROWS = BATCH * SEQ
N_KV_BLOCKS = SEQ // MOBA_BLOCK
TILE = MOBA_BLOCK
VMEM_LIMIT = 48 * 1024 * 1024

F32 = jnp.float32
BF16 = jnp.bfloat16


def _bucket_of_distance():
    n = np.arange(MAX_DISTANCE + 1)
    nf = np.maximum(n, 1).astype(np.float64)
    val = np.log(nf / MAX_EXACT) / math.log(MAX_DISTANCE / MAX_EXACT) * (N_BUCKETS - MAX_EXACT)
    frac = np.abs(val - np.round(val))
    assert np.all((frac > 5e-5) | (n <= MAX_EXACT) | (n == MAX_DISTANCE))
    large = np.minimum(MAX_EXACT + np.floor(val + 1e-9).astype(np.int64), N_BUCKETS - 1)
    return np.where(n < MAX_EXACT, n, large)


_BUCKETS = _bucket_of_distance()
FAR_DISTANCE = int(np.min(np.nonzero(_BUCKETS == N_BUCKETS - 1)[0]))
N_BIAS_TILES = (FAR_DISTANCE + TILE - 1) // TILE + 1
N_TABLE_TILES = N_BIAS_TILES + 2


def _bias_bucket_tiles():
    i = np.arange(TILE)[:, None]
    j = np.arange(TILE)[None, :]
    out = []
    for t in range(-1, N_BIAS_TILES + 1):
        rel = t * TILE + i - j
        out.append(np.where(rel >= 0, _BUCKETS[np.clip(rel, 0, MAX_DISTANCE)], -1))
    return np.stack(out).astype(np.int32)


def _table_index(tiles_behind):
    return jnp.clip(tiles_behind, -1, N_BIAS_TILES) + 1


MOD_TN = 1024


def _mod_kernel(ct_ref, w_ref, b_ref, o_ref):
    ct = ct_ref[...]
    cact = ct / (1.0 + jnp.exp(-ct))
    w = w_ref[0]
    for b in range(BATCH):
        row = jnp.sum(w * cact[:, b:b + 1], axis=0, keepdims=True)
        o_ref[0, b:b + 1, :] = row + b_ref[0]


def _adaln_mod(c, ada_w, ada_b):
    n_out = ada_w.shape[-1]
    return pl.pallas_call(
        _mod_kernel,
        grid=(DEPTH, n_out // MOD_TN),
        in_specs=[
            pl.BlockSpec((D_MODEL, BATCH), lambda i, n: (0, 0)),
            pl.BlockSpec((1, D_MODEL, MOD_TN), lambda i, n: (i, 0, n)),
            pl.BlockSpec((1, 1, MOD_TN), lambda i, n: (i, 0, n)),
        ],
        out_specs=pl.BlockSpec((1, BATCH, MOD_TN), lambda i, n: (i, 0, n)),
        out_shape=jax.ShapeDtypeStruct((DEPTH, BATCH, n_out), F32),
        name="adaln_mod",
    )(c.T, ada_w, ada_b.reshape(DEPTH, 1, n_out))


LOG2_E = math.log2(math.e)


def _bias_kernel(tab_ref, bkt_ref, o_ref, *, buckets_present):
    h = pl.program_id(0)
    far = tab_ref[N_BUCKETS - 1, h]
    for t, present in enumerate(buckets_present):
        bkt = bkt_ref[t]
        acc = jnp.full(bkt.shape, NEG_INF, F32)
        for b in present:
            acc = jnp.where(bkt == b, (tab_ref[b, h] - far) * LOG2_E, acc)
        o_ref[0, t] = acc


def _bias_tiles(rel_bias):
    bkt = _bias_bucket_tiles()
    buckets_present = tuple(tuple(int(b) for b in np.unique(tile) if b >= 0) for tile in bkt)
    return pl.pallas_call(
        functools.partial(_bias_kernel, buckets_present=buckets_present),
        grid=(N_HEADS,),
        in_specs=[
            pl.BlockSpec(memory_space=pltpu.SMEM),
            pl.BlockSpec((N_TABLE_TILES, TILE, TILE), lambda h: (0, 0, 0)),
        ],
        out_specs=pl.BlockSpec((1, N_TABLE_TILES, TILE, TILE), lambda h: (h, 0, 0, 0)),
        out_shape=jax.ShapeDtypeStruct((N_HEADS, N_TABLE_TILES, TILE, TILE), F32),
        name="bias_tiles",
    )(rel_bias, jnp.asarray(bkt))


QKV_TM = 1024


def _norm_mod(x, g, sc, sh):
    y = x * lax.rsqrt(jnp.mean(x * x, axis=-1, keepdims=True) + RMS_EPS)
    return (y * g) * (1.0 + sc) + sh


def _cast_weight_once(w_ref, wb_ref):
    @pl.when(pl.program_id(0) == 0)
    def _():
        wb_ref[...] = w_ref[0].astype(BF16)


GAIN_MIXER_IN, GAIN_MIXER_OUT, GAIN_FFN_IN, GAIN_FFN_OUT = range(4)
MOD_SHIFT_MIXER, MOD_SCALE_MIXER, MOD_GATE_MIXER, MOD_SHIFT_FFN, MOD_SCALE_FFN, MOD_GATE_FFN = range(6)


def _gain(gains_ref, which):
    return gains_ref[0, which:which + 1, :]


def _mod(mod_ref, which):
    return mod_ref[0, 0, which:which + 1, :]


def _layer_vector_specs(layer, rows_per_tile):
    tiles_per_batch = SEQ // rows_per_tile
    return (pl.BlockSpec((1, 4, D_MODEL), lambda i, *_: (layer, 0, 0)),
            pl.BlockSpec((1, 1, 6, D_MODEL), lambda i, *_: (layer, i // tiles_per_batch, 0, 0)))


def _qkv_kernel(x_ref, gains_ref, mod_ref, w_ref, o_ref, *rest, q_scale):
    *km_ref, wb_ref = rest
    _cast_weight_once(w_ref, wb_ref)
    hb = _norm_mod(x_ref[...], _gain(gains_ref, GAIN_MIXER_IN), _mod(mod_ref, MOD_SCALE_MIXER),
                   _mod(mod_ref, MOD_SHIFT_MIXER)).astype(BF16)
    for n in range(3):
        r = jnp.dot(hb, wb_ref[:, n * D_MODEL:(n + 1) * D_MODEL], preferred_element_type=F32)
        if n == 0:
            r = r * q_scale
        if n == 1 and km_ref:
            km_ref[0][0] = jnp.mean(r.reshape(QKV_TM // MOBA_BLOCK, MOBA_BLOCK, D_MODEL), axis=1)
        o_ref[:, n * D_MODEL:(n + 1) * D_MODEL] = r.astype(BF16)


def _qkv_proj(x2, gains, mod, layer, w_layers, w_layer, q_scale, with_kmean):
    n_tiles = ROWS // QKV_TM
    out_shape = [jax.ShapeDtypeStruct((ROWS, 3 * D_MODEL), BF16)]
    out_specs = [pl.BlockSpec((QKV_TM, 3 * D_MODEL), lambda i: (i, 0))]
    if with_kmean:
        per_tile = QKV_TM // MOBA_BLOCK
        out_shape.append(jax.ShapeDtypeStruct((n_tiles, per_tile, D_MODEL), F32))
        out_specs.append(pl.BlockSpec((1, per_tile, D_MODEL), lambda i: (i, 0, 0)))
    return pl.pallas_call(
        functools.partial(_qkv_kernel, q_scale=q_scale),
        grid=(n_tiles,),
        in_specs=[
            pl.BlockSpec((QKV_TM, D_MODEL), lambda i: (i, 0)),
            *_layer_vector_specs(layer, QKV_TM),
            pl.BlockSpec((1, D_MODEL, 3 * D_MODEL), lambda i: (w_layer, 0, 0), pipeline_mode=pl.Buffered(1)),
        ],
        out_specs=out_specs,
        out_shape=out_shape,
        scratch_shapes=[pltpu.VMEM((D_MODEL, 3 * D_MODEL), BF16)],
        compiler_params=pltpu.CompilerParams(
            dimension_semantics=("arbitrary",), vmem_limit_bytes=VMEM_LIMIT),
        name="qkv_moba" if with_kmean else "qkv_diff",
    )(x2, gains, mod, w_layers)


_NT = (((1,), (1,)), ((), ()))
GROUP = 4
GROUP_KEYS = GROUP * TILE
N_GROUPS = SEQ // GROUP_KEYS
GROUPS_PER_STEP = 4
N_STEPS = N_GROUPS // GROUPS_PER_STEP
STEP_KEYS = GROUPS_PER_STEP * GROUP_KEYS
ROW_TILE = 2 * TILE
N_ROW_TILES = SEQ // ROW_TILE
ROW_TILES_PER_GROUP = GROUP_KEYS // ROW_TILE
SOFTMAX_ROWS = 64
BIAS_FREE_FROM_ROW_TILE = -(-(N_BIAS_TILES + GROUP - 1) // (ROW_TILE // TILE))


def _flash_scratch():
    scores = pltpu.VMEM((ROW_TILE, GROUP_KEYS), F32)
    probs = pltpu.VMEM((ROW_TILE, GROUP_KEYS), BF16)
    rescale = pltpu.VMEM((ROW_TILE, 1), F32)
    values = pltpu.VMEM((2, GROUP_KEYS, 2 * HEAD_DIM), BF16)
    running_max = pltpu.VMEM((SEQ, 1), F32)
    acc = pltpu.VMEM((SEQ, 2 * HEAD_DIM), F32)
    return [scores, scores, probs, probs, rescale, rescale, values, running_max, acc]


def _tile_rows(tile):
    return pl.ds(pl.multiple_of(tile * ROW_TILE, ROW_TILE), ROW_TILE)


def _flash_sweep(grp, qa_ref, group_keys, group_values, bias_ref,
                 s0_ref, s1_ref, p0_ref, p1_ref, alpha0_ref, alpha1_ref, va_ref, m_ref, acc_ref):
    half = GROUP_KEYS // 2
    first = grp * ROW_TILES_PER_GROUP
    last = N_ROW_TILES - 1
    va_new = va_ref.at[grp % 2]
    va_old = va_ref.at[1 - grp % 2]
    va_new[:, :HEAD_DIM] = group_values()

    @pl.when(grp < 2)
    def _():
        va_new[:, HEAD_DIM:] = jnp.ones((GROUP_KEYS, HEAD_DIM), BF16)

    def scores(tile, s_ref, n_keys=GROUP_KEYS, biased=True):
        raw = lax.dot_general(qa_ref[_tile_rows(tile), :], group_keys(n_keys), _NT,
                              preferred_element_type=F32)
        if not biased:
            s_ref[:, :n_keys] = raw
            return
        for r in range(ROW_TILE // TILE):
            for j in range(n_keys // TILE):
                bias = bias_ref[0, _table_index(tile * (ROW_TILE // TILE) + r - (grp * GROUP + j))]
                s_ref[r * TILE:(r + 1) * TILE, j * TILE:(j + 1) * TILE] = (
                    raw[r * TILE:(r + 1) * TILE, j * TILE:(j + 1) * TILE] + bias)

    def softmax(tile, s_ref, p_ref, alpha_ref, n_keys=GROUP_KEYS):
        for c in range(ROW_TILE // SOFTMAX_ROWS):
            chunk = slice(c * SOFTMAX_ROWS, (c + 1) * SOFTMAX_ROWS)
            rows = pl.ds(pl.multiple_of(tile * ROW_TILE + c * SOFTMAX_ROWS, SOFTMAX_ROWS), SOFTMAX_ROWS)
            s = s_ref[chunk, :n_keys]
            m_prev = m_ref[rows, :]
            m_new = jnp.maximum(m_prev, jnp.max(s, axis=1, keepdims=True))
            alpha_ref[chunk, :] = jnp.exp2(m_prev - m_new)
            m_ref[rows, :] = m_new
            p_ref[chunk, :n_keys] = jnp.exp2(s - m_new).astype(BF16)

    def accumulate(tile, p_ref, alpha_ref, va, n_keys=GROUP_KEYS):
        rows = _tile_rows(tile)
        acc_ref[rows, :] = (alpha_ref[...] * acc_ref[rows, :]
                            + jnp.dot(p_ref[:, :n_keys], va[:n_keys, :], preferred_element_type=F32))

    def enter(with_pending):
        scores(first, s0_ref, half)
        if with_pending:
            softmax(last, s1_ref, p1_ref, alpha1_ref)
            accumulate(last - 1, p0_ref, alpha0_ref, va_old)
        scores(first + 1, s1_ref)
        softmax(first, s0_ref, p0_ref, alpha0_ref, half)
        if with_pending:
            accumulate(last, p1_ref, alpha1_ref, va_old)

    def second_pair():
        scores(first + 2, s0_ref)
        softmax(first + 1, s1_ref, p1_ref, alpha1_ref)
        accumulate(first, p0_ref, alpha0_ref, va_new, half)
        scores(first + 3, s1_ref)
        softmax(first + 2, s0_ref, p0_ref, alpha0_ref)
        accumulate(first + 1, p1_ref, alpha1_ref, va_new)

    def pair(t, first_biased=False, values_first=False):
        tile = first + 3 + 2 * t
        if values_first:
            accumulate(tile - 1, p0_ref, alpha0_ref, va_new)
        scores(tile + 1, s0_ref, biased=first_biased)
        softmax(tile, s1_ref, p1_ref, alpha1_ref)
        if not values_first:
            accumulate(tile - 1, p0_ref, alpha0_ref, va_new)
        else:
            accumulate(tile, p1_ref, alpha1_ref, va_new)
        scores(tile + 2, s1_ref, biased=False)
        softmax(tile + 1, s0_ref, p0_ref, alpha0_ref)
        if not values_first:
            accumulate(tile, p1_ref, alpha1_ref, va_new)

    @pl.when(grp == 0)
    def _():
        enter(False)
        second_pair()

    @pl.when((grp > 0) & (grp < N_GROUPS - 1))
    def _():
        enter(True)
        second_pair()

    @pl.when(grp == N_GROUPS - 1)
    def _():
        enter(True)
        softmax(last, s1_ref, p1_ref, alpha1_ref)
        accumulate(last - 1, p0_ref, alpha0_ref, va_new, half)
        accumulate(last, p1_ref, alpha1_ref, va_new)

    assert BIAS_FREE_FROM_ROW_TILE == 5
    n_pairs = jnp.maximum(last - first - 3, 0) // 2

    @pl.when(n_pairs % 2 == 1)
    def _():
        pair(0, first_biased=True)

    @pl.when((n_pairs > 0) & (n_pairs % 2 == 0))
    def _():
        pair(0, first_biased=True, values_first=True)
        pair(1, values_first=True)

    def two_pairs(u, carry):
        t = 2 - n_pairs % 2 + 2 * u
        pair(t, values_first=True)
        pair(t + 1, values_first=True)
        return carry

    lax.fori_loop(0, jnp.maximum(n_pairs - 1, 0) // 2, two_pairs, 0)


def _reset_state(rows, m_ref, acc_ref):
    m_ref[rows, :] = jnp.full((rows.size, 1), NEG_INF, F32)
    acc_ref[rows, :] = jnp.zeros((rows.size, 2 * HEAD_DIM), F32)


def _attention_of(acc):
    return acc[:, :HEAD_DIM] / acc[:, HEAD_DIM:]


GATE_QUERIES = 8 * TILE


def _moba_gating(q_ref, km_ref, qa_ref, m_ref, acc_ref):
    blk = lax.broadcasted_iota(jnp.int32, (N_KV_BLOCKS, GATE_QUERIES), 0)
    blkf = blk.astype(F32)
    tile_in_step = lax.broadcasted_iota(jnp.int32, (N_KV_BLOCKS, GATE_QUERIES), 1) // TILE

    def gate_step(step, carry):
        rows = pl.ds(pl.multiple_of(step * GATE_QUERIES, GATE_QUERIES), GATE_QUERIES)
        _reset_state(rows, m_ref, acc_ref)
        q = q_ref[rows, :]
        gate = lax.dot_general(km_ref[...].astype(BF16), q, _NT, preferred_element_type=F32)
        tile = step * (GATE_QUERIES // TILE) + tile_in_step
        past = blk < tile
        g = jnp.where(past, gate, NEG_INF)
        madd = jnp.full(gate.shape, NEG_INF, F32)
        for _ in range(MOBA_TOPK):
            mx = jnp.max(g, axis=0, keepdims=True)
            first = jnp.min(jnp.where(g == mx, blkf, float(N_KV_BLOCKS)), axis=0, keepdims=True)
            hit = blkf == first
            madd = jnp.where(hit & past, 0.0, madd)
            g = jnp.where(hit, -jnp.inf, g)
        madd = jnp.where(blk == tile, 0.0, madd)
        madd = jnp.concatenate([madd, jnp.zeros((HEAD_DIM - N_KV_BLOCKS, GATE_QUERIES), F32)], axis=0)
        qa_ref[rows, :HEAD_DIM] = q
        qa_ref[rows, HEAD_DIM:] = madd.T.astype(BF16)
        return carry

    lax.fori_loop(0, SEQ // GATE_QUERIES, gate_step, 0, unroll=2)


def _step_rows(group_in_step, n_rows=GROUP_KEYS):
    return pl.ds(pl.multiple_of(group_in_step * GROUP_KEYS, GROUP_KEYS), n_rows)


def _moba_kernel(q_ref, k_ref, v_ref, km_ref, bias_ref, o_ref, qa_ref, ka_ref, *flash_refs):
    step = pl.program_id(2)
    m_ref, acc_ref = flash_refs[-2:]

    @pl.when(step == 0)
    def _():
        _moba_gating(q_ref, km_ref, qa_ref, m_ref, acc_ref)

    lane = lax.broadcasted_iota(jnp.int32, (GROUP_KEYS, HEAD_DIM), 1)
    block_in_group = lax.broadcasted_iota(jnp.int32, (GROUP_KEYS, HEAD_DIM), 0) // TILE

    def group(i, carry):
        grp = step * GROUPS_PER_STEP + i
        ka_ref[:, :HEAD_DIM] = k_ref[_step_rows(i), :]
        ka_ref[:, HEAD_DIM:] = jnp.where(lane == grp * GROUP + block_in_group, 1.0, 0.0).astype(BF16)
        _flash_sweep(grp, qa_ref, lambda n: ka_ref[:n, :], lambda: v_ref[_step_rows(i), :], bias_ref,
                     *flash_refs)
        return carry

    lax.fori_loop(0, GROUPS_PER_STEP, group, 0)

    @pl.when(step == N_STEPS - 1)
    def _():
        def finish(tile, carry):
            rows = _tile_rows(tile)
            o_ref[rows, :] = _attention_of(acc_ref[rows, :]).astype(BF16)
            return carry

        lax.fori_loop(0, N_ROW_TILES, finish, 0)


def _moba_attention(qkv, kmean, bias):
    return pl.pallas_call(
        _moba_kernel,
        grid=(BATCH, N_HEADS, N_STEPS),
        in_specs=[
            pl.BlockSpec((SEQ, HEAD_DIM), lambda b, h, g: (b, h)),
            pl.BlockSpec((STEP_KEYS, HEAD_DIM), lambda b, h, g: (b * N_STEPS + g, N_HEADS + h)),
            pl.BlockSpec((STEP_KEYS, HEAD_DIM), lambda b, h, g: (b * N_STEPS + g, 2 * N_HEADS + h)),
            pl.BlockSpec((N_KV_BLOCKS, HEAD_DIM), lambda b, h, g: (b, h)),
            pl.BlockSpec((1, N_TABLE_TILES, TILE, TILE), lambda b, h, g: (h, 0, 0, 0)),
        ],
        out_specs=pl.BlockSpec((SEQ, HEAD_DIM), lambda b, h, g: (b, h)),
        out_shape=jax.ShapeDtypeStruct((ROWS, D_MODEL), BF16),
        scratch_shapes=[
            pltpu.VMEM((SEQ, 2 * HEAD_DIM), BF16),
            pltpu.VMEM((GROUP_KEYS, 2 * HEAD_DIM), BF16),
        ] + _flash_scratch(),
        compiler_params=pltpu.CompilerParams(
            dimension_semantics=("arbitrary", "arbitrary", "arbitrary"), vmem_limit_bytes=VMEM_LIMIT),
        name="moba_attn",
    )(qkv, qkv, qkv, kmean, bias)


def _diff_kernel(q_ref, k_ref, v_ref, bias_ref, lam_ref, g_ref, o_ref, qa_ref, map0_ref, *flash_refs,
                 lambda_init):
    which = pl.program_id(2)
    step = pl.program_id(3)
    m_ref, acc_ref = flash_refs[-2:]

    @pl.when(step == 0)
    def _():
        lane_map = lax.broadcasted_iota(jnp.int32, (ROW_TILE, HEAD_DIM), 1) // DIFF_HALF

        def split(tile, carry):
            rows = _tile_rows(tile)
            _reset_state(rows, m_ref, acc_ref)
            q = q_ref[rows, :]
            qa_ref[rows, :] = jnp.where(lane_map == which, q, jnp.zeros_like(q))
            return carry

        lax.fori_loop(0, N_ROW_TILES, split, 0)

    def group(i, carry):
        _flash_sweep(step * GROUPS_PER_STEP + i, qa_ref, lambda n: k_ref[_step_rows(i, n), :],
                     lambda: v_ref[_step_rows(i), :], bias_ref, *flash_refs)
        return carry

    lax.fori_loop(0, GROUPS_PER_STEP, group, 0)

    @pl.when((step == N_STEPS - 1) & (which == 0))
    def _():
        def keep(tile, carry):
            rows = _tile_rows(tile)
            map0_ref[rows, :] = _attention_of(acc_ref[rows, :])
            return carry

        lax.fori_loop(0, N_ROW_TILES, keep, 0)

    @pl.when((step == N_STEPS - 1) & (which == 1))
    def _():
        lam = lam_ref[...]
        lam_full = (jnp.exp(jnp.sum(lam[0:1] * lam[1:2], axis=1, keepdims=True))
                    - jnp.exp(jnp.sum(lam[2:3] * lam[3:4], axis=1, keepdims=True)) + lambda_init)

        def finish(tile, carry):
            rows = _tile_rows(tile)
            o = map0_ref[rows, :] - lam_full * _attention_of(acc_ref[rows, :])
            o = o * lax.rsqrt(jnp.mean(o * o, axis=-1, keepdims=True) + RMS_EPS) * g_ref[...]
            o_ref[rows, :] = (o * (1.0 - lambda_init)).astype(BF16)
            return carry

        lax.fori_loop(0, N_ROW_TILES, finish, 0, unroll=2)


def _diff_attention(qkv, bias, lam, subln_g, layer_idx):
    lambda_init = 0.8 - 0.6 * math.exp(-0.3 * layer_idx)
    return pl.pallas_call(
        functools.partial(_diff_kernel, lambda_init=lambda_init),
        grid=(BATCH, N_HEADS, 2, N_STEPS),
        in_specs=[
            pl.BlockSpec((SEQ, HEAD_DIM), lambda b, h, c, g: (b, h)),
            pl.BlockSpec((STEP_KEYS, HEAD_DIM), lambda b, h, c, g: (b * N_STEPS + g, N_HEADS + h)),
            pl.BlockSpec((STEP_KEYS, HEAD_DIM), lambda b, h, c, g: (b * N_STEPS + g, 2 * N_HEADS + h)),
            pl.BlockSpec((1, N_TABLE_TILES, TILE, TILE), lambda b, h, c, g: (h, 0, 0, 0)),
            pl.BlockSpec((4, DIFF_HALF), lambda b, h, c, g: (0, 0)),
            pl.BlockSpec((1, HEAD_DIM), lambda b, h, c, g: (0, 0)),
        ],
        out_specs=pl.BlockSpec((SEQ, HEAD_DIM), lambda b, h, c, g: (b, h)),
        out_shape=jax.ShapeDtypeStruct((ROWS, D_MODEL), BF16),
        scratch_shapes=[
            pltpu.VMEM((SEQ, HEAD_DIM), BF16),
            pltpu.VMEM((SEQ, HEAD_DIM), F32),
        ] + _flash_scratch(),
        compiler_params=pltpu.CompilerParams(
            dimension_semantics=("arbitrary", "arbitrary", "arbitrary", "arbitrary"),
            vmem_limit_bytes=VMEM_LIMIT),
        name="diff_attn",
    )(qkv, qkv, qkv, bias, lam, subln_g.reshape(1, HEAD_DIM))


FFN_TM = 1024
FFN_TF = 256


def _rms(y, g):
    return y * lax.rsqrt(jnp.mean(y * y, axis=-1, keepdims=True) + RMS_EPS) * g


def _mixer_out_ffn_kernel(o_ref, x_ref, wa_ref, gains_ref, mod_ref, wg_ref, wu_ref, wo_ref, out_ref,
                          xn_ref, h_ref, acc_ref):
    c = pl.program_id(1)
    last = pl.num_programs(1) - 1

    def chunk(h):
        gp = jnp.dot(h, wg_ref[0].astype(BF16), preferred_element_type=F32)
        up = jnp.dot(h, wu_ref[0].astype(BF16), preferred_element_type=F32)
        act = (gp / (1.0 + jnp.exp(-gp)) * up).astype(BF16)
        return jnp.dot(act, wo_ref[0].astype(BF16), preferred_element_type=F32)

    @pl.when(c == 0)
    def _():
        y = jnp.dot(o_ref[...], wa_ref[...], preferred_element_type=F32)
        xn = x_ref[...] + _mod(mod_ref, MOD_GATE_MIXER) * _rms(y, _gain(gains_ref, GAIN_MIXER_OUT))
        xn_ref[...] = xn
        h = _norm_mod(xn, _gain(gains_ref, GAIN_FFN_IN), _mod(mod_ref, MOD_SCALE_FFN),
                      _mod(mod_ref, MOD_SHIFT_FFN)).astype(BF16)
        h_ref[...] = h
        acc_ref[...] = chunk(h)

    @pl.when((c > 0) & (c < last))
    def _():
        acc_ref[...] += chunk(h_ref[...])

    @pl.when(c == last)
    def _():
        y = acc_ref[...] + chunk(h_ref[...])
        out_ref[...] = xn_ref[...] + _mod(mod_ref, MOD_GATE_FFN) * _rms(y, _gain(gains_ref, GAIN_FFN_OUT))


def _mixer_out_ffn(o, x2, w_attn_out_bf16, gains, mod, layer, w_in_layers, w_out_layers):
    n_chunks = D_FF // FFN_TF
    return pl.pallas_call(
        _mixer_out_ffn_kernel,
        grid=(ROWS // FFN_TM, n_chunks),
        in_specs=[
            pl.BlockSpec((FFN_TM, D_MODEL), lambda i, c: (i, 0)),
            pl.BlockSpec((FFN_TM, D_MODEL), lambda i, c: (i, 0)),
            pl.BlockSpec((D_MODEL, D_MODEL), lambda i, c: (0, 0), pipeline_mode=pl.Buffered(1)),
            *_layer_vector_specs(layer, FFN_TM),
            pl.BlockSpec((1, D_MODEL, FFN_TF), lambda i, c: (layer, 0, c)),
            pl.BlockSpec((1, D_MODEL, FFN_TF), lambda i, c: (layer, 0, n_chunks + c)),
            pl.BlockSpec((1, FFN_TF, D_MODEL), lambda i, c: (layer, c, 0)),
        ],
        out_specs=pl.BlockSpec((FFN_TM, D_MODEL), lambda i, c: (i, 0)),
        out_shape=jax.ShapeDtypeStruct((ROWS, D_MODEL), F32),
        scratch_shapes=[
            pltpu.VMEM((FFN_TM, D_MODEL), F32),
            pltpu.VMEM((FFN_TM, D_MODEL), BF16),
            pltpu.VMEM((FFN_TM, D_MODEL), F32),
        ],
        compiler_params=pltpu.CompilerParams(
            dimension_semantics=("parallel", "arbitrary"), vmem_limit_bytes=VMEM_LIMIT),
        name="mixer_out_ffn",
    )(o, x2, w_attn_out_bf16, gains, mod, w_in_layers, w_in_layers, w_out_layers)


def kernel(x, c, rel_bias, ada_w, ada_b, norm_g, moba_w_qkv, moba_w_o, diff_w_qkv, diff_w_o, diff_lambda,
           diff_subln_g, ffn_w_in, ffn_w_out):
    x2 = x.reshape(ROWS, D_MODEL)
    mod = _adaln_mod(c, ada_w, ada_b).reshape(DEPTH, BATCH, 6, D_MODEL)
    bias = _bias_tiles(rel_bias)
    for i in range(DEPTH):
        if i % 2 == 0:
            qkv, kmean = _qkv_proj(x2, norm_g, mod, i, moba_w_qkv, i // 2, HEAD_DIM ** -0.5 * LOG2_E, True)
            o = _moba_attention(qkv, kmean.reshape(BATCH * N_KV_BLOCKS, D_MODEL), bias)
            w_o = moba_w_o
        else:
            (qkv,) = _qkv_proj(x2, norm_g, mod, i, diff_w_qkv, i // 2, DIFF_HALF ** -0.5 * LOG2_E, False)
            o = _diff_attention(qkv, bias, diff_lambda[i // 2], diff_subln_g[i // 2], i)
            w_o = diff_w_o
        x2 = _mixer_out_ffn(o, x2, w_o[i // 2].astype(BF16), norm_g, mod, i, ffn_w_in, ffn_w_out)
    return x2.reshape(BATCH, SEQ, D_MODEL)
```

```python
import functools
import math

import numpy as np
import jax
import jax.numpy as jnp
from jax import lax
from jax.experimental import pallas as pl
from jax.experimental.pallas import tpu as pltpu

D_MODEL = 1024
BATCH = 2
SEQ = 8192
DEPTH = 2
N_HEADS = 8
HEAD_DIM = D_MODEL // N_HEADS
DIFF_HALF = HEAD_DIM // 2
MOBA_BLOCK = 256
MOBA_TOPK = 3
D_FF = 2816
N_BUCKETS = 32
MAX_EXACT = N_BUCKETS // 2
MAX_DISTANCE = 2048
RMS_EPS = 1e-6
NEG_INF = -1e30

ROWS = BATCH * SEQ
N_KV_BLOCKS = SEQ // MOBA_BLOCK
TILE = MOBA_BLOCK
VMEM_LIMIT = 48 * 1024 * 1024

F32 = jnp.float32
BF16 = jnp.bfloat16


def _bucket_of_distance():
    n = np.arange(MAX_DISTANCE + 1)
    nf = np.maximum(n, 1).astype(np.float64)
    val = np.log(nf / MAX_EXACT) / math.log(MAX_DISTANCE / MAX_EXACT) * (N_BUCKETS - MAX_EXACT)
    frac = np.abs(val - np.round(val))
    assert np.all((frac > 5e-5) | (n <= MAX_EXACT) | (n == MAX_DISTANCE))
    large = np.minimum(MAX_EXACT + np.floor(val + 1e-9).astype(np.int64), N_BUCKETS - 1)
    return np.where(n < MAX_EXACT, n, large)


_BUCKETS = _bucket_of_distance()
FAR_DISTANCE = int(np.min(np.nonzero(_BUCKETS == N_BUCKETS - 1)[0]))
N_BIAS_TILES = (FAR_DISTANCE + TILE - 1) // TILE + 1
N_TABLE_TILES = N_BIAS_TILES + 2


def _bias_bucket_tiles():
    i = np.arange(TILE)[:, None]
    j = np.arange(TILE)[None, :]
    out = []
    for t in range(-1, N_BIAS_TILES + 1):
        rel = t * TILE + i - j
        out.append(np.where(rel >= 0, _BUCKETS[np.clip(rel, 0, MAX_DISTANCE)], -1))
    return np.stack(out).astype(np.int32)


def _table_index(tiles_behind):
    return jnp.clip(tiles_behind, -1, N_BIAS_TILES) + 1


MOD_TN = 1024


def _mod_kernel(ct_ref, w_ref, b_ref, o_ref):
    ct = ct_ref[...]
    cact = ct / (1.0 + jnp.exp(-ct))
    w = w_ref[0]
    for b in range(BATCH):
        row = jnp.sum(w * cact[:, b:b + 1], axis=0, keepdims=True)
        o_ref[0, b:b + 1, :] = row + b_ref[0]


def _adaln_mod(c, ada_w, ada_b):
    n_out = ada_w.shape[-1]
    return pl.pallas_call(
        _mod_kernel,
        grid=(DEPTH, n_out // MOD_TN),
        in_specs=[
            pl.BlockSpec((D_MODEL, BATCH), lambda i, n: (0, 0)),
            pl.BlockSpec((1, D_MODEL, MOD_TN), lambda i, n: (i, 0, n)),
            pl.BlockSpec((1, 1, MOD_TN), lambda i, n: (i, 0, n)),
        ],
        out_specs=pl.BlockSpec((1, BATCH, MOD_TN), lambda i, n: (i, 0, n)),
        out_shape=jax.ShapeDtypeStruct((DEPTH, BATCH, n_out), F32),
        name="adaln_mod",
    )(c.T, ada_w, ada_b.reshape(DEPTH, 1, n_out))


LOG2_E = math.log2(math.e)


def _bias_kernel(tab_ref, bkt_ref, o_ref, *, buckets_present):
    h = pl.program_id(0)
    far = tab_ref[N_BUCKETS - 1, h]
    for t, present in enumerate(buckets_present):
        bkt = bkt_ref[t]
        acc = jnp.full(bkt.shape, NEG_INF, F32)
        for b in present:
            acc = jnp.where(bkt == b, (tab_ref[b, h] - far) * LOG2_E, acc)
        o_ref[0, t] = acc


def _bias_tiles(rel_bias):
    bkt = _bias_bucket_tiles()
    buckets_present = tuple(tuple(int(b) for b in np.unique(tile) if b >= 0) for tile in bkt)
    return pl.pallas_call(
        functools.partial(_bias_kernel, buckets_present=buckets_present),
        grid=(N_HEADS,),
        in_specs=[
            pl.BlockSpec(memory_space=pltpu.SMEM),
            pl.BlockSpec((N_TABLE_TILES, TILE, TILE), lambda h: (0, 0, 0)),
        ],
        out_specs=pl.BlockSpec((1, N_TABLE_TILES, TILE, TILE), lambda h: (h, 0, 0, 0)),
        out_shape=jax.ShapeDtypeStruct((N_HEADS, N_TABLE_TILES, TILE, TILE), F32),
        name="bias_tiles",
    )(rel_bias, jnp.asarray(bkt))


QKV_TM = 1024


def _norm_mod(x, g, sc, sh):
    y = x * lax.rsqrt(jnp.mean(x * x, axis=-1, keepdims=True) + RMS_EPS)
    return (y * g) * (1.0 + sc) + sh


def _cast_weight_once(w_ref, wb_ref):
    @pl.when(pl.program_id(0) == 0)
    def _():
        wb_ref[...] = w_ref[0].astype(BF16)


GAIN_MIXER_IN, GAIN_MIXER_OUT, GAIN_FFN_IN, GAIN_FFN_OUT = range(4)
MOD_SHIFT_MIXER, MOD_SCALE_MIXER, MOD_GATE_MIXER, MOD_SHIFT_FFN, MOD_SCALE_FFN, MOD_GATE_FFN = range(6)


def _gain(gains_ref, which):
    return gains_ref[0, which:which + 1, :]


def _mod(mod_ref, which):
    return mod_ref[0, 0, which:which + 1, :]


def _layer_vector_specs(layer, rows_per_tile):
    tiles_per_batch = SEQ // rows_per_tile
    return (pl.BlockSpec((1, 4, D_MODEL), lambda i, *_: (layer, 0, 0)),
            pl.BlockSpec((1, 1, 6, D_MODEL), lambda i, *_: (layer, i // tiles_per_batch, 0, 0)))


def _qkv_kernel(x_ref, gains_ref, mod_ref, w_ref, o_ref, *rest, q_scale):
    *km_ref, wb_ref = rest
    _cast_weight_once(w_ref, wb_ref)
    hb = _norm_mod(x_ref[...], _gain(gains_ref, GAIN_MIXER_IN), _mod(mod_ref, MOD_SCALE_MIXER),
                   _mod(mod_ref, MOD_SHIFT_MIXER)).astype(BF16)
    for n in range(3):
        r = jnp.dot(hb, wb_ref[:, n * D_MODEL:(n + 1) * D_MODEL], preferred_element_type=F32)
        if n == 0:
            r = r * q_scale
        if n == 1 and km_ref:
            km_ref[0][0] = jnp.mean(r.reshape(QKV_TM // MOBA_BLOCK, MOBA_BLOCK, D_MODEL), axis=1)
        o_ref[:, n * D_MODEL:(n + 1) * D_MODEL] = r.astype(BF16)


def _qkv_proj(x2, gains, mod, layer, w_layers, w_layer, q_scale, with_kmean):
    n_tiles = ROWS // QKV_TM
    out_shape = [jax.ShapeDtypeStruct((ROWS, 3 * D_MODEL), BF16)]
    out_specs = [pl.BlockSpec((QKV_TM, 3 * D_MODEL), lambda i: (i, 0))]
    if with_kmean:
        per_tile = QKV_TM // MOBA_BLOCK
        out_shape.append(jax.ShapeDtypeStruct((n_tiles, per_tile, D_MODEL), F32))
        out_specs.append(pl.BlockSpec((1, per_tile, D_MODEL), lambda i: (i, 0, 0)))
    return pl.pallas_call(
        functools.partial(_qkv_kernel, q_scale=q_scale),
        grid=(n_tiles,),
        in_specs=[
            pl.BlockSpec((QKV_TM, D_MODEL), lambda i: (i, 0)),
            *_layer_vector_specs(layer, QKV_TM),
            pl.BlockSpec((1, D_MODEL, 3 * D_MODEL), lambda i: (w_layer, 0, 0), pipeline_mode=pl.Buffered(1)),
        ],
        out_specs=out_specs,
        out_shape=out_shape,
        scratch_shapes=[pltpu.VMEM((D_MODEL, 3 * D_MODEL), BF16)],
        compiler_params=pltpu.CompilerParams(
            dimension_semantics=("arbitrary",), vmem_limit_bytes=VMEM_LIMIT),
        name="qkv_moba" if with_kmean else "qkv_diff",
    )(x2, gains, mod, w_layers)


_NT = (((1,), (1,)), ((), ()))
GROUP = 4
GROUP_KEYS = GROUP * TILE
N_GROUPS = SEQ // GROUP_KEYS
GROUPS_PER_STEP = 4
N_STEPS = N_GROUPS // GROUPS_PER_STEP
STEP_KEYS = GROUPS_PER_STEP * GROUP_KEYS
ROW_TILE = 2 * TILE
N_ROW_TILES = SEQ // ROW_TILE
ROW_TILES_PER_GROUP = GROUP_KEYS // ROW_TILE
SOFTMAX_ROWS = 64
BIAS_FREE_FROM_ROW_TILE = -(-(N_BIAS_TILES + GROUP - 1) // (ROW_TILE // TILE))


def _flash_scratch():
    scores = pltpu.VMEM((ROW_TILE, GROUP_KEYS), F32)
    probs = pltpu.VMEM((ROW_TILE, GROUP_KEYS), BF16)
    rescale = pltpu.VMEM((ROW_TILE, 1), F32)
    values = pltpu.VMEM((2, GROUP_KEYS, 2 * HEAD_DIM), BF16)
    running_max = pltpu.VMEM((SEQ, 1), F32)
    acc = pltpu.VMEM((SEQ, 2 * HEAD_DIM), F32)
    return [scores, scores, probs, probs, rescale, rescale, values, running_max, acc]


def _tile_rows(tile):
    return pl.ds(pl.multiple_of(tile * ROW_TILE, ROW_TILE), ROW_TILE)


def _flash_sweep(grp, qa_ref, group_keys, group_values, bias_ref,
                 s0_ref, s1_ref, p0_ref, p1_ref, alpha0_ref, alpha1_ref, va_ref, m_ref, acc_ref):
    half = GROUP_KEYS // 2
    first = grp * ROW_TILES_PER_GROUP
    last = N_ROW_TILES - 1
    va_new = va_ref.at[grp % 2]
    va_old = va_ref.at[1 - grp % 2]
    va_new[:, :HEAD_DIM] = group_values()

    @pl.when(grp < 2)
    def _():
        va_new[:, HEAD_DIM:] = jnp.ones((GROUP_KEYS, HEAD_DIM), BF16)

    def scores(tile, s_ref, n_keys=GROUP_KEYS, biased=True):
        raw = lax.dot_general(qa_ref[_tile_rows(tile), :], group_keys(n_keys), _NT,
                              preferred_element_type=F32)
        if not biased:
            s_ref[:, :n_keys] = raw
            return
        for r in range(ROW_TILE // TILE):
            for j in range(n_keys // TILE):
                bias = bias_ref[0, _table_index(tile * (ROW_TILE // TILE) + r - (grp * GROUP + j))]
                s_ref[r * TILE:(r + 1) * TILE, j * TILE:(j + 1) * TILE] = (
                    raw[r * TILE:(r + 1) * TILE, j * TILE:(j + 1) * TILE] + bias)

    def softmax(tile, s_ref, p_ref, alpha_ref, n_keys=GROUP_KEYS):
        for c in range(ROW_TILE // SOFTMAX_ROWS):
            chunk = slice(c * SOFTMAX_ROWS, (c + 1) * SOFTMAX_ROWS)
            rows = pl.ds(pl.multiple_of(tile * ROW_TILE + c * SOFTMAX_ROWS, SOFTMAX_ROWS), SOFTMAX_ROWS)
            s = s_ref[chunk, :n_keys]
            m_prev = m_ref[rows, :]
            m_new = jnp.maximum(m_prev, jnp.max(s, axis=1, keepdims=True))
            alpha_ref[chunk, :] = jnp.exp2(m_prev - m_new)
            m_ref[rows, :] = m_new
            p_ref[chunk, :n_keys] = jnp.exp2(s - m_new).astype(BF16)

    def accumulate(tile, p_ref, alpha_ref, va, n_keys=GROUP_KEYS):
        rows = _tile_rows(tile)
        acc_ref[rows, :] = (alpha_ref[...] * acc_ref[rows, :]
                            + jnp.dot(p_ref[:, :n_keys], va[:n_keys, :], preferred_element_type=F32))

    def enter(with_pending):
        scores(first, s0_ref, half)
        if with_pending:
            softmax(last, s1_ref, p1_ref, alpha1_ref)
            accumulate(last - 1, p0_ref, alpha0_ref, va_old)
        scores(first + 1, s1_ref)
        softmax(first, s0_ref, p0_ref, alpha0_ref, half)
        if with_pending:
            accumulate(last, p1_ref, alpha1_ref, va_old)

    def second_pair():
        scores(first + 2, s0_ref)
        softmax(first + 1, s1_ref, p1_ref, alpha1_ref)
        accumulate(first, p0_ref, alpha0_ref, va_new, half)
        scores(first + 3, s1_ref)
        softmax(first + 2, s0_ref, p0_ref, alpha0_ref)
        accumulate(first + 1, p1_ref, alpha1_ref, va_new)

    def pair(t, first_biased=False, values_first=False):
        tile = first + 3 + 2 * t
        if values_first:
            accumulate(tile - 1, p0_ref, alpha0_ref, va_new)
        scores(tile + 1, s0_ref, biased=first_biased)
        softmax(tile, s1_ref, p1_ref, alpha1_ref)
        if not values_first:
            accumulate(tile - 1, p0_ref, alpha0_ref, va_new)
        else:
            accumulate(tile, p1_ref, alpha1_ref, va_new)
        scores(tile + 2, s1_ref, biased=False)
        softmax(tile + 1, s0_ref, p0_ref, alpha0_ref)
        if not values_first:
            accumulate(tile, p1_ref, alpha1_ref, va_new)

    assert BIAS_FREE_FROM_ROW_TILE == 5
    n_pairs = jnp.maximum(last - first - 3, 0) // 2
    middle = (grp > 0) & (grp < N_GROUPS - 1)

    def first_pairs(even_count):
        pair(0, first_biased=True, values_first=even_count)
        if even_count:
            pair(1, values_first=True)

    @pl.when(grp == 0)
    def _():
        assert ((N_ROW_TILES - 4) // 2) % 2 == 0
        enter(False)
        second_pair()
        first_pairs(True)

    @pl.when(middle & (n_pairs % 2 == 1))
    def _():
        enter(True)
        second_pair()
        first_pairs(False)

    @pl.when(middle & (n_pairs > 0) & (n_pairs % 2 == 0))
    def _():
        enter(True)
        second_pair()
        first_pairs(True)

    @pl.when(middle & (n_pairs == 0))
    def _():
        enter(True)
        second_pair()

    @pl.when(grp == N_GROUPS - 1)
    def _():
        enter(True)
        softmax(last, s1_ref, p1_ref, alpha1_ref)
        accumulate(last - 1, p0_ref, alpha0_ref, va_new, half)
        accumulate(last, p1_ref, alpha1_ref, va_new)

    def two_pairs(u, carry):
        t = 2 - n_pairs % 2 + 2 * u
        pair(t, values_first=True)
        pair(t + 1, values_first=True)
        return carry

    lax.fori_loop(0, jnp.maximum(n_pairs - 1, 0) // 2, two_pairs, 0)


def _reset_state(rows, m_ref, acc_ref):
    m_ref[rows, :] = jnp.full((rows.size, 1), NEG_INF, F32)
    acc_ref[rows, :] = jnp.zeros((rows.size, 2 * HEAD_DIM), F32)


def _attention_of(acc):
    return acc[:, :HEAD_DIM] / acc[:, HEAD_DIM:]


GATE_QUERIES = 8 * TILE


def _moba_gating(q_ref, km_ref, qa_ref, m_ref, acc_ref):
    blk = lax.broadcasted_iota(jnp.int32, (N_KV_BLOCKS, GATE_QUERIES), 0)
    blkf = blk.astype(F32)
    tile_in_step = lax.broadcasted_iota(jnp.int32, (N_KV_BLOCKS, GATE_QUERIES), 1) // TILE

    def gate_step(step, carry):
        rows = pl.ds(pl.multiple_of(step * GATE_QUERIES, GATE_QUERIES), GATE_QUERIES)
        _reset_state(rows, m_ref, acc_ref)
        q = q_ref[rows, :]
        gate = lax.dot_general(km_ref[...].astype(BF16), q, _NT, preferred_element_type=F32)
        tile = step * (GATE_QUERIES // TILE) + tile_in_step
        past = blk < tile
        g = jnp.where(past, gate, NEG_INF)
        madd = jnp.full(gate.shape, NEG_INF, F32)
        for _ in range(MOBA_TOPK):
            mx = jnp.max(g, axis=0, keepdims=True)
            first = jnp.min(jnp.where(g == mx, blkf, float(N_KV_BLOCKS)), axis=0, keepdims=True)
            hit = blkf == first
            madd = jnp.where(hit & past, 0.0, madd)
            g = jnp.where(hit, -jnp.inf, g)
        madd = jnp.where(blk == tile, 0.0, madd)
        madd = jnp.concatenate([madd, jnp.zeros((HEAD_DIM - N_KV_BLOCKS, GATE_QUERIES), F32)], axis=0)
        qa_ref[rows, :HEAD_DIM] = q
        qa_ref[rows, HEAD_DIM:] = madd.T.astype(BF16)
        return carry

    lax.fori_loop(0, SEQ // GATE_QUERIES, gate_step, 0, unroll=2)


def _step_rows(group_in_step, n_rows=GROUP_KEYS):
    return pl.ds(pl.multiple_of(group_in_step * GROUP_KEYS, GROUP_KEYS), n_rows)


def _moba_kernel(q_ref, k_ref, v_ref, km_ref, bias_ref, o_ref, qa_ref, ka_ref, *flash_refs):
    step = pl.program_id(2)
    m_ref, acc_ref = flash_refs[-2:]

    @pl.when(step == 0)
    def _():
        _moba_gating(q_ref, km_ref, qa_ref, m_ref, acc_ref)

    lane = lax.broadcasted_iota(jnp.int32, (GROUP_KEYS, HEAD_DIM), 1)
    block_in_group = lax.broadcasted_iota(jnp.int32, (GROUP_KEYS, HEAD_DIM), 0) // TILE

    def group(i, carry):
        grp = step * GROUPS_PER_STEP + i
        ka_ref[:, :HEAD_DIM] = k_ref[_step_rows(i), :]
        ka_ref[:, HEAD_DIM:] = jnp.where(lane == grp * GROUP + block_in_group, 1.0, 0.0).astype(BF16)
        _flash_sweep(grp, qa_ref, lambda n: ka_ref[:n, :], lambda: v_ref[_step_rows(i), :], bias_ref,
                     *flash_refs)
        return carry

    lax.fori_loop(0, GROUPS_PER_STEP, group, 0)

    @pl.when(step == N_STEPS - 1)
    def _():
        def finish(tile, carry):
            rows = _tile_rows(tile)
            o_ref[rows, :] = _attention_of(acc_ref[rows, :]).astype(BF16)
            return carry

        lax.fori_loop(0, N_ROW_TILES, finish, 0)


def _moba_attention(qkv, kmean, bias):
    return pl.pallas_call(
        _moba_kernel,
        grid=(BATCH, N_HEADS, N_STEPS),
        in_specs=[
            pl.BlockSpec((SEQ, HEAD_DIM), lambda b, h, g: (b, h)),
            pl.BlockSpec((STEP_KEYS, HEAD_DIM), lambda b, h, g: (b * N_STEPS + g, N_HEADS + h)),
            pl.BlockSpec((STEP_KEYS, HEAD_DIM), lambda b, h, g: (b * N_STEPS + g, 2 * N_HEADS + h)),
            pl.BlockSpec((N_KV_BLOCKS, HEAD_DIM), lambda b, h, g: (b, h)),
            pl.BlockSpec((1, N_TABLE_TILES, TILE, TILE), lambda b, h, g: (h, 0, 0, 0)),
        ],
        out_specs=pl.BlockSpec((SEQ, HEAD_DIM), lambda b, h, g: (b, h)),
        out_shape=jax.ShapeDtypeStruct((ROWS, D_MODEL), BF16),
        scratch_shapes=[
            pltpu.VMEM((SEQ, 2 * HEAD_DIM), BF16),
            pltpu.VMEM((GROUP_KEYS, 2 * HEAD_DIM), BF16),
        ] + _flash_scratch(),
        compiler_params=pltpu.CompilerParams(
            dimension_semantics=("arbitrary", "arbitrary", "arbitrary"), vmem_limit_bytes=VMEM_LIMIT),
        name="moba_attn",
    )(qkv, qkv, qkv, kmean, bias)


def _diff_kernel(q_ref, k_ref, v_ref, bias_ref, lam_ref, g_ref, o_ref, qa_ref, map0_ref, *flash_refs,
                 lambda_init):
    which = pl.program_id(2)
    step = pl.program_id(3)
    m_ref, acc_ref = flash_refs[-2:]

    @pl.when(step == 0)
    def _():
        lane_map = lax.broadcasted_iota(jnp.int32, (ROW_TILE, HEAD_DIM), 1) // DIFF_HALF

        def split(tile, carry):
            rows = _tile_rows(tile)
            _reset_state(rows, m_ref, acc_ref)
            q = q_ref[rows, :]
            qa_ref[rows, :] = jnp.where(lane_map == which, q, jnp.zeros_like(q))
            return carry

        lax.fori_loop(0, N_ROW_TILES, split, 0)

    def group(i, carry):
        _flash_sweep(step * GROUPS_PER_STEP + i, qa_ref, lambda n: k_ref[_step_rows(i, n), :],
                     lambda: v_ref[_step_rows(i), :], bias_ref, *flash_refs)
        return carry

    lax.fori_loop(0, GROUPS_PER_STEP, group, 0)

    @pl.when((step == N_STEPS - 1) & (which == 0))
    def _():
        def keep(tile, carry):
            rows = _tile_rows(tile)
            map0_ref[rows, :] = _attention_of(acc_ref[rows, :])
            return carry

        lax.fori_loop(0, N_ROW_TILES, keep, 0)

    @pl.when((step == N_STEPS - 1) & (which == 1))
    def _():
        lam = lam_ref[...]
        lam_full = (jnp.exp(jnp.sum(lam[0:1] * lam[1:2], axis=1, keepdims=True))
                    - jnp.exp(jnp.sum(lam[2:3] * lam[3:4], axis=1, keepdims=True)) + lambda_init)

        def finish(tile, carry):
            rows = _tile_rows(tile)
            o = map0_ref[rows, :] - lam_full * _attention_of(acc_ref[rows, :])
            o = o * lax.rsqrt(jnp.mean(o * o, axis=-1, keepdims=True) + RMS_EPS) * g_ref[...]
            o_ref[rows, :] = (o * (1.0 - lambda_init)).astype(BF16)
            return carry

        lax.fori_loop(0, N_ROW_TILES, finish, 0, unroll=2)


def _diff_attention(qkv, bias, lam, subln_g, layer_idx):
    lambda_init = 0.8 - 0.6 * math.exp(-0.3 * layer_idx)
    return pl.pallas_call(
        functools.partial(_diff_kernel, lambda_init=lambda_init),
        grid=(BATCH, N_HEADS, 2, N_STEPS),
        in_specs=[
            pl.BlockSpec((SEQ, HEAD_DIM), lambda b, h, c, g: (b, h)),
            pl.BlockSpec((STEP_KEYS, HEAD_DIM), lambda b, h, c, g: (b * N_STEPS + g, N_HEADS + h)),
            pl.BlockSpec((STEP_KEYS, HEAD_DIM), lambda b, h, c, g: (b * N_STEPS + g, 2 * N_HEADS + h)),
            pl.BlockSpec((1, N_TABLE_TILES, TILE, TILE), lambda b, h, c, g: (h, 0, 0, 0)),
            pl.BlockSpec((4, DIFF_HALF), lambda b, h, c, g: (0, 0)),
            pl.BlockSpec((1, HEAD_DIM), lambda b, h, c, g: (0, 0)),
        ],
        out_specs=pl.BlockSpec((SEQ, HEAD_DIM), lambda b, h, c, g: (b, h)),
        out_shape=jax.ShapeDtypeStruct((ROWS, D_MODEL), BF16),
        scratch_shapes=[
            pltpu.VMEM((SEQ, HEAD_DIM), BF16),
            pltpu.VMEM((SEQ, HEAD_DIM), F32),
        ] + _flash_scratch(),
        compiler_params=pltpu.CompilerParams(
            dimension_semantics=("arbitrary", "arbitrary", "arbitrary", "arbitrary"),
            vmem_limit_bytes=VMEM_LIMIT),
        name="diff_attn",
    )(qkv, qkv, qkv, bias, lam, subln_g.reshape(1, HEAD_DIM))


FFN_TM = 1024
FFN_TF = 256


def _rms(y, g):
    return y * lax.rsqrt(jnp.mean(y * y, axis=-1, keepdims=True) + RMS_EPS) * g


def _mixer_out_ffn_kernel(o_ref, x_ref, wa_ref, gains_ref, mod_ref, wg_ref, wu_ref, wo_ref, out_ref,
                          xn_ref, h_ref, acc_ref):
    c = pl.program_id(1)
    last = pl.num_programs(1) - 1

    def chunk(h):
        gp = jnp.dot(h, wg_ref[0].astype(BF16), preferred_element_type=F32)
        up = jnp.dot(h, wu_ref[0].astype(BF16), preferred_element_type=F32)
        act = (gp / (1.0 + jnp.exp(-gp)) * up).astype(BF16)
        return jnp.dot(act, wo_ref[0].astype(BF16), preferred_element_type=F32)

    @pl.when(c == 0)
    def _():
        y = jnp.dot(o_ref[...], wa_ref[...], preferred_element_type=F32)
        xn = x_ref[...] + _mod(mod_ref, MOD_GATE_MIXER) * _rms(y, _gain(gains_ref, GAIN_MIXER_OUT))
        xn_ref[...] = xn
        h = _norm_mod(xn, _gain(gains_ref, GAIN_FFN_IN), _mod(mod_ref, MOD_SCALE_FFN),
                      _mod(mod_ref, MOD_SHIFT_FFN)).astype(BF16)
        h_ref[...] = h
        acc_ref[...] = chunk(h)

    @pl.when((c > 0) & (c < last))
    def _():
        acc_ref[...] += chunk(h_ref[...])

    @pl.when(c == last)
    def _():
        y = acc_ref[...] + chunk(h_ref[...])
        out_ref[...] = xn_ref[...] + _mod(mod_ref, MOD_GATE_FFN) * _rms(y, _gain(gains_ref, GAIN_FFN_OUT))


def _mixer_out_ffn(o, x2, w_attn_out_bf16, gains, mod, layer, w_in_layers, w_out_layers):
    n_chunks = D_FF // FFN_TF
    return pl.pallas_call(
        _mixer_out_ffn_kernel,
        grid=(ROWS // FFN_TM, n_chunks),
        in_specs=[
            pl.BlockSpec((FFN_TM, D_MODEL), lambda i, c: (i, 0)),
            pl.BlockSpec((FFN_TM, D_MODEL), lambda i, c: (i, 0)),
            pl.BlockSpec((D_MODEL, D_MODEL), lambda i, c: (0, 0), pipeline_mode=pl.Buffered(1)),
            *_layer_vector_specs(layer, FFN_TM),
            pl.BlockSpec((1, D_MODEL, FFN_TF), lambda i, c: (layer, 0, c)),
            pl.BlockSpec((1, D_MODEL, FFN_TF), lambda i, c: (layer, 0, n_chunks + c)),
            pl.BlockSpec((1, FFN_TF, D_MODEL), lambda i, c: (layer, c, 0)),
        ],
        out_specs=pl.BlockSpec((FFN_TM, D_MODEL), lambda i, c: (i, 0)),
        out_shape=jax.ShapeDtypeStruct((ROWS, D_MODEL), F32),
        scratch_shapes=[
            pltpu.VMEM((FFN_TM, D_MODEL), F32),
            pltpu.VMEM((FFN_TM, D_MODEL), BF16),
            pltpu.VMEM((FFN_TM, D_MODEL), F32),
        ],
        compiler_params=pltpu.CompilerParams(
            dimension_semantics=("parallel", "arbitrary"), vmem_limit_bytes=VMEM_LIMIT),
        name="mixer_out_ffn",
    )(o, x2, w_attn_out_bf16, gains, mod, w_in_layers, w_in_layers, w_out_layers)


def kernel(x, c, rel_bias, ada_w, ada_b, norm_g, moba_w_qkv, moba_w_o, diff_w_qkv, diff_w_o, diff_lambda,
           diff_subln_g, ffn_w_in, ffn_w_out):
    x2 = x.reshape(ROWS, D_MODEL)
    mod = _adaln_mod(c, ada_w, ada_b).reshape(DEPTH, BATCH, 6, D_MODEL)
    bias = _bias_tiles(rel_bias)
    for i in range(DEPTH):
        if i % 2 == 0:
            qkv, kmean = _qkv_proj(x2, norm_g, mod, i, moba_w_qkv, i // 2, HEAD_DIM ** -0.5 * LOG2_E, True)
            o = _moba_attention(qkv, kmean.reshape(BATCH * N_KV_BLOCKS, D_MODEL), bias)
            w_o = moba_w_o
        else:
            (qkv,) = _qkv_proj(x2, norm_g, mod, i, diff_w_qkv, i // 2, DIFF_HALF ** -0.5 * LOG2_E, False)
            o = _diff_attention(qkv, bias, diff_lambda[i // 2], diff_subln_g[i // 2], i)
            w_o = diff_w_o
        x2 = _mixer_out_ffn(o, x2, w_o[i // 2].astype(BF16), norm_g, mod, i, ffn_w_in, ffn_w_out)
    return x2.reshape(BATCH, SEQ, D_MODEL)
```

```python
import functools
import math

import numpy as np
import jax
import jax.numpy as jnp
from jax import lax
from jax.experimental import pallas as pl
from jax.experimental.pallas import tpu as pltpu

D_MODEL = 1024
BATCH = 2
SEQ = 8192
DEPTH = 2
N_HEADS = 8
HEAD_DIM = D_MODEL // N_HEADS
DIFF_HALF = HEAD_DIM // 2
MOBA_BLOCK = 256
MOBA_TOPK = 3
D_FF = 2816
N_BUCKETS = 32
MAX_EXACT = N_BUCKETS // 2
MAX_DISTANCE = 2048
RMS_EPS = 1e-6
NEG_INF = -1e30

ROWS = BATCH * SEQ
N_KV_BLOCKS = SEQ // MOBA_BLOCK
TILE = MOBA_BLOCK
VMEM_LIMIT = 48 * 1024 * 1024

F32 = jnp.float32
BF16 = jnp.bfloat16


def _bucket_of_distance():
    n = np.arange(MAX_DISTANCE + 1)
    nf = np.maximum(n, 1).astype(np.float64)
    val = np.log(nf / MAX_EXACT) / math.log(MAX_DISTANCE / MAX_EXACT) * (N_BUCKETS - MAX_EXACT)
    frac = np.abs(val - np.round(val))
    assert np.all((frac > 5e-5) | (n <= MAX_EXACT) | (n == MAX_DISTANCE))
    large = np.minimum(MAX_EXACT + np.floor(val + 1e-9).astype(np.int64), N_BUCKETS - 1)
    return np.where(n < MAX_EXACT, n, large)


_BUCKETS = _bucket_of_distance()
FAR_DISTANCE = int(np.min(np.nonzero(_BUCKETS == N_BUCKETS - 1)[0]))
N_BIAS_TILES = (FAR_DISTANCE + TILE - 1) // TILE + 1
N_TABLE_TILES = N_BIAS_TILES + 2


def _bias_bucket_tiles():
    i = np.arange(TILE)[:, None]
    j = np.arange(TILE)[None, :]
    out = []
    for t in range(-1, N_BIAS_TILES + 1):
        rel = t * TILE + i - j
        out.append(np.where(rel >= 0, _BUCKETS[np.clip(rel, 0, MAX_DISTANCE)], -1))
    return np.stack(out).astype(np.int32)


def _table_index(tiles_behind):
    return jnp.clip(tiles_behind, -1, N_BIAS_TILES) + 1


MOD_TN = 1024


def _mod_kernel(ct_ref, w_ref, b_ref, o_ref):
    ct = ct_ref[...]
    cact = ct / (1.0 + jnp.exp(-ct))
    w = w_ref[0]
    for b in range(BATCH):
        row = jnp.sum(w * cact[:, b:b + 1], axis=0, keepdims=True)
        o_ref[0, b:b + 1, :] = row + b_ref[0]


def _adaln_mod(c, ada_w, ada_b):
    n_out = ada_w.shape[-1]
    return pl.pallas_call(
        _mod_kernel,
        grid=(DEPTH, n_out // MOD_TN),
        in_specs=[
            pl.BlockSpec((D_MODEL, BATCH), lambda i, n: (0, 0)),
            pl.BlockSpec((1, D_MODEL, MOD_TN), lambda i, n: (i, 0, n)),
            pl.BlockSpec((1, 1, MOD_TN), lambda i, n: (i, 0, n)),
        ],
        out_specs=pl.BlockSpec((1, BATCH, MOD_TN), lambda i, n: (i, 0, n)),
        out_shape=jax.ShapeDtypeStruct((DEPTH, BATCH, n_out), F32),
        name="adaln_mod",
    )(c.T, ada_w, ada_b.reshape(DEPTH, 1, n_out))


LOG2_E = math.log2(math.e)


def _bias_kernel(tab_ref, bkt_ref, o_ref, *, buckets_present):
    h = pl.program_id(0)
    far = tab_ref[N_BUCKETS - 1, h]
    for t, present in enumerate(buckets_present):
        bkt = bkt_ref[t]
        acc = jnp.full(bkt.shape, NEG_INF, F32)
        for b in present:
            acc = jnp.where(bkt == b, (tab_ref[b, h] - far) * LOG2_E, acc)
        o_ref[0, t] = acc


def _bias_tiles(rel_bias):
    bkt = _bias_bucket_tiles()
    buckets_present = tuple(tuple(int(b) for b in np.unique(tile) if b >= 0) for tile in bkt)
    return pl.pallas_call(
        functools.partial(_bias_kernel, buckets_present=buckets_present),
        grid=(N_HEADS,),
        in_specs=[
            pl.BlockSpec(memory_space=pltpu.SMEM),
            pl.BlockSpec((N_TABLE_TILES, TILE, TILE), lambda h: (0, 0, 0)),
        ],
        out_specs=pl.BlockSpec((1, N_TABLE_TILES, TILE, TILE), lambda h: (h, 0, 0, 0)),
        out_shape=jax.ShapeDtypeStruct((N_HEADS, N_TABLE_TILES, TILE, TILE), F32),
        name="bias_tiles",
    )(rel_bias, jnp.asarray(bkt))


QKV_TM = 1024


def _norm_mod(x, g, sc, sh):
    y = x * lax.rsqrt(jnp.mean(x * x, axis=-1, keepdims=True) + RMS_EPS)
    return (y * g) * (1.0 + sc) + sh


def _cast_weight_once(w_ref, wb_ref):
    @pl.when(pl.program_id(0) == 0)
    def _():
        wb_ref[...] = w_ref[0].astype(BF16)


GAIN_MIXER_IN, GAIN_MIXER_OUT, GAIN_FFN_IN, GAIN_FFN_OUT = range(4)
MOD_SHIFT_MIXER, MOD_SCALE_MIXER, MOD_GATE_MIXER, MOD_SHIFT_FFN, MOD_SCALE_FFN, MOD_GATE_FFN = range(6)


def _gain(gains_ref, which):
    return gains_ref[0, which:which + 1, :]


def _mod(mod_ref, which):
    return mod_ref[0, 0, which:which + 1, :]


def _layer_vector_specs(layer, rows_per_tile):
    tiles_per_batch = SEQ // rows_per_tile
    return (pl.BlockSpec((1, 4, D_MODEL), lambda i, *_: (layer, 0, 0)),
            pl.BlockSpec((1, 1, 6, D_MODEL), lambda i, *_: (layer, i // tiles_per_batch, 0, 0)))


def _qkv_kernel(x_ref, gains_ref, mod_ref, w_ref, o_ref, *rest, q_scale):
    *km_ref, wb_ref = rest
    _cast_weight_once(w_ref, wb_ref)
    hb = _norm_mod(x_ref[...], _gain(gains_ref, GAIN_MIXER_IN), _mod(mod_ref, MOD_SCALE_MIXER),
                   _mod(mod_ref, MOD_SHIFT_MIXER)).astype(BF16)
    for n in range(3):
        r = jnp.dot(hb, wb_ref[:, n * D_MODEL:(n + 1) * D_MODEL], preferred_element_type=F32)
        if n == 0:
            r = r * q_scale
        if n == 1 and km_ref:
            km_ref[0][0] = jnp.mean(r.reshape(QKV_TM // MOBA_BLOCK, MOBA_BLOCK, D_MODEL), axis=1)
        o_ref[:, n * D_MODEL:(n + 1) * D_MODEL] = r.astype(BF16)


def _qkv_proj(x2, gains, mod, layer, w_layers, w_layer, q_scale, with_kmean):
    n_tiles = ROWS // QKV_TM
    out_shape = [jax.ShapeDtypeStruct((ROWS, 3 * D_MODEL), BF16)]
    out_specs = [pl.BlockSpec((QKV_TM, 3 * D_MODEL), lambda i: (i, 0))]
    if with_kmean:
        per_tile = QKV_TM // MOBA_BLOCK
        out_shape.append(jax.ShapeDtypeStruct((n_tiles, per_tile, D_MODEL), F32))
        out_specs.append(pl.BlockSpec((1, per_tile, D_MODEL), lambda i: (i, 0, 0)))
    return pl.pallas_call(
        functools.partial(_qkv_kernel, q_scale=q_scale),
        grid=(n_tiles,),
        in_specs=[
            pl.BlockSpec((QKV_TM, D_MODEL), lambda i: (i, 0)),
            *_layer_vector_specs(layer, QKV_TM),
            pl.BlockSpec((1, D_MODEL, 3 * D_MODEL), lambda i: (w_layer, 0, 0), pipeline_mode=pl.Buffered(1)),
        ],
        out_specs=out_specs,
        out_shape=out_shape,
        scratch_shapes=[pltpu.VMEM((D_MODEL, 3 * D_MODEL), BF16)],
        compiler_params=pltpu.CompilerParams(
            dimension_semantics=("arbitrary",), vmem_limit_bytes=VMEM_LIMIT),
        name="qkv_moba" if with_kmean else "qkv_diff",
    )(x2, gains, mod, w_layers)


_NT = (((1,), (1,)), ((), ()))
GROUP = 4
GROUP_KEYS = GROUP * TILE
N_GROUPS = SEQ // GROUP_KEYS
GROUPS_PER_STEP = 4
N_STEPS = N_GROUPS // GROUPS_PER_STEP
STEP_KEYS = GROUPS_PER_STEP * GROUP_KEYS
ROW_TILE = 2 * TILE
N_ROW_TILES = SEQ // ROW_TILE
ROW_TILES_PER_GROUP = GROUP_KEYS // ROW_TILE
SOFTMAX_ROWS = 64
BIAS_FREE_FROM_ROW_TILE = -(-(N_BIAS_TILES + GROUP - 1) // (ROW_TILE // TILE))


def _flash_scratch():
    scores = pltpu.VMEM((ROW_TILE, GROUP_KEYS), F32)
    probs = pltpu.VMEM((ROW_TILE, GROUP_KEYS), BF16)
    rescale = pltpu.VMEM((ROW_TILE, 1), F32)
    values = pltpu.VMEM((2, GROUP_KEYS, 2 * HEAD_DIM), BF16)
    running_max = pltpu.VMEM((SEQ, 1), F32)
    acc = pltpu.VMEM((SEQ, 2 * HEAD_DIM), F32)
    return [scores, scores, probs, probs, rescale, rescale, values, running_max, acc]


def _tile_rows(tile):
    return pl.ds(pl.multiple_of(tile * ROW_TILE, ROW_TILE), ROW_TILE)


def _flash_sweep(grp, qa_ref, group_keys, group_values, bias_ref,
                 s0_ref, s1_ref, p0_ref, p1_ref, alpha0_ref, alpha1_ref, va_ref, m_ref, acc_ref):
    half = GROUP_KEYS // 2
    first = grp * ROW_TILES_PER_GROUP
    last = N_ROW_TILES - 1
    va_new = va_ref.at[grp % 2]
    va_old = va_ref.at[1 - grp % 2]
    va_new[:, :HEAD_DIM] = group_values()

    @pl.when(grp < 2)
    def _():
        va_new[:, HEAD_DIM:] = jnp.ones((GROUP_KEYS, HEAD_DIM), BF16)

    def scores(tile, s_ref, n_keys=GROUP_KEYS, biased=True):
        raw = lax.dot_general(qa_ref[_tile_rows(tile), :], group_keys(n_keys), _NT,
                              preferred_element_type=F32)
        if not biased:
            s_ref[:, :n_keys] = raw
            return
        for r in range(ROW_TILE // TILE):
            for j in range(n_keys // TILE):
                bias = bias_ref[0, _table_index(tile * (ROW_TILE // TILE) + r - (grp * GROUP + j))]
                s_ref[r * TILE:(r + 1) * TILE, j * TILE:(j + 1) * TILE] = (
                    raw[r * TILE:(r + 1) * TILE, j * TILE:(j + 1) * TILE] + bias)

    def softmax(tile, s_ref, p_ref, alpha_ref, n_keys=GROUP_KEYS):
        for c in range(ROW_TILE // SOFTMAX_ROWS):
            chunk = slice(c * SOFTMAX_ROWS, (c + 1) * SOFTMAX_ROWS)
            rows = pl.ds(pl.multiple_of(tile * ROW_TILE + c * SOFTMAX_ROWS, SOFTMAX_ROWS), SOFTMAX_ROWS)
            s = s_ref[chunk, :n_keys]
            m_prev = m_ref[rows, :]
            m_new = jnp.maximum(m_prev, jnp.max(s, axis=1, keepdims=True))
            alpha_ref[chunk, :] = jnp.exp2(m_prev - m_new)
            m_ref[rows, :] = m_new
            p_ref[chunk, :n_keys] = jnp.exp2(s - m_new).astype(BF16)

    def accumulate(tile, p_ref, alpha_ref, va, n_keys=GROUP_KEYS):
        rows = _tile_rows(tile)
        acc_ref[rows, :] = (alpha_ref[...] * acc_ref[rows, :]
                            + jnp.dot(p_ref[:, :n_keys], va[:n_keys, :], preferred_element_type=F32))

    def enter(with_pending):
        scores(first, s0_ref, half)
        if with_pending:
            softmax(last, s1_ref, p1_ref, alpha1_ref)
            accumulate(last - 1, p0_ref, alpha0_ref, va_old)
        scores(first + 1, s1_ref)
        softmax(first, s0_ref, p0_ref, alpha0_ref, half)
        if with_pending:
            accumulate(last, p1_ref, alpha1_ref, va_old)

    def second_pair():
        scores(first + 2, s0_ref)
        softmax(first + 1, s1_ref, p1_ref, alpha1_ref)
        accumulate(first, p0_ref, alpha0_ref, va_new, half)
        scores(first + 3, s1_ref)
        softmax(first + 2, s0_ref, p0_ref, alpha0_ref)
        accumulate(first + 1, p1_ref, alpha1_ref, va_new)

    def pair(t, first_biased=False, values_first=False):
        tile = first + 3 + 2 * t
        if values_first:
            accumulate(tile - 1, p0_ref, alpha0_ref, va_new)
        scores(tile + 1, s0_ref, biased=first_biased)
        softmax(tile, s1_ref, p1_ref, alpha1_ref)
        if not values_first:
            accumulate(tile - 1, p0_ref, alpha0_ref, va_new)
        else:
            accumulate(tile, p1_ref, alpha1_ref, va_new)
        scores(tile + 2, s1_ref, biased=False)
        softmax(tile + 1, s0_ref, p0_ref, alpha0_ref)
        if not values_first:
            accumulate(tile, p1_ref, alpha1_ref, va_new)

    assert BIAS_FREE_FROM_ROW_TILE == 5
    n_pairs = jnp.maximum(last - first - 3, 0) // 2
    middle = (grp > 0) & (grp < N_GROUPS - 1)

    def first_pairs(even_count):
        pair(0, first_biased=True, values_first=even_count)
        if even_count:
            pair(1, values_first=True)

    @pl.when(grp == 0)
    def _():
        assert ((N_ROW_TILES - 4) // 2) % 2 == 0
        enter(False)
        second_pair()
        first_pairs(True)

    @pl.when(middle & (n_pairs % 2 == 1))
    def _():
        enter(True)
        second_pair()
        first_pairs(False)

    @pl.when(middle & (n_pairs > 0) & (n_pairs % 2 == 0))
    def _():
        enter(True)
        second_pair()
        first_pairs(True)

    @pl.when(middle & (n_pairs == 0))
    def _():
        enter(True)
        second_pair()

    @pl.when(grp == N_GROUPS - 1)
    def _():
        enter(True)
        softmax(last, s1_ref, p1_ref, alpha1_ref)
        accumulate(last - 1, p0_ref, alpha0_ref, va_new, half)
        accumulate(last, p1_ref, alpha1_ref, va_new)

    def two_pairs(u, carry):
        t = 2 - n_pairs % 2 + 2 * u
        pair(t, values_first=True)
        pair(t + 1, values_first=True)
        return carry

    lax.fori_loop(0, jnp.maximum(n_pairs - 1, 0) // 2, two_pairs, 0)


def _reset_state(rows, m_ref, acc_ref):
    m_ref[rows, :] = jnp.full((rows.size, 1), NEG_INF, F32)
    acc_ref[rows, :] = jnp.zeros((rows.size, 2 * HEAD_DIM), F32)


def _attention_of(acc):
    return acc[:, :HEAD_DIM] / acc[:, HEAD_DIM:]


GATE_QUERIES = 8 * TILE


def _moba_gating(q_ref, km_ref, qa_ref, m_ref, acc_ref):
    blk = lax.broadcasted_iota(jnp.int32, (N_KV_BLOCKS, GATE_QUERIES), 0)
    blkf = blk.astype(F32)
    tile_in_step = lax.broadcasted_iota(jnp.int32, (N_KV_BLOCKS, GATE_QUERIES), 1) // TILE

    def gate_step(step, carry):
        rows = pl.ds(pl.multiple_of(step * GATE_QUERIES, GATE_QUERIES), GATE_QUERIES)
        _reset_state(rows, m_ref, acc_ref)
        q = q_ref[rows, :]
        gate = lax.dot_general(km_ref[...].astype(BF16), q, _NT, preferred_element_type=F32)
        tile = step * (GATE_QUERIES // TILE) + tile_in_step
        past = blk < tile
        g = jnp.where(past, gate, NEG_INF)
        madd = jnp.full(gate.shape, NEG_INF, F32)
        for _ in range(MOBA_TOPK):
            mx = jnp.max(g, axis=0, keepdims=True)
            first = jnp.min(jnp.where(g == mx, blkf, float(N_KV_BLOCKS)), axis=0, keepdims=True)
            hit = blkf == first
            madd = jnp.where(hit & past, 0.0, madd)
            g = jnp.where(hit, -jnp.inf, g)
        madd = jnp.where(blk == tile, 0.0, madd)
        madd = jnp.concatenate([madd, jnp.zeros((HEAD_DIM - N_KV_BLOCKS, GATE_QUERIES), F32)], axis=0)
        qa_ref[rows, :HEAD_DIM] = q
        qa_ref[rows, HEAD_DIM:] = madd.T.astype(BF16)
        return carry

    lax.fori_loop(0, SEQ // GATE_QUERIES, gate_step, 0, unroll=2)


def _step_rows(group_in_step, n_rows=GROUP_KEYS):
    return pl.ds(pl.multiple_of(group_in_step * GROUP_KEYS, GROUP_KEYS), n_rows)


def _moba_kernel(q_ref, k_ref, v_ref, km_ref, bias_ref, o_ref, qa_ref, ka_ref, *flash_refs):
    step = pl.program_id(2)
    m_ref, acc_ref = flash_refs[-2:]

    @pl.when(step == 0)
    def _():
        _moba_gating(q_ref, km_ref, qa_ref, m_ref, acc_ref)

    lane = lax.broadcasted_iota(jnp.int32, (GROUP_KEYS, HEAD_DIM), 1)
    block_in_group = lax.broadcasted_iota(jnp.int32, (GROUP_KEYS, HEAD_DIM), 0) // TILE

    def group(i, carry):
        grp = step * GROUPS_PER_STEP + i
        ka_ref[:, :HEAD_DIM] = k_ref[_step_rows(i), :]
        ka_ref[:, HEAD_DIM:] = jnp.where(lane == grp * GROUP + block_in_group, 1.0, 0.0).astype(BF16)
        _flash_sweep(grp, qa_ref, lambda n: ka_ref[:n, :], lambda: v_ref[_step_rows(i), :], bias_ref,
                     *flash_refs)
        return carry

    lax.fori_loop(0, GROUPS_PER_STEP, group, 0)

    @pl.when(step == N_STEPS - 1)
    def _():
        def finish(tile, carry):
            rows = _tile_rows(tile)
            o_ref[rows, :] = _attention_of(acc_ref[rows, :]).astype(BF16)
            return carry

        lax.fori_loop(0, N_ROW_TILES, finish, 0)


def _moba_attention(qkv, kmean, bias):
    return pl.pallas_call(
        _moba_kernel,
        grid=(BATCH, N_HEADS, N_STEPS),
        in_specs=[
            pl.BlockSpec((SEQ, HEAD_DIM), lambda b, h, g: (b, h)),
            pl.BlockSpec((STEP_KEYS, HEAD_DIM), lambda b, h, g: (b * N_STEPS + g, N_HEADS + h)),
            pl.BlockSpec((STEP_KEYS, HEAD_DIM), lambda b, h, g: (b * N_STEPS + g, 2 * N_HEADS + h)),
            pl.BlockSpec((N_KV_BLOCKS, HEAD_DIM), lambda b, h, g: (b, h)),
            pl.BlockSpec((1, N_TABLE_TILES, TILE, TILE), lambda b, h, g: (h, 0, 0, 0)),
        ],
        out_specs=pl.BlockSpec((SEQ, HEAD_DIM), lambda b, h, g: (b, h)),
        out_shape=jax.ShapeDtypeStruct((ROWS, D_MODEL), BF16),
        scratch_shapes=[
            pltpu.VMEM((SEQ, 2 * HEAD_DIM), BF16),
            pltpu.VMEM((GROUP_KEYS, 2 * HEAD_DIM), BF16),
        ] + _flash_scratch(),
        compiler_params=pltpu.CompilerParams(
            dimension_semantics=("arbitrary", "arbitrary", "arbitrary"), vmem_limit_bytes=VMEM_LIMIT),
        name="moba_attn",
    )(qkv, qkv, qkv, kmean, bias)


def _diff_kernel(q_ref, k_ref, v_ref, bias_ref, lam_ref, g_ref, o_ref, qa_ref, map0_ref, *flash_refs,
                 lambda_init):
    which = pl.program_id(2)
    step = pl.program_id(3)
    m_ref, acc_ref = flash_refs[-2:]

    @pl.when(step == 0)
    def _():
        lane_map = lax.broadcasted_iota(jnp.int32, (ROW_TILE, HEAD_DIM), 1) // DIFF_HALF

        def split(tile, carry):
            rows = _tile_rows(tile)
            _reset_state(rows, m_ref, acc_ref)
            q = q_ref[rows, :]
            qa_ref[rows, :] = jnp.where(lane_map == which, q, jnp.zeros_like(q))
            return carry

        lax.fori_loop(0, N_ROW_TILES, split, 0)

    def group(i, carry):
        _flash_sweep(step * GROUPS_PER_STEP + i, qa_ref, lambda n: k_ref[_step_rows(i, n), :],
                     lambda: v_ref[_step_rows(i), :], bias_ref, *flash_refs)
        return carry

    lax.fori_loop(0, GROUPS_PER_STEP, group, 0)

    @pl.when((step == N_STEPS - 1) & (which == 0))
    def _():
        def keep(tile, carry):
            rows = _tile_rows(tile)
            map0_ref[rows, :] = _attention_of(acc_ref[rows, :])
            return carry

        lax.fori_loop(0, N_ROW_TILES, keep, 0)

    @pl.when((step == N_STEPS - 1) & (which == 1))
    def _():
        lam = lam_ref[...]
        lam_full = (jnp.exp(jnp.sum(lam[0:1] * lam[1:2], axis=1, keepdims=True))
                    - jnp.exp(jnp.sum(lam[2:3] * lam[3:4], axis=1, keepdims=True)) + lambda_init)

        def finish(tile, carry):
            rows = _tile_rows(tile)
            o = map0_ref[rows, :] - lam_full * _attention_of(acc_ref[rows, :])
            o = o * lax.rsqrt(jnp.mean(o * o, axis=-1, keepdims=True) + RMS_EPS) * g_ref[...]
            o_ref[rows, :] = (o * (1.0 - lambda_init)).astype(BF16)
            return carry

        lax.fori_loop(0, N_ROW_TILES, finish, 0, unroll=2)


def _diff_attention(qkv, bias, lam, subln_g, layer_idx):
    lambda_init = 0.8 - 0.6 * math.exp(-0.3 * layer_idx)
    return pl.pallas_call(
        functools.partial(_diff_kernel, lambda_init=lambda_init),
        grid=(BATCH, N_HEADS, 2, N_STEPS),
        in_specs=[
            pl.BlockSpec((SEQ, HEAD_DIM), lambda b, h, c, g: (b, h)),
            pl.BlockSpec((STEP_KEYS, HEAD_DIM), lambda b, h, c, g: (b * N_STEPS + g, N_HEADS + h)),
            pl.BlockSpec((STEP_KEYS, HEAD_DIM), lambda b, h, c, g: (b * N_STEPS + g, 2 * N_HEADS + h)),
            pl.BlockSpec((1, N_TABLE_TILES, TILE, TILE), lambda b, h, c, g: (h, 0, 0, 0)),
            pl.BlockSpec((4, DIFF_HALF), lambda b, h, c, g: (0, 0)),
            pl.BlockSpec((1, HEAD_DIM), lambda b, h, c, g: (0, 0)),
        ],
        out_specs=pl.BlockSpec((SEQ, HEAD_DIM), lambda b, h, c, g: (b, h)),
        out_shape=jax.ShapeDtypeStruct((ROWS, D_MODEL), BF16),
        scratch_shapes=[
            pltpu.VMEM((SEQ, HEAD_DIM), BF16),
            pltpu.VMEM((SEQ, HEAD_DIM), F32),
        ] + _flash_scratch(),
        compiler_params=pltpu.CompilerParams(
            dimension_semantics=("arbitrary", "arbitrary", "arbitrary", "arbitrary"),
            vmem_limit_bytes=VMEM_LIMIT),
        name="diff_attn",
    )(qkv, qkv, qkv, bias, lam, subln_g.reshape(1, HEAD_DIM))


FFN_TM = 1024
FFN_TF = 256


def _rms(y, g):
    return y * lax.rsqrt(jnp.mean(y * y, axis=-1, keepdims=True) + RMS_EPS) * g


def _repack_kernel(wg_ref, wu_ref, o_ref):
    o_ref[0, 0, :, :FFN_TF] = wg_ref[0].astype(BF16)
    o_ref[0, 0, :, FFN_TF:] = wu_ref[0].astype(BF16)


def _repack_ffn_in(w_in_layers):
    n_chunks = D_FF // FFN_TF
    return pl.pallas_call(
        _repack_kernel,
        grid=(DEPTH, n_chunks),
        in_specs=[
            pl.BlockSpec((1, D_MODEL, FFN_TF), lambda l, c: (l, 0, c)),
            pl.BlockSpec((1, D_MODEL, FFN_TF), lambda l, c: (l, 0, n_chunks + c)),
        ],
        out_specs=pl.BlockSpec((1, 1, D_MODEL, 2 * FFN_TF), lambda l, c: (l, c, 0, 0)),
        out_shape=jax.ShapeDtypeStruct((DEPTH, n_chunks, D_MODEL, 2 * FFN_TF), BF16),
        name="repack_ffn_in",
    )(w_in_layers, w_in_layers)


def _mixer_out_ffn_kernel(o_ref, x_ref, wa_ref, gains_ref, mod_ref, wgu_ref, wo_ref, out_ref,
                          xn_ref, h_ref, acc_ref):
    c = pl.program_id(1)
    last = pl.num_programs(1) - 1

    def chunk(h):
        gu = jnp.dot(h, wgu_ref[0, 0], preferred_element_type=F32)
        gp, up = gu[:, :FFN_TF], gu[:, FFN_TF:]
        act = (gp / (1.0 + jnp.exp(-gp)) * up).astype(BF16)
        return jnp.dot(act, wo_ref[0].astype(BF16), preferred_element_type=F32)

    @pl.when(c == 0)
    def _():
        y = jnp.dot(o_ref[...], wa_ref[...], preferred_element_type=F32)
        xn = x_ref[...] + _mod(mod_ref, MOD_GATE_MIXER) * _rms(y, _gain(gains_ref, GAIN_MIXER_OUT))
        xn_ref[...] = xn
        h = _norm_mod(xn, _gain(gains_ref, GAIN_FFN_IN), _mod(mod_ref, MOD_SCALE_FFN),
                      _mod(mod_ref, MOD_SHIFT_FFN)).astype(BF16)
        h_ref[...] = h
        acc_ref[...] = chunk(h)

    @pl.when((c > 0) & (c < last))
    def _():
        acc_ref[...] += chunk(h_ref[...])

    @pl.when(c == last)
    def _():
        y = acc_ref[...] + chunk(h_ref[...])
        out_ref[...] = xn_ref[...] + _mod(mod_ref, MOD_GATE_FFN) * _rms(y, _gain(gains_ref, GAIN_FFN_OUT))


def _mixer_out_ffn(o, x2, w_attn_out_bf16, gains, mod, layer, w_in_packed, w_out_layers):
    n_chunks = D_FF // FFN_TF
    return pl.pallas_call(
        _mixer_out_ffn_kernel,
        grid=(ROWS // FFN_TM, n_chunks),
        in_specs=[
            pl.BlockSpec((FFN_TM, D_MODEL), lambda i, c: (i, 0)),
            pl.BlockSpec((FFN_TM, D_MODEL), lambda i, c: (i, 0)),
            pl.BlockSpec((D_MODEL, D_MODEL), lambda i, c: (0, 0), pipeline_mode=pl.Buffered(1)),
            *_layer_vector_specs(layer, FFN_TM),
            pl.BlockSpec((1, 1, D_MODEL, 2 * FFN_TF), lambda i, c: (layer, c, 0, 0)),
            pl.BlockSpec((1, FFN_TF, D_MODEL), lambda i, c: (layer, c, 0)),
        ],
        out_specs=pl.BlockSpec((FFN_TM, D_MODEL), lambda i, c: (i, 0)),
        out_shape=jax.ShapeDtypeStruct((ROWS, D_MODEL), F32),
        scratch_shapes=[
            pltpu.VMEM((FFN_TM, D_MODEL), F32),
            pltpu.VMEM((FFN_TM, D_MODEL), BF16),
            pltpu.VMEM((FFN_TM, D_MODEL), F32),
        ],
        compiler_params=pltpu.CompilerParams(
            dimension_semantics=("parallel", "arbitrary"), vmem_limit_bytes=VMEM_LIMIT),
        name="mixer_out_ffn",
    )(o, x2, w_attn_out_bf16, gains, mod, w_in_packed, w_out_layers)


def kernel(x, c, rel_bias, ada_w, ada_b, norm_g, moba_w_qkv, moba_w_o, diff_w_qkv, diff_w_o, diff_lambda,
           diff_subln_g, ffn_w_in, ffn_w_out):
    x2 = x.reshape(ROWS, D_MODEL)
    mod = _adaln_mod(c, ada_w, ada_b).reshape(DEPTH, BATCH, 6, D_MODEL)
    bias = _bias_tiles(rel_bias)
    w_in_packed = _repack_ffn_in(ffn_w_in)
    for i in range(DEPTH):
        if i % 2 == 0:
            qkv, kmean = _qkv_proj(x2, norm_g, mod, i, moba_w_qkv, i // 2, HEAD_DIM ** -0.5 * LOG2_E, True)
            o = _moba_attention(qkv, kmean.reshape(BATCH * N_KV_BLOCKS, D_MODEL), bias)
            w_o = moba_w_o
        else:
            (qkv,) = _qkv_proj(x2, norm_g, mod, i, diff_w_qkv, i // 2, DIFF_HALF ** -0.5 * LOG2_E, False)
            o = _diff_attention(qkv, bias, diff_lambda[i // 2], diff_subln_g[i // 2], i)
            w_o = diff_w_o
        x2 = _mixer_out_ffn(o, x2, w_o[i // 2].astype(BF16), norm_g, mod, i, w_in_packed, ffn_w_out)
    return x2.reshape(BATCH, SEQ, D_MODEL)
```

```python
import functools
import math

import numpy as np
import jax
import jax.numpy as jnp
from jax import lax
from jax.experimental import pallas as pl
from jax.experimental.pallas import tpu as pltpu

D_MODEL = 1024
BATCH = 2
SEQ = 8192
DEPTH = 2
N_HEADS = 8
HEAD_DIM = D_MODEL // N_HEADS
DIFF_HALF = HEAD_DIM // 2
MOBA_BLOCK = 256
MOBA_TOPK = 3
D_FF = 2816
N_BUCKETS = 32
MAX_EXACT = N_BUCKETS // 2
MAX_DISTANCE = 2048
RMS_EPS = 1e-6
NEG_INF = -1e30

ROWS = BATCH * SEQ
N_KV_BLOCKS = SEQ // MOBA_BLOCK
TILE = MOBA_BLOCK
VMEM_LIMIT = 48 * 1024 * 1024

F32 = jnp.float32
BF16 = jnp.bfloat16


def _bucket_of_distance():
    n = np.arange(MAX_DISTANCE + 1)
    nf = np.maximum(n, 1).astype(np.float64)
    val = np.log(nf / MAX_EXACT) / math.log(MAX_DISTANCE / MAX_EXACT) * (N_BUCKETS - MAX_EXACT)
    frac = np.abs(val - np.round(val))
    assert np.all((frac > 5e-5) | (n <= MAX_EXACT) | (n == MAX_DISTANCE))
    large = np.minimum(MAX_EXACT + np.floor(val + 1e-9).astype(np.int64), N_BUCKETS - 1)
    return np.where(n < MAX_EXACT, n, large)


_BUCKETS = _bucket_of_distance()
FAR_DISTANCE = int(np.min(np.nonzero(_BUCKETS == N_BUCKETS - 1)[0]))
N_BIAS_TILES = (FAR_DISTANCE + TILE - 1) // TILE + 1
N_TABLE_TILES = N_BIAS_TILES + 2


def _bias_bucket_tiles():
    i = np.arange(TILE)[:, None]
    j = np.arange(TILE)[None, :]
    out = []
    for t in range(-1, N_BIAS_TILES + 1):
        rel = t * TILE + i - j
        out.append(np.where(rel >= 0, _BUCKETS[np.clip(rel, 0, MAX_DISTANCE)], -1))
    return np.stack(out).astype(np.int32)


def _table_index(tiles_behind):
    return jnp.clip(tiles_behind, -1, N_BIAS_TILES) + 1


MOD_TN = 1024


def _mod_kernel(ct_ref, w_ref, b_ref, o_ref):
    ct = ct_ref[...]
    cact = ct / (1.0 + jnp.exp(-ct))
    w = w_ref[0]
    for b in range(BATCH):
        row = jnp.sum(w * cact[:, b:b + 1], axis=0, keepdims=True)
        o_ref[0, b:b + 1, :] = row + b_ref[0]


def _adaln_mod(c, ada_w, ada_b):
    n_out = ada_w.shape[-1]
    return pl.pallas_call(
        _mod_kernel,
        grid=(DEPTH, n_out // MOD_TN),
        in_specs=[
            pl.BlockSpec((D_MODEL, BATCH), lambda i, n: (0, 0)),
            pl.BlockSpec((1, D_MODEL, MOD_TN), lambda i, n: (i, 0, n)),
            pl.BlockSpec((1, 1, MOD_TN), lambda i, n: (i, 0, n)),
        ],
        out_specs=pl.BlockSpec((1, BATCH, MOD_TN), lambda i, n: (i, 0, n)),
        out_shape=jax.ShapeDtypeStruct((DEPTH, BATCH, n_out), F32),
        name="adaln_mod",
    )(c.T, ada_w, ada_b.reshape(DEPTH, 1, n_out))


LOG2_E = math.log2(math.e)


def _bias_kernel(tab_ref, bkt_ref, o_ref, *, buckets_present):
    h = pl.program_id(0)
    far = tab_ref[N_BUCKETS - 1, h]
    for t, present in enumerate(buckets_present):
        bkt = bkt_ref[t]
        acc = jnp.full(bkt.shape, NEG_INF, F32)
        for b in present:
            acc = jnp.where(bkt == b, (tab_ref[b, h] - far) * LOG2_E, acc)
        o_ref[0, t] = acc


def _bias_tiles(rel_bias):
    bkt = _bias_bucket_tiles()
    buckets_present = tuple(tuple(int(b) for b in np.unique(tile) if b >= 0) for tile in bkt)
    return pl.pallas_call(
        functools.partial(_bias_kernel, buckets_present=buckets_present),
        grid=(N_HEADS,),
        in_specs=[
            pl.BlockSpec(memory_space=pltpu.SMEM),
            pl.BlockSpec((N_TABLE_TILES, TILE, TILE), lambda h: (0, 0, 0)),
        ],
        out_specs=pl.BlockSpec((1, N_TABLE_TILES, TILE, TILE), lambda h: (h, 0, 0, 0)),
        out_shape=jax.ShapeDtypeStruct((N_HEADS, N_TABLE_TILES, TILE, TILE), F32),
        name="bias_tiles",
    )(rel_bias, jnp.asarray(bkt))


QKV_TM = 1024


def _norm_mod(x, g, sc, sh):
    y = x * lax.rsqrt(jnp.mean(x * x, axis=-1, keepdims=True) + RMS_EPS)
    return (y * g) * (1.0 + sc) + sh


def _cast_weight_once(w_ref, wb_ref):
    @pl.when(pl.program_id(0) == 0)
    def _():
        wb_ref[...] = w_ref[0].astype(BF16)


GAIN_MIXER_IN, GAIN_MIXER_OUT, GAIN_FFN_IN, GAIN_FFN_OUT = range(4)
MOD_SHIFT_MIXER, MOD_SCALE_MIXER, MOD_GATE_MIXER, MOD_SHIFT_FFN, MOD_SCALE_FFN, MOD_GATE_FFN = range(6)


def _gain(gains_ref, which):
    return gains_ref[0, which:which + 1, :]


def _mod(mod_ref, which):
    return mod_ref[0, 0, which:which + 1, :]


def _layer_vector_specs(layer, rows_per_tile):
    tiles_per_batch = SEQ // rows_per_tile
    return (pl.BlockSpec((1, 4, D_MODEL), lambda i, *_: (layer, 0, 0)),
            pl.BlockSpec((1, 1, 6, D_MODEL), lambda i, *_: (layer, i // tiles_per_batch, 0, 0)))


def _qkv_kernel(x_ref, gains_ref, mod_ref, w_ref, o_ref, *rest, q_scale):
    *km_ref, wb_ref = rest
    _cast_weight_once(w_ref, wb_ref)
    hb = _norm_mod(x_ref[...], _gain(gains_ref, GAIN_MIXER_IN), _mod(mod_ref, MOD_SCALE_MIXER),
                   _mod(mod_ref, MOD_SHIFT_MIXER)).astype(BF16)
    for n in range(3):
        r = jnp.dot(hb, wb_ref[:, n * D_MODEL:(n + 1) * D_MODEL], preferred_element_type=F32)
        if n == 0:
            r = r * q_scale
        if n == 1 and km_ref:
            km_ref[0][0] = jnp.mean(r.reshape(QKV_TM // MOBA_BLOCK, MOBA_BLOCK, D_MODEL), axis=1)
        o_ref[:, n * D_MODEL:(n + 1) * D_MODEL] = r.astype(BF16)


def _qkv_proj(x2, gains, mod, layer, w_layers, w_layer, q_scale, with_kmean):
    n_tiles = ROWS // QKV_TM
    out_shape = [jax.ShapeDtypeStruct((ROWS, 3 * D_MODEL), BF16)]
    out_specs = [pl.BlockSpec((QKV_TM, 3 * D_MODEL), lambda i: (i, 0))]
    if with_kmean:
        per_tile = QKV_TM // MOBA_BLOCK
        out_shape.append(jax.ShapeDtypeStruct((n_tiles, per_tile, D_MODEL), F32))
        out_specs.append(pl.BlockSpec((1, per_tile, D_MODEL), lambda i: (i, 0, 0)))
    return pl.pallas_call(
        functools.partial(_qkv_kernel, q_scale=q_scale),
        grid=(n_tiles,),
        in_specs=[
            pl.BlockSpec((QKV_TM, D_MODEL), lambda i: (i, 0)),
            *_layer_vector_specs(layer, QKV_TM),
            pl.BlockSpec((1, D_MODEL, 3 * D_MODEL), lambda i: (w_layer, 0, 0), pipeline_mode=pl.Buffered(1)),
        ],
        out_specs=out_specs,
        out_shape=out_shape,
        scratch_shapes=[pltpu.VMEM((D_MODEL, 3 * D_MODEL), BF16)],
        compiler_params=pltpu.CompilerParams(
            dimension_semantics=("arbitrary",), vmem_limit_bytes=VMEM_LIMIT),
        name="qkv_moba" if with_kmean else "qkv_diff",
    )(x2, gains, mod, w_layers)


_NT = (((1,), (1,)), ((), ()))
GROUP = 4
GROUP_KEYS = GROUP * TILE
N_GROUPS = SEQ // GROUP_KEYS
GROUPS_PER_STEP = 4
N_STEPS = N_GROUPS // GROUPS_PER_STEP
STEP_KEYS = GROUPS_PER_STEP * GROUP_KEYS
ROW_TILE = 2 * TILE
N_ROW_TILES = SEQ // ROW_TILE
ROW_TILES_PER_GROUP = GROUP_KEYS // ROW_TILE
SOFTMAX_ROWS = 64
BIAS_FREE_FROM_ROW_TILE = -(-(N_BIAS_TILES + GROUP - 1) // (ROW_TILE // TILE))


def _flash_scratch():
    scores = pltpu.VMEM((ROW_TILE, GROUP_KEYS), F32)
    probs = pltpu.VMEM((ROW_TILE, GROUP_KEYS), BF16)
    rescale = pltpu.VMEM((ROW_TILE, 1), F32)
    values = pltpu.VMEM((2, GROUP_KEYS, 2 * HEAD_DIM), BF16)
    running_max = pltpu.VMEM((SEQ, 1), F32)
    acc = pltpu.VMEM((SEQ, 2 * HEAD_DIM), F32)
    return [scores, scores, probs, probs, rescale, rescale, values, running_max, acc]


def _tile_rows(tile):
    return pl.ds(pl.multiple_of(tile * ROW_TILE, ROW_TILE), ROW_TILE)


def _flash_sweep(grp, qa_ref, group_keys, group_values, bias_ref,
                 s0_ref, s1_ref, p0_ref, p1_ref, alpha0_ref, alpha1_ref, va_ref, m_ref, acc_ref):
    half = GROUP_KEYS // 2
    first = grp * ROW_TILES_PER_GROUP
    last = N_ROW_TILES - 1
    va_new = va_ref.at[grp % 2]
    va_old = va_ref.at[1 - grp % 2]
    va_new[:, :HEAD_DIM] = group_values()

    @pl.when(grp < 2)
    def _():
        va_new[:, HEAD_DIM:] = jnp.ones((GROUP_KEYS, HEAD_DIM), BF16)

    def scores(tile, s_ref, n_keys=GROUP_KEYS, biased=True):
        raw = lax.dot_general(qa_ref[_tile_rows(tile), :], group_keys(n_keys), _NT,
                              preferred_element_type=F32)
        if not biased:
            s_ref[:, :n_keys] = raw
            return
        for r in range(ROW_TILE // TILE):
            for j in range(n_keys // TILE):
                bias = bias_ref[0, _table_index(tile * (ROW_TILE // TILE) + r - (grp * GROUP + j))]
                s_ref[r * TILE:(r + 1) * TILE, j * TILE:(j + 1) * TILE] = (
                    raw[r * TILE:(r + 1) * TILE, j * TILE:(j + 1) * TILE] + bias)

    def softmax(tile, s_ref, p_ref, alpha_ref, n_keys=GROUP_KEYS, fresh=False):
        for c in range(ROW_TILE // SOFTMAX_ROWS):
            chunk = slice(c * SOFTMAX_ROWS, (c + 1) * SOFTMAX_ROWS)
            rows = pl.ds(pl.multiple_of(tile * ROW_TILE + c * SOFTMAX_ROWS, SOFTMAX_ROWS), SOFTMAX_ROWS)
            s = s_ref[chunk, :n_keys]
            m_new = jnp.max(s, axis=1, keepdims=True)
            if fresh:
                alpha_ref[chunk, :] = jnp.zeros((SOFTMAX_ROWS, 1), F32)
            else:
                m_prev = m_ref[rows, :]
                m_new = jnp.maximum(m_prev, m_new)
                alpha_ref[chunk, :] = jnp.exp2(m_prev - m_new)
            m_ref[rows, :] = m_new
            p_ref[chunk, :n_keys] = jnp.exp2(s - m_new).astype(BF16)

    def accumulate(tile, p_ref, alpha_ref, va, n_keys=GROUP_KEYS, fresh=False):
        rows = _tile_rows(tile)
        update = jnp.dot(p_ref[:, :n_keys], va[:n_keys, :], preferred_element_type=F32)
        acc_ref[rows, :] = update if fresh else alpha_ref[...] * acc_ref[rows, :] + update

    def enter(with_pending):
        scores(first, s0_ref, half)
        if with_pending:
            softmax(last, s1_ref, p1_ref, alpha1_ref)
            accumulate(last - 1, p0_ref, alpha0_ref, va_old)
        scores(first + 1, s1_ref)
        softmax(first, s0_ref, p0_ref, alpha0_ref, half, fresh=not with_pending)
        if with_pending:
            accumulate(last, p1_ref, alpha1_ref, va_old)

    def second_pair(fresh=False):
        scores(first + 2, s0_ref)
        softmax(first + 1, s1_ref, p1_ref, alpha1_ref, fresh=fresh)
        accumulate(first, p0_ref, alpha0_ref, va_new, half, fresh=fresh)
        scores(first + 3, s1_ref)
        softmax(first + 2, s0_ref, p0_ref, alpha0_ref, fresh=fresh)
        accumulate(first + 1, p1_ref, alpha1_ref, va_new, fresh=fresh)

    def pair(t, first_biased=False, values_first=False, fresh=False):
        tile = first + 3 + 2 * t
        if values_first:
            accumulate(tile - 1, p0_ref, alpha0_ref, va_new, fresh=fresh)
        scores(tile + 1, s0_ref, biased=first_biased)
        softmax(tile, s1_ref, p1_ref, alpha1_ref, fresh=fresh)
        if not values_first:
            accumulate(tile - 1, p0_ref, alpha0_ref, va_new, fresh=fresh)
        else:
            accumulate(tile, p1_ref, alpha1_ref, va_new, fresh=fresh)
        scores(tile + 2, s1_ref, biased=False)
        softmax(tile + 1, s0_ref, p0_ref, alpha0_ref, fresh=fresh)
        if not values_first:
            accumulate(tile, p1_ref, alpha1_ref, va_new, fresh=fresh)

    assert BIAS_FREE_FROM_ROW_TILE == 5
    n_pairs = jnp.maximum(last - first - 3, 0) // 2
    middle = (grp > 0) & (grp < N_GROUPS - 1)

    def first_pairs(even_count, fresh=False):
        pair(0, first_biased=True, values_first=even_count, fresh=fresh)
        if even_count:
            pair(1, values_first=True, fresh=fresh)

    @pl.when(grp == 0)
    def _():
        assert ((N_ROW_TILES - 4) // 2) % 2 == 0
        enter(False)
        second_pair(fresh=True)
        first_pairs(True, fresh=True)

    @pl.when(middle & (n_pairs % 2 == 1))
    def _():
        enter(True)
        second_pair()
        first_pairs(False)

    @pl.when(middle & (n_pairs > 0) & (n_pairs % 2 == 0))
    def _():
        enter(True)
        second_pair()
        first_pairs(True)

    @pl.when(middle & (n_pairs == 0))
    def _():
        enter(True)
        second_pair()

    @pl.when(grp == N_GROUPS - 1)
    def _():
        enter(True)
        softmax(last, s1_ref, p1_ref, alpha1_ref)
        accumulate(last - 1, p0_ref, alpha0_ref, va_new, half)
        accumulate(last, p1_ref, alpha1_ref, va_new)

    def two_pairs(u, carry, fresh):
        t = 2 - n_pairs % 2 + 2 * u
        pair(t, values_first=True, fresh=fresh)
        pair(t + 1, values_first=True, fresh=fresh)
        return carry

    n_trips = jnp.maximum(n_pairs - 1, 0) // 2
    lax.fori_loop(0, jnp.where(grp == 0, n_trips, 0), functools.partial(two_pairs, fresh=True), 0)
    lax.fori_loop(0, jnp.where(grp == 0, 0, n_trips), functools.partial(two_pairs, fresh=False), 0)


def _reset_pending_state(m_ref, acc_ref):
    rows = pl.ds(SEQ - 2 * ROW_TILE, 2 * ROW_TILE)
    m_ref[rows, :] = jnp.full((rows.size, 1), NEG_INF, F32)
    acc_ref[rows, :] = jnp.zeros((rows.size, 2 * HEAD_DIM), F32)


def _attention_of(acc):
    return acc[:, :HEAD_DIM] / acc[:, HEAD_DIM:]


GATE_QUERIES = 8 * TILE


def _moba_gating(q_ref, km_ref, qa_ref):
    blk = lax.broadcasted_iota(jnp.int32, (N_KV_BLOCKS, GATE_QUERIES), 0)
    blkf = blk.astype(F32)
    tile_in_step = lax.broadcasted_iota(jnp.int32, (N_KV_BLOCKS, GATE_QUERIES), 1) // TILE

    def gate_step(step, carry):
        rows = pl.ds(pl.multiple_of(step * GATE_QUERIES, GATE_QUERIES), GATE_QUERIES)
        q = q_ref[rows, :]
        gate = lax.dot_general(km_ref[...].astype(BF16), q, _NT, preferred_element_type=F32)
        tile = step * (GATE_QUERIES // TILE) + tile_in_step
        past = blk < tile
        g = jnp.where(past, gate, NEG_INF)
        madd = jnp.full(gate.shape, NEG_INF, F32)
        for _ in range(MOBA_TOPK):
            mx = jnp.max(g, axis=0, keepdims=True)
            first = jnp.min(jnp.where(g == mx, blkf, float(N_KV_BLOCKS)), axis=0, keepdims=True)
            hit = blkf == first
            madd = jnp.where(hit & past, 0.0, madd)
            g = jnp.where(hit, -jnp.inf, g)
        madd = jnp.where(blk == tile, 0.0, madd)
        madd = jnp.concatenate([madd, jnp.zeros((HEAD_DIM - N_KV_BLOCKS, GATE_QUERIES), F32)], axis=0)
        qa_ref[rows, :HEAD_DIM] = q
        qa_ref[rows, HEAD_DIM:] = madd.T.astype(BF16)
        return carry

    lax.fori_loop(0, SEQ // GATE_QUERIES, gate_step, 0, unroll=2)


def _step_rows(group_in_step, n_rows=GROUP_KEYS):
    return pl.ds(pl.multiple_of(group_in_step * GROUP_KEYS, GROUP_KEYS), n_rows)


def _moba_kernel(q_ref, k_ref, v_ref, km_ref, bias_ref, o_ref, qa_ref, ka_ref, *flash_refs):
    step = pl.program_id(2)
    m_ref, acc_ref = flash_refs[-2:]

    @pl.when(step == 0)
    def _():
        _reset_pending_state(m_ref, acc_ref)
        _moba_gating(q_ref, km_ref, qa_ref)

    lane = lax.broadcasted_iota(jnp.int32, (GROUP_KEYS, HEAD_DIM), 1)
    block_in_group = lax.broadcasted_iota(jnp.int32, (GROUP_KEYS, HEAD_DIM), 0) // TILE

    def group(i, carry):
        grp = step * GROUPS_PER_STEP + i
        ka_ref[:, :HEAD_DIM] = k_ref[_step_rows(i), :]
        ka_ref[:, HEAD_DIM:] = jnp.where(lane == grp * GROUP + block_in_group, 1.0, 0.0).astype(BF16)
        _flash_sweep(grp, qa_ref, lambda n: ka_ref[:n, :], lambda: v_ref[_step_rows(i), :], bias_ref,
                     *flash_refs)
        return carry

    lax.fori_loop(0, GROUPS_PER_STEP, group, 0)

    @pl.when(step == N_STEPS - 1)
    def _():
        def finish(tile, carry):
            rows = _tile_rows(tile)
            o_ref[rows, :] = _attention_of(acc_ref[rows, :]).astype(BF16)
            return carry

        lax.fori_loop(0, N_ROW_TILES, finish, 0)


def _moba_attention(qkv, kmean, bias):
    return pl.pallas_call(
        _moba_kernel,
        grid=(BATCH, N_HEADS, N_STEPS),
        in_specs=[
            pl.BlockSpec((SEQ, HEAD_DIM), lambda b, h, g: (b, h)),
            pl.BlockSpec((STEP_KEYS, HEAD_DIM), lambda b, h, g: (b * N_STEPS + g, N_HEADS + h)),
            pl.BlockSpec((STEP_KEYS, HEAD_DIM), lambda b, h, g: (b * N_STEPS + g, 2 * N_HEADS + h)),
            pl.BlockSpec((N_KV_BLOCKS, HEAD_DIM), lambda b, h, g: (b, h)),
            pl.BlockSpec((1, N_TABLE_TILES, TILE, TILE), lambda b, h, g: (h, 0, 0, 0)),
        ],
        out_specs=pl.BlockSpec((SEQ, HEAD_DIM), lambda b, h, g: (b, h)),
        out_shape=jax.ShapeDtypeStruct((ROWS, D_MODEL), BF16),
        scratch_shapes=[
            pltpu.VMEM((SEQ, 2 * HEAD_DIM), BF16),
            pltpu.VMEM((GROUP_KEYS, 2 * HEAD_DIM), BF16),
        ] + _flash_scratch(),
        compiler_params=pltpu.CompilerParams(
            dimension_semantics=("arbitrary", "arbitrary", "arbitrary"), vmem_limit_bytes=VMEM_LIMIT),
        name="moba_attn",
    )(qkv, qkv, qkv, kmean, bias)


def _diff_kernel(q_ref, k_ref, v_ref, bias_ref, lam_ref, g_ref, o_ref, qa_ref, map0_ref, *flash_refs,
                 lambda_init):
    which = pl.program_id(2)
    step = pl.program_id(3)
    m_ref, acc_ref = flash_refs[-2:]

    @pl.when(step == 0)
    def _():
        lane_map = lax.broadcasted_iota(jnp.int32, (ROW_TILE, HEAD_DIM), 1) // DIFF_HALF
        _reset_pending_state(m_ref, acc_ref)

        def split(tile, carry):
            rows = _tile_rows(tile)
            q = q_ref[rows, :]
            qa_ref[rows, :] = jnp.where(lane_map == which, q, jnp.zeros_like(q))
            return carry

        lax.fori_loop(0, N_ROW_TILES, split, 0)

    def group(i, carry):
        _flash_sweep(step * GROUPS_PER_STEP + i, qa_ref, lambda n: k_ref[_step_rows(i, n), :],
                     lambda: v_ref[_step_rows(i), :], bias_ref, *flash_refs)
        return carry

    lax.fori_loop(0, GROUPS_PER_STEP, group, 0)

    @pl.when((step == N_STEPS - 1) & (which == 0))
    def _():
        def keep(tile, carry):
            rows = _tile_rows(tile)
            map0_ref[rows, :] = _attention_of(acc_ref[rows, :])
            return carry

        lax.fori_loop(0, N_ROW_TILES, keep, 0)

    @pl.when((step == N_STEPS - 1) & (which == 1))
    def _():
        lam = lam_ref[...]
        lam_full = (jnp.exp(jnp.sum(lam[0:1] * lam[1:2], axis=1, keepdims=True))
                    - jnp.exp(jnp.sum(lam[2:3] * lam[3:4], axis=1, keepdims=True)) + lambda_init)

        def finish(tile, carry):
            rows = _tile_rows(tile)
            o = map0_ref[rows, :] - lam_full * _attention_of(acc_ref[rows, :])
            o = o * lax.rsqrt(jnp.mean(o * o, axis=-1, keepdims=True) + RMS_EPS) * g_ref[...]
            o_ref[rows, :] = (o * (1.0 - lambda_init)).astype(BF16)
            return carry

        lax.fori_loop(0, N_ROW_TILES, finish, 0, unroll=2)


def _diff_attention(qkv, bias, lam, subln_g, layer_idx):
    lambda_init = 0.8 - 0.6 * math.exp(-0.3 * layer_idx)
    return pl.pallas_call(
        functools.partial(_diff_kernel, lambda_init=lambda_init),
        grid=(BATCH, N_HEADS, 2, N_STEPS),
        in_specs=[
            pl.BlockSpec((SEQ, HEAD_DIM), lambda b, h, c, g: (b, h)),
            pl.BlockSpec((STEP_KEYS, HEAD_DIM), lambda b, h, c, g: (b * N_STEPS + g, N_HEADS + h)),
            pl.BlockSpec((STEP_KEYS, HEAD_DIM), lambda b, h, c, g: (b * N_STEPS + g, 2 * N_HEADS + h)),
            pl.BlockSpec((1, N_TABLE_TILES, TILE, TILE), lambda b, h, c, g: (h, 0, 0, 0)),
            pl.BlockSpec((4, DIFF_HALF), lambda b, h, c, g: (0, 0)),
            pl.BlockSpec((1, HEAD_DIM), lambda b, h, c, g: (0, 0)),
        ],
        out_specs=pl.BlockSpec((SEQ, HEAD_DIM), lambda b, h, c, g: (b, h)),
        out_shape=jax.ShapeDtypeStruct((ROWS, D_MODEL), BF16),
        scratch_shapes=[
            pltpu.VMEM((SEQ, HEAD_DIM), BF16),
            pltpu.VMEM((SEQ, HEAD_DIM), F32),
        ] + _flash_scratch(),
        compiler_params=pltpu.CompilerParams(
            dimension_semantics=("arbitrary", "arbitrary", "arbitrary", "arbitrary"),
            vmem_limit_bytes=VMEM_LIMIT),
        name="diff_attn",
    )(qkv, qkv, qkv, bias, lam, subln_g.reshape(1, HEAD_DIM))


FFN_TM = 1024
FFN_TF = 256


def _rms(y, g):
    return y * lax.rsqrt(jnp.mean(y * y, axis=-1, keepdims=True) + RMS_EPS) * g


def _mixer_out_ffn_kernel(o_ref, x_ref, wa_ref, gains_ref, mod_ref, wg_ref, wu_ref, wo_ref, out_ref,
                          xn_ref, h_ref, acc_ref):
    c = pl.program_id(1)
    last = pl.num_programs(1) - 1

    def chunk(h):
        gp = jnp.dot(h, wg_ref[0].astype(BF16), preferred_element_type=F32)
        up = jnp.dot(h, wu_ref[0].astype(BF16), preferred_element_type=F32)
        act = (gp / (1.0 + jnp.exp(-gp)) * up).astype(BF16)
        return jnp.dot(act, wo_ref[0].astype(BF16), preferred_element_type=F32)

    @pl.when(c == 0)
    def _():
        y = jnp.dot(o_ref[...], wa_ref[...], preferred_element_type=F32)
        xn = x_ref[...] + _mod(mod_ref, MOD_GATE_MIXER) * _rms(y, _gain(gains_ref, GAIN_MIXER_OUT))
        xn_ref[...] = xn
        h = _norm_mod(xn, _gain(gains_ref, GAIN_FFN_IN), _mod(mod_ref, MOD_SCALE_FFN),
                      _mod(mod_ref, MOD_SHIFT_FFN)).astype(BF16)
        h_ref[...] = h
        acc_ref[...] = chunk(h)

    @pl.when((c > 0) & (c < last))
    def _():
        acc_ref[...] += chunk(h_ref[...])

    @pl.when(c == last)
    def _():
        y = acc_ref[...] + chunk(h_ref[...])
        out_ref[...] = xn_ref[...] + _mod(mod_ref, MOD_GATE_FFN) * _rms(y, _gain(gains_ref, GAIN_FFN_OUT))


def _mixer_out_ffn(o, x2, w_attn_out_bf16, gains, mod, layer, w_in_layers, w_out_layers):
    n_chunks = D_FF // FFN_TF
    return pl.pallas_call(
        _mixer_out_ffn_kernel,
        grid=(ROWS // FFN_TM, n_chunks),
        in_specs=[
            pl.BlockSpec((FFN_TM, D_MODEL), lambda i, c: (i, 0)),
            pl.BlockSpec((FFN_TM, D_MODEL), lambda i, c: (i, 0)),
            pl.BlockSpec((D_MODEL, D_MODEL), lambda i, c: (0, 0), pipeline_mode=pl.Buffered(1)),
            *_layer_vector_specs(layer, FFN_TM),
            pl.BlockSpec((1, D_MODEL, FFN_TF), lambda i, c: (layer, 0, c)),
            pl.BlockSpec((1, D_MODEL, FFN_TF), lambda i, c: (layer, 0, n_chunks + c)),
            pl.BlockSpec((1, FFN_TF, D_MODEL), lambda i, c: (layer, c, 0)),
        ],
        out_specs=pl.BlockSpec((FFN_TM, D_MODEL), lambda i, c: (i, 0)),
        out_shape=jax.ShapeDtypeStruct((ROWS, D_MODEL), F32),
        scratch_shapes=[
            pltpu.VMEM((FFN_TM, D_MODEL), F32),
            pltpu.VMEM((FFN_TM, D_MODEL), BF16),
            pltpu.VMEM((FFN_TM, D_MODEL), F32),
        ],
        compiler_params=pltpu.CompilerParams(
            dimension_semantics=("parallel", "arbitrary"), vmem_limit_bytes=VMEM_LIMIT),
        name="mixer_out_ffn",
    )(o, x2, w_attn_out_bf16, gains, mod, w_in_layers, w_in_layers, w_out_layers)


def kernel(x, c, rel_bias, ada_w, ada_b, norm_g, moba_w_qkv, moba_w_o, diff_w_qkv, diff_w_o, diff_lambda,
           diff_subln_g, ffn_w_in, ffn_w_out):
    x2 = x.reshape(ROWS, D_MODEL)
    mod = _adaln_mod(c, ada_w, ada_b).reshape(DEPTH, BATCH, 6, D_MODEL)
    bias = _bias_tiles(rel_bias)
    for i in range(DEPTH):
        if i % 2 == 0:
            qkv, kmean = _qkv_proj(x2, norm_g, mod, i, moba_w_qkv, i // 2, HEAD_DIM ** -0.5 * LOG2_E, True)
            o = _moba_attention(qkv, kmean.reshape(BATCH * N_KV_BLOCKS, D_MODEL), bias)
            w_o = moba_w_o
        else:
            (qkv,) = _qkv_proj(x2, norm_g, mod, i, diff_w_qkv, i // 2, DIFF_HALF ** -0.5 * LOG2_E, False)
            o = _diff_attention(qkv, bias, diff_lambda[i // 2], diff_subln_g[i // 2], i)
            w_o = diff_w_o
        x2 = _mixer_out_ffn(o, x2, w_o[i // 2].astype(BF16), norm_g, mod, i, ffn_w_in, ffn_w_out)
    return x2.reshape(BATCH, SEQ, D_MODEL)
```

```python
import functools
import math

import numpy as np
import jax
import jax.numpy as jnp
from jax import lax
from jax.experimental import pallas as pl
from jax.experimental.pallas import tpu as pltpu

D_MODEL = 1024
BATCH = 2
SEQ = 8192
DEPTH = 2
N_HEADS = 8
HEAD_DIM = D_MODEL // N_HEADS
DIFF_HALF = HEAD_DIM // 2
MOBA_BLOCK = 256
MOBA_TOPK = 3
D_FF = 2816
N_BUCKETS = 32
MAX_EXACT = N_BUCKETS // 2
MAX_DISTANCE = 2048
RMS_EPS = 1e-6
NEG_INF = -1e30

ROWS = BATCH * SEQ
N_KV_BLOCKS = SEQ // MOBA_BLOCK
TILE = MOBA_BLOCK
VMEM_LIMIT = 48 * 1024 * 1024

F32 = jnp.float32
BF16 = jnp.bfloat16


def _bucket_of_distance():
    n = np.arange(MAX_DISTANCE + 1)
    nf = np.maximum(n, 1).astype(np.float64)
    val = np.log(nf / MAX_EXACT) / math.log(MAX_DISTANCE / MAX_EXACT) * (N_BUCKETS - MAX_EXACT)
    frac = np.abs(val - np.round(val))
    assert np.all((frac > 5e-5) | (n <= MAX_EXACT) | (n == MAX_DISTANCE))
    large = np.minimum(MAX_EXACT + np.floor(val + 1e-9).astype(np.int64), N_BUCKETS - 1)
    return np.where(n < MAX_EXACT, n, large)


_BUCKETS = _bucket_of_distance()
FAR_DISTANCE = int(np.min(np.nonzero(_BUCKETS == N_BUCKETS - 1)[0]))
N_BIAS_TILES = (FAR_DISTANCE + TILE - 1) // TILE + 1
N_TABLE_TILES = N_BIAS_TILES + 2


def _bias_bucket_tiles():
    i = np.arange(TILE)[:, None]
    j = np.arange(TILE)[None, :]
    out = []
    for t in range(-1, N_BIAS_TILES + 1):
        rel = t * TILE + i - j
        out.append(np.where(rel >= 0, _BUCKETS[np.clip(rel, 0, MAX_DISTANCE)], -1))
    return np.stack(out).astype(np.int32)


def _table_index(tiles_behind):
    return jnp.clip(tiles_behind, -1, N_BIAS_TILES) + 1


MOD_TN = 1024


def _mod_kernel(ct_ref, w_ref, b_ref, o_ref):
    ct = ct_ref[...]
    cact = ct / (1.0 + jnp.exp(-ct))
    w = w_ref[0]
    for b in range(BATCH):
        row = jnp.sum(w * cact[:, b:b + 1], axis=0, keepdims=True)
        o_ref[0, b:b + 1, :] = row + b_ref[0]


def _adaln_mod(c, ada_w, ada_b):
    n_out = ada_w.shape[-1]
    return pl.pallas_call(
        _mod_kernel,
        grid=(DEPTH, n_out // MOD_TN),
        in_specs=[
            pl.BlockSpec((D_MODEL, BATCH), lambda i, n: (0, 0)),
            pl.BlockSpec((1, D_MODEL, MOD_TN), lambda i, n: (i, 0, n)),
            pl.BlockSpec((1, 1, MOD_TN), lambda i, n: (i, 0, n)),
        ],
        out_specs=pl.BlockSpec((1, BATCH, MOD_TN), lambda i, n: (i, 0, n)),
        out_shape=jax.ShapeDtypeStruct((DEPTH, BATCH, n_out), F32),
        name="adaln_mod",
    )(c.T, ada_w, ada_b.reshape(DEPTH, 1, n_out))


LOG2_E = math.log2(math.e)


def _bias_kernel(tab_ref, bkt_ref, o_ref, *, buckets_present):
    h = pl.program_id(0)
    far = tab_ref[N_BUCKETS - 1, h]
    for t, present in enumerate(buckets_present):
        bkt = bkt_ref[t]
        acc = jnp.full(bkt.shape, NEG_INF, F32)
        for b in present:
            acc = jnp.where(bkt == b, (tab_ref[b, h] - far) * LOG2_E, acc)
        o_ref[0, t] = acc


def _bias_tiles(rel_bias):
    bkt = _bias_bucket_tiles()
    buckets_present = tuple(tuple(int(b) for b in np.unique(tile) if b >= 0) for tile in bkt)
    return pl.pallas_call(
        functools.partial(_bias_kernel, buckets_present=buckets_present),
        grid=(N_HEADS,),
        in_specs=[
            pl.BlockSpec(memory_space=pltpu.SMEM),
            pl.BlockSpec((N_TABLE_TILES, TILE, TILE), lambda h: (0, 0, 0)),
        ],
        out_specs=pl.BlockSpec((1, N_TABLE_TILES, TILE, TILE), lambda h: (h, 0, 0, 0)),
        out_shape=jax.ShapeDtypeStruct((N_HEADS, N_TABLE_TILES, TILE, TILE), F32),
        name="bias_tiles",
    )(rel_bias, jnp.asarray(bkt))


QKV_TM = 1024


def _norm_mod(x, g, sc, sh):
    y = x * lax.rsqrt(jnp.mean(x * x, axis=-1, keepdims=True) + RMS_EPS)
    return (y * g) * (1.0 + sc) + sh


def _cast_weight_once(w_ref, wb_ref):
    @pl.when(pl.program_id(0) == 0)
    def _():
        wb_ref[...] = w_ref[0].astype(BF16)


GAIN_MIXER_IN, GAIN_MIXER_OUT, GAIN_FFN_IN, GAIN_FFN_OUT = range(4)
MOD_SHIFT_MIXER, MOD_SCALE_MIXER, MOD_GATE_MIXER, MOD_SHIFT_FFN, MOD_SCALE_FFN, MOD_GATE_FFN = range(6)


def _gain(gains_ref, which):
    return gains_ref[0, which:which + 1, :]


def _mod(mod_ref, which):
    return mod_ref[0, 0, which:which + 1, :]


def _layer_vector_specs(layer, rows_per_tile):
    tiles_per_batch = SEQ // rows_per_tile
    return (pl.BlockSpec((1, 4, D_MODEL), lambda i, *_: (layer, 0, 0)),
            pl.BlockSpec((1, 1, 6, D_MODEL), lambda i, *_: (layer, i // tiles_per_batch, 0, 0)))


def _qkv_kernel(x_ref, gains_ref, mod_ref, w_ref, o_ref, *rest, q_scale):
    *km_ref, wb_ref = rest
    _cast_weight_once(w_ref, wb_ref)
    hb = _norm_mod(x_ref[...], _gain(gains_ref, GAIN_MIXER_IN), _mod(mod_ref, MOD_SCALE_MIXER),
                   _mod(mod_ref, MOD_SHIFT_MIXER)).astype(BF16)
    for n in range(3):
        r = jnp.dot(hb, wb_ref[:, n * D_MODEL:(n + 1) * D_MODEL], preferred_element_type=F32)
        if n == 0:
            r = r * q_scale
        if n == 1 and km_ref:
            km_ref[0][0] = jnp.mean(r.reshape(QKV_TM // MOBA_BLOCK, MOBA_BLOCK, D_MODEL), axis=1)
        o_ref[:, n * D_MODEL:(n + 1) * D_MODEL] = r.astype(BF16)


def _qkv_proj(x2, gains, mod, layer, w_layers, w_layer, q_scale, with_kmean):
    n_tiles = ROWS // QKV_TM
    out_shape = [jax.ShapeDtypeStruct((ROWS, 3 * D_MODEL), BF16)]
    out_specs = [pl.BlockSpec((QKV_TM, 3 * D_MODEL), lambda i: (i, 0))]
    if with_kmean:
        per_tile = QKV_TM // MOBA_BLOCK
        out_shape.append(jax.ShapeDtypeStruct((n_tiles, per_tile, D_MODEL), F32))
        out_specs.append(pl.BlockSpec((1, per_tile, D_MODEL), lambda i: (i, 0, 0)))
    return pl.pallas_call(
        functools.partial(_qkv_kernel, q_scale=q_scale),
        grid=(n_tiles,),
        in_specs=[
            pl.BlockSpec((QKV_TM, D_MODEL), lambda i: (i, 0)),
            *_layer_vector_specs(layer, QKV_TM),
            pl.BlockSpec((1, D_MODEL, 3 * D_MODEL), lambda i: (w_layer, 0, 0), pipeline_mode=pl.Buffered(1)),
        ],
        out_specs=out_specs,
        out_shape=out_shape,
        scratch_shapes=[pltpu.VMEM((D_MODEL, 3 * D_MODEL), BF16)],
        compiler_params=pltpu.CompilerParams(
            dimension_semantics=("arbitrary",), vmem_limit_bytes=VMEM_LIMIT),
        name="qkv_moba" if with_kmean else "qkv_diff",
    )(x2, gains, mod, w_layers)


_NT = (((1,), (1,)), ((), ()))
GROUP = 4
GROUP_KEYS = GROUP * TILE
N_GROUPS = SEQ // GROUP_KEYS
GROUPS_PER_STEP = 4
N_STEPS = N_GROUPS // GROUPS_PER_STEP
STEP_KEYS = GROUPS_PER_STEP * GROUP_KEYS
ROW_TILE = 2 * TILE
N_ROW_TILES = SEQ // ROW_TILE
ROW_TILES_PER_GROUP = GROUP_KEYS // ROW_TILE
SOFTMAX_ROWS = 64
BIAS_FREE_FROM_ROW_TILE = -(-(N_BIAS_TILES + GROUP - 1) // (ROW_TILE // TILE))


def _flash_scratch():
    scores = pltpu.VMEM((ROW_TILE, GROUP_KEYS), F32)
    probs = pltpu.VMEM((ROW_TILE, GROUP_KEYS), BF16)
    rescale = pltpu.VMEM((ROW_TILE, 1), F32)
    values = pltpu.VMEM((2, GROUP_KEYS, 2 * HEAD_DIM), BF16)
    running_max = pltpu.VMEM((SEQ, 1), F32)
    acc = pltpu.VMEM((SEQ, 2 * HEAD_DIM), F32)
    return [scores, scores, probs, probs, rescale, rescale, values, running_max, acc]


def _tile_rows(tile):
    return pl.ds(pl.multiple_of(tile * ROW_TILE, ROW_TILE), ROW_TILE)


def _flash_sweep(grp, qa_ref, group_keys, group_values, bias_ref,
                 s0_ref, s1_ref, p0_ref, p1_ref, alpha0_ref, alpha1_ref, va_ref, m_ref, acc_ref):
    half = GROUP_KEYS // 2
    first = grp * ROW_TILES_PER_GROUP
    last = N_ROW_TILES - 1
    va_new = va_ref.at[grp % 2]
    va_old = va_ref.at[1 - grp % 2]
    va_new[:, :HEAD_DIM] = group_values()

    @pl.when(grp < 2)
    def _():
        va_new[:, HEAD_DIM:] = jnp.ones((GROUP_KEYS, HEAD_DIM), BF16)

    def scores(tile, s_ref, n_keys=GROUP_KEYS, biased=True):
        raw = lax.dot_general(qa_ref[_tile_rows(tile), :], group_keys(n_keys), _NT,
                              preferred_element_type=F32)
        if not biased:
            s_ref[:, :n_keys] = raw
            return
        for r in range(ROW_TILE // TILE):
            for j in range(n_keys // TILE):
                bias = bias_ref[0, _table_index(tile * (ROW_TILE // TILE) + r - (grp * GROUP + j))]
                s_ref[r * TILE:(r + 1) * TILE, j * TILE:(j + 1) * TILE] = (
                    raw[r * TILE:(r + 1) * TILE, j * TILE:(j + 1) * TILE] + bias)

    def softmax(tile, s_ref, p_ref, alpha_ref, n_keys=GROUP_KEYS, fresh=False):
        for c in range(ROW_TILE // SOFTMAX_ROWS):
            chunk = slice(c * SOFTMAX_ROWS, (c + 1) * SOFTMAX_ROWS)
            rows = pl.ds(pl.multiple_of(tile * ROW_TILE + c * SOFTMAX_ROWS, SOFTMAX_ROWS), SOFTMAX_ROWS)
            s = s_ref[chunk, :n_keys]
            m_new = jnp.max(s, axis=1, keepdims=True)
            if fresh:
                alpha_ref[chunk, :] = jnp.zeros((SOFTMAX_ROWS, 1), F32)
            else:
                m_prev = m_ref[rows, :]
                m_new = jnp.maximum(m_prev, m_new)
                alpha_ref[chunk, :] = jnp.exp2(m_prev - m_new)
            m_ref[rows, :] = m_new
            p_ref[chunk, :n_keys] = jnp.exp2(s - m_new).astype(BF16)

    def accumulate(tile, p_ref, alpha_ref, va, n_keys=GROUP_KEYS, fresh=False):
        rows = _tile_rows(tile)
        update = jnp.dot(p_ref[:, :n_keys], va[:n_keys, :], preferred_element_type=F32)
        acc_ref[rows, :] = update if fresh else alpha_ref[...] * acc_ref[rows, :] + update

    def enter(with_pending):
        scores(first, s0_ref, half)
        if with_pending:
            softmax(last, s1_ref, p1_ref, alpha1_ref)
            accumulate(last - 1, p0_ref, alpha0_ref, va_old)
        scores(first + 1, s1_ref)
        softmax(first, s0_ref, p0_ref, alpha0_ref, half, fresh=not with_pending)
        if with_pending:
            accumulate(last, p1_ref, alpha1_ref, va_old)

    def second_pair(fresh=False):
        scores(first + 2, s0_ref)
        softmax(first + 1, s1_ref, p1_ref, alpha1_ref, fresh=fresh)
        accumulate(first, p0_ref, alpha0_ref, va_new, half, fresh=fresh)
        scores(first + 3, s1_ref)
        softmax(first + 2, s0_ref, p0_ref, alpha0_ref, fresh=fresh)
        accumulate(first + 1, p1_ref, alpha1_ref, va_new, fresh=fresh)

    def pair(t, first_biased=False, values_first=False, fresh=False):
        tile = first + 3 + 2 * t
        if values_first:
            accumulate(tile - 1, p0_ref, alpha0_ref, va_new, fresh=fresh)
        scores(tile + 1, s0_ref, biased=first_biased)
        softmax(tile, s1_ref, p1_ref, alpha1_ref, fresh=fresh)
        if not values_first:
            accumulate(tile - 1, p0_ref, alpha0_ref, va_new, fresh=fresh)
        else:
            accumulate(tile, p1_ref, alpha1_ref, va_new, fresh=fresh)
        scores(tile + 2, s1_ref, biased=False)
        softmax(tile + 1, s0_ref, p0_ref, alpha0_ref, fresh=fresh)
        if not values_first:
            accumulate(tile, p1_ref, alpha1_ref, va_new, fresh=fresh)

    assert BIAS_FREE_FROM_ROW_TILE == 5
    n_pairs = jnp.maximum(last - first - 3, 0) // 2
    middle = (grp > 0) & (grp < N_GROUPS - 1)

    def first_pairs(even_count, fresh=False):
        pair(0, first_biased=True, values_first=even_count, fresh=fresh)
        if even_count:
            pair(1, values_first=True, fresh=fresh)

    @pl.when(grp == 0)
    def _():
        assert ((N_ROW_TILES - 4) // 2) % 2 == 0
        enter(False)
        second_pair(fresh=True)
        first_pairs(True, fresh=True)

    @pl.when(middle & (n_pairs % 2 == 1))
    def _():
        enter(True)
        second_pair()
        first_pairs(False)

    @pl.when(middle & (n_pairs > 0) & (n_pairs % 2 == 0))
    def _():
        enter(True)
        second_pair()
        first_pairs(True)

    @pl.when(middle & (n_pairs == 0))
    def _():
        enter(True)
        second_pair()

    @pl.when(grp == N_GROUPS - 1)
    def _():
        enter(True)
        softmax(last, s1_ref, p1_ref, alpha1_ref)
        accumulate(last - 1, p0_ref, alpha0_ref, va_new, half)
        accumulate(last, p1_ref, alpha1_ref, va_new)

    def two_pairs(u, carry, fresh):
        t = 2 - n_pairs % 2 + 2 * u
        pair(t, values_first=True, fresh=fresh)
        pair(t + 1, values_first=True, fresh=fresh)
        return carry

    n_trips = jnp.maximum(n_pairs - 1, 0) // 2
    lax.fori_loop(0, jnp.where(grp == 0, n_trips, 0), functools.partial(two_pairs, fresh=True), 0)
    lax.fori_loop(0, jnp.where(grp == 0, 0, n_trips), functools.partial(two_pairs, fresh=False), 0)


def _reset_pending_state(m_ref, acc_ref):
    rows = pl.ds(SEQ - 2 * ROW_TILE, 2 * ROW_TILE)
    m_ref[rows, :] = jnp.full((rows.size, 1), NEG_INF, F32)
    acc_ref[rows, :] = jnp.zeros((rows.size, 2 * HEAD_DIM), F32)


def _attention_of(acc):
    return acc[:, :HEAD_DIM] / acc[:, HEAD_DIM:]


GATE_QUERIES = 8 * TILE


def _moba_gating(q_ref, km_ref, qa_ref):
    blk = lax.broadcasted_iota(jnp.int32, (N_KV_BLOCKS, GATE_QUERIES), 0)
    blkf = blk.astype(F32)
    tile_in_step = lax.broadcasted_iota(jnp.int32, (N_KV_BLOCKS, GATE_QUERIES), 1) // TILE

    def gate_step(step, carry):
        rows = pl.ds(pl.multiple_of(step * GATE_QUERIES, GATE_QUERIES), GATE_QUERIES)
        q = q_ref[rows, :]
        gate = lax.dot_general(km_ref[...].astype(BF16), q, _NT, preferred_element_type=F32)
        tile = step * (GATE_QUERIES // TILE) + tile_in_step
        past = blk < tile
        g = jnp.where(past, gate, NEG_INF)
        madd = jnp.full(gate.shape, NEG_INF, F32)
        for _ in range(MOBA_TOPK):
            mx = jnp.max(g, axis=0, keepdims=True)
            first = jnp.min(jnp.where(g == mx, blkf, float(N_KV_BLOCKS)), axis=0, keepdims=True)
            hit = blkf == first
            madd = jnp.where(hit & past, 0.0, madd)
            g = jnp.where(hit, -jnp.inf, g)
        madd = jnp.where(blk == tile, 0.0, madd)
        madd = jnp.concatenate([madd, jnp.zeros((HEAD_DIM - N_KV_BLOCKS, GATE_QUERIES), F32)], axis=0)
        qa_ref[rows, :HEAD_DIM] = q
        qa_ref[rows, HEAD_DIM:] = madd.T.astype(BF16)
        return carry

    lax.fori_loop(0, SEQ // GATE_QUERIES, gate_step, 0, unroll=2)


def _step_rows(group_in_step, n_rows=GROUP_KEYS):
    return pl.ds(pl.multiple_of(group_in_step * GROUP_KEYS, GROUP_KEYS), n_rows)


def _moba_kernel(q_ref, k_ref, v_ref, km_ref, bias_ref, o_ref, qa_ref, ka_ref, *flash_refs):
    step = pl.program_id(2)
    m_ref, acc_ref = flash_refs[-2:]

    @pl.when(step == 0)
    def _():
        _reset_pending_state(m_ref, acc_ref)
        _moba_gating(q_ref, km_ref, qa_ref)

    lane = lax.broadcasted_iota(jnp.int32, (GROUP_KEYS, HEAD_DIM), 1)
    block_in_group = lax.broadcasted_iota(jnp.int32, (GROUP_KEYS, HEAD_DIM), 0) // TILE

    def group(i, carry):
        grp = step * GROUPS_PER_STEP + i
        ka_ref[:, :HEAD_DIM] = k_ref[_step_rows(i), :]
        ka_ref[:, HEAD_DIM:] = jnp.where(lane == grp * GROUP + block_in_group, 1.0, 0.0).astype(BF16)
        _flash_sweep(grp, qa_ref, lambda n: ka_ref[:n, :], lambda: v_ref[_step_rows(i), :], bias_ref,
                     *flash_refs)
        return carry

    lax.fori_loop(0, GROUPS_PER_STEP, group, 0)

    @pl.when(step == N_STEPS - 1)
    def _():
        def finish(tile, carry):
            rows = _tile_rows(tile)
            o_ref[rows, :] = _attention_of(acc_ref[rows, :]).astype(BF16)
            return carry

        lax.fori_loop(0, N_ROW_TILES, finish, 0)


def _moba_attention(qkv, kmean, bias):
    return pl.pallas_call(
        _moba_kernel,
        grid=(BATCH, N_HEADS, N_STEPS),
        in_specs=[
            pl.BlockSpec((SEQ, HEAD_DIM), lambda b, h, g: (b, h)),
            pl.BlockSpec((STEP_KEYS, HEAD_DIM), lambda b, h, g: (b * N_STEPS + g, N_HEADS + h)),
            pl.BlockSpec((STEP_KEYS, HEAD_DIM), lambda b, h, g: (b * N_STEPS + g, 2 * N_HEADS + h)),
            pl.BlockSpec((N_KV_BLOCKS, HEAD_DIM), lambda b, h, g: (b, h)),
            pl.BlockSpec((1, N_TABLE_TILES, TILE, TILE), lambda b, h, g: (h, 0, 0, 0)),
        ],
        out_specs=pl.BlockSpec((SEQ, HEAD_DIM), lambda b, h, g: (b, h)),
        out_shape=jax.ShapeDtypeStruct((ROWS, D_MODEL), BF16),
        scratch_shapes=[
            pltpu.VMEM((SEQ, 2 * HEAD_DIM), BF16),
            pltpu.VMEM((GROUP_KEYS, 2 * HEAD_DIM), BF16),
        ] + _flash_scratch(),
        compiler_params=pltpu.CompilerParams(
            dimension_semantics=("arbitrary", "arbitrary", "arbitrary"), vmem_limit_bytes=VMEM_LIMIT),
        name="moba_attn",
    )(qkv, qkv, qkv, kmean, bias)


def _diff_kernel(q_ref, k_ref, v_ref, bias_ref, lam_ref, g_ref, o_ref, qa_ref, map0_ref, *flash_refs,
                 lambda_init):
    which = pl.program_id(2)
    step = pl.program_id(3)
    m_ref, acc_ref = flash_refs[-2:]

    @pl.when(step == 0)
    def _():
        lane_map = lax.broadcasted_iota(jnp.int32, (ROW_TILE, HEAD_DIM), 1) // DIFF_HALF
        _reset_pending_state(m_ref, acc_ref)

        def split(tile, carry):
            rows = _tile_rows(tile)
            q = q_ref[rows, :]
            qa_ref[rows, :] = jnp.where(lane_map == which, q, jnp.zeros_like(q))
            return carry

        lax.fori_loop(0, N_ROW_TILES, split, 0)

    def group(i, carry):
        _flash_sweep(step * GROUPS_PER_STEP + i, qa_ref, lambda n: k_ref[_step_rows(i, n), :],
                     lambda: v_ref[_step_rows(i), :], bias_ref, *flash_refs)
        return carry

    lax.fori_loop(0, GROUPS_PER_STEP, group, 0)

    @pl.when((step == N_STEPS - 1) & (which == 0))
    def _():
        def keep(tile, carry):
            rows = _tile_rows(tile)
            map0_ref[rows, :] = _attention_of(acc_ref[rows, :])
            return carry

        lax.fori_loop(0, N_ROW_TILES, keep, 0)

    @pl.when((step == N_STEPS - 1) & (which == 1))
    def _():
        lam = lam_ref[...]
        lam_full = (jnp.exp(jnp.sum(lam[0:1] * lam[1:2], axis=1, keepdims=True))
                    - jnp.exp(jnp.sum(lam[2:3] * lam[3:4], axis=1, keepdims=True)) + lambda_init)

        def finish(tile, carry):
            rows = _tile_rows(tile)
            o = map0_ref[rows, :] - lam_full * _attention_of(acc_ref[rows, :])
            o = o * lax.rsqrt(jnp.mean(o * o, axis=-1, keepdims=True) + RMS_EPS) * g_ref[...]
            o_ref[rows, :] = (o * (1.0 - lambda_init)).astype(BF16)
            return carry

        lax.fori_loop(0, N_ROW_TILES, finish, 0, unroll=2)


def _diff_attention(qkv, bias, lam, subln_g, layer_idx):
    lambda_init = 0.8 - 0.6 * math.exp(-0.3 * layer_idx)
    return pl.pallas_call(
        functools.partial(_diff_kernel, lambda_init=lambda_init),
        grid=(BATCH, N_HEADS, 2, N_STEPS),
        in_specs=[
            pl.BlockSpec((SEQ, HEAD_DIM), lambda b, h, c, g: (b, h)),
            pl.BlockSpec((STEP_KEYS, HEAD_DIM), lambda b, h, c, g: (b * N_STEPS + g, N_HEADS + h)),
            pl.BlockSpec((STEP_KEYS, HEAD_DIM), lambda b, h, c, g: (b * N_STEPS + g, 2 * N_HEADS + h)),
            pl.BlockSpec((1, N_TABLE_TILES, TILE, TILE), lambda b, h, c, g: (h, 0, 0, 0)),
            pl.BlockSpec((4, DIFF_HALF), lambda b, h, c, g: (0, 0)),
            pl.BlockSpec((1, HEAD_DIM), lambda b, h, c, g: (0, 0)),
        ],
        out_specs=pl.BlockSpec((SEQ, HEAD_DIM), lambda b, h, c, g: (b, h)),
        out_shape=jax.ShapeDtypeStruct((ROWS, D_MODEL), BF16),
        scratch_shapes=[
            pltpu.VMEM((SEQ, HEAD_DIM), BF16),
            pltpu.VMEM((SEQ, HEAD_DIM), F32),
        ] + _flash_scratch(),
        compiler_params=pltpu.CompilerParams(
            dimension_semantics=("arbitrary", "arbitrary", "arbitrary", "arbitrary"),
            vmem_limit_bytes=VMEM_LIMIT),
        name="diff_attn",
    )(qkv, qkv, qkv, bias, lam, subln_g.reshape(1, HEAD_DIM))


FFN_TM = 1024
FFN_TF = 256


def _rms(y, g):
    return y * lax.rsqrt(jnp.mean(y * y, axis=-1, keepdims=True) + RMS_EPS) * g


WEIGHT_SLOTS = 3


def _mixer_out_ffn_kernel(o_ref, x_ref, wa_ref, gains_ref, mod_ref, w_in_hbm, w_out_hbm, out_ref,
                          xn_ref, h_ref, acc_ref, wg_buf, wu_buf, wo_buf, sems, *, layer, n_row_tiles, n_chunks):
    c = pl.program_id(1)
    last = n_chunks - 1
    step = pl.program_id(0) * n_chunks + c
    n_steps = n_row_tiles * n_chunks

    def weight_copies(s):
        slot = s % WEIGHT_SLOTS
        col = (s % n_chunks) * FFN_TF
        if not isinstance(s, int):
            col = pl.multiple_of(col, FFN_TF)
        return (
            pltpu.make_async_copy(w_in_hbm.at[layer, :, pl.ds(col, FFN_TF)], wg_buf.at[slot], sems.at[0, slot]),
            pltpu.make_async_copy(w_in_hbm.at[layer, :, pl.ds(D_FF + col, FFN_TF)], wu_buf.at[slot],
                                  sems.at[1, slot]),
            pltpu.make_async_copy(w_out_hbm.at[layer, pl.ds(col, FFN_TF), :], wo_buf.at[slot], sems.at[2, slot]),
        )

    @pl.when(step == 0)
    def _():
        for s in range(WEIGHT_SLOTS - 1):
            for copy in weight_copies(s):
                copy.start()

    @pl.when(step + WEIGHT_SLOTS - 1 < n_steps)
    def _():
        for copy in weight_copies(step + WEIGHT_SLOTS - 1):
            copy.start()

    for copy in weight_copies(step):
        copy.wait()
    slot = step % WEIGHT_SLOTS

    def chunk(h):
        gp = jnp.dot(h, wg_buf[slot].astype(BF16), preferred_element_type=F32)
        up = jnp.dot(h, wu_buf[slot].astype(BF16), preferred_element_type=F32)
        act = (gp / (1.0 + jnp.exp(-gp)) * up).astype(BF16)
        return jnp.dot(act, wo_buf[slot].astype(BF16), preferred_element_type=F32)

    @pl.when(c == 0)
    def _():
        y = jnp.dot(o_ref[...], wa_ref[...], preferred_element_type=F32)
        xn = x_ref[...] + _mod(mod_ref, MOD_GATE_MIXER) * _rms(y, _gain(gains_ref, GAIN_MIXER_OUT))
        xn_ref[...] = xn
        h = _norm_mod(xn, _gain(gains_ref, GAIN_FFN_IN), _mod(mod_ref, MOD_SCALE_FFN),
                      _mod(mod_ref, MOD_SHIFT_FFN)).astype(BF16)
        h_ref[...] = h
        acc_ref[...] = chunk(h)

    @pl.when((c > 0) & (c < last))
    def _():
        acc_ref[...] += chunk(h_ref[...])

    @pl.when(c == last)
    def _():
        y = acc_ref[...] + chunk(h_ref[...])
        out_ref[...] = xn_ref[...] + _mod(mod_ref, MOD_GATE_FFN) * _rms(y, _gain(gains_ref, GAIN_FFN_OUT))


def _mixer_out_ffn(o, x2, w_attn_out_bf16, gains, mod, layer, w_in_layers, w_out_layers):
    n_chunks = D_FF // FFN_TF
    n_row_tiles = ROWS // FFN_TM
    return pl.pallas_call(
        functools.partial(_mixer_out_ffn_kernel, layer=layer, n_row_tiles=n_row_tiles, n_chunks=n_chunks),
        grid=(n_row_tiles, n_chunks),
        in_specs=[
            pl.BlockSpec((FFN_TM, D_MODEL), lambda i, c: (i, 0)),
            pl.BlockSpec((FFN_TM, D_MODEL), lambda i, c: (i, 0)),
            pl.BlockSpec((D_MODEL, D_MODEL), lambda i, c: (0, 0), pipeline_mode=pl.Buffered(1)),
            *_layer_vector_specs(layer, FFN_TM),
            pl.BlockSpec(memory_space=pl.ANY),
            pl.BlockSpec(memory_space=pl.ANY),
        ],
        out_specs=pl.BlockSpec((FFN_TM, D_MODEL), lambda i, c: (i, 0)),
        out_shape=jax.ShapeDtypeStruct((ROWS, D_MODEL), F32),
        scratch_shapes=[
            pltpu.VMEM((FFN_TM, D_MODEL), F32),
            pltpu.VMEM((FFN_TM, D_MODEL), BF16),
            pltpu.VMEM((FFN_TM, D_MODEL), F32),
            pltpu.VMEM((WEIGHT_SLOTS, D_MODEL, FFN_TF), F32),
            pltpu.VMEM((WEIGHT_SLOTS, D_MODEL, FFN_TF), F32),
            pltpu.VMEM((WEIGHT_SLOTS, FFN_TF, D_MODEL), F32),
            pltpu.SemaphoreType.DMA((3, WEIGHT_SLOTS)),
        ],
        compiler_params=pltpu.CompilerParams(
            dimension_semantics=("arbitrary", "arbitrary"), vmem_limit_bytes=VMEM_LIMIT),
        name="mixer_out_ffn",
    )(o, x2, w_attn_out_bf16, gains, mod, w_in_layers, w_out_layers)


def kernel(x, c, rel_bias, ada_w, ada_b, norm_g, moba_w_qkv, moba_w_o, diff_w_qkv, diff_w_o, diff_lambda,
           diff_subln_g, ffn_w_in, ffn_w_out):
    x2 = x.reshape(ROWS, D_MODEL)
    mod = _adaln_mod(c, ada_w, ada_b).reshape(DEPTH, BATCH, 6, D_MODEL)
    bias = _bias_tiles(rel_bias)
    for i in range(DEPTH):
        if i % 2 == 0:
            qkv, kmean = _qkv_proj(x2, norm_g, mod, i, moba_w_qkv, i // 2, HEAD_DIM ** -0.5 * LOG2_E, True)
            o = _moba_attention(qkv, kmean.reshape(BATCH * N_KV_BLOCKS, D_MODEL), bias)
            w_o = moba_w_o
        else:
            (qkv,) = _qkv_proj(x2, norm_g, mod, i, diff_w_qkv, i // 2, DIFF_HALF ** -0.5 * LOG2_E, False)
            o = _diff_attention(qkv, bias, diff_lambda[i // 2], diff_subln_g[i // 2], i)
            w_o = diff_w_o
        x2 = _mixer_out_ffn(o, x2, w_o[i // 2].astype(BF16), norm_g, mod, i, ffn_w_in, ffn_w_out)
    return x2.reshape(BATCH, SEQ, D_MODEL)
```
